```python
import jax, jax.numpy as jnp
from jax import lax
import numpy as np

D_MODEL = 2048
BATCH = 1
SEQ = 8192
DEPTH = 1
DEC_BATCH = 32
DEC_SEQ = 1
PAST_LEN = 8192
PAGE_SIZE = 128

PLE_DIM = 256
HEAD_DIM = 64
ATTN_WIDTH = D_MODEL // 2
N_ATTN_HEADS = ATTN_WIDTH // HEAD_DIM
DILATED_PATTERNS = ((128, 1), (512, 4), (2048, 16))
N_PATTERNS = len(DILATED_PATTERNS)
BAND = 128
POOL_WINDOWS = (2, 4, 8, 16)
POOL_WIDTH = D_MODEL - ATTN_WIDTH
POOL_GROUP = POOL_WIDTH // len(POOL_WINDOWS)
POOL_STATE = max(POOL_WINDOWS) - 1
ROT_DIM = HEAD_DIM // 4
ROPE_THETA = 500000.0
QKV_WIDTH = N_PATTERNS * 3 * ATTN_WIDTH
IN_WIDTH = QKV_WIDTH + POOL_WIDTH
N_GROUPS = 4
EXPERTS_PER_GROUP = 8
N_EXPERTS = N_GROUPS * EXPERTS_PER_GROUP
TOP_K_IN_GROUP = 2
D_FF_EXPERT = 256
ALPHA = (2.0 * DEPTH) ** 0.25
BETA = (8.0 * DEPTH) ** -0.25
LN_EPS = 1e-5
NEG_INF = -1e30

kernel_name = "hymba_pool_dilated_hmoe_decoder_step"


def layer_norm(x, g, b):
    xf = x.astype(jnp.float32)
    mu = jnp.mean(xf, -1, keepdims=True)
    var = jnp.mean(jnp.square(xf - mu), -1, keepdims=True)
    return ((xf - mu) * lax.rsqrt(var + LN_EPS) * g + b).astype(x.dtype)


def partial_rope(x, pos):
    half = ROT_DIM // 2
    inv_freq = ROPE_THETA ** (-jnp.arange(0, ROT_DIM, 2, dtype=jnp.float32) / ROT_DIM)
    ang = pos.astype(jnp.float32)[:, None] * inv_freq[None, :]
    cos = jnp.cos(ang)[None, :, None, :]
    sin = jnp.sin(ang)[None, :, None, :]
    xf = x.astype(jnp.float32)
    x1, x2, rest = xf[..., :half], xf[..., half:ROT_DIM], xf[..., ROT_DIM:]
    out = jnp.concatenate([x1 * cos - x2 * sin, x2 * cos + x1 * sin, rest], -1)
    return out.astype(x.dtype)


def in_proj(x, w_in):
    b, t, _ = x.shape
    z = jnp.einsum('btd,de->bte', x, w_in)
    qkv = z[..., :QKV_WIDTH].reshape(b, t, N_PATTERNS, 3, N_ATTN_HEADS, HEAD_DIM)
    return qkv, z[..., QKV_WIDTH:]


def dilated_attn_prompt(q, k, v, window, dil):
    b, s, h, dh = q.shape
    r_max = window // dil
    span = dil * BAND
    s_pad = -(-s // span) * span
    nb = s_pad // span

    def to_blocks(t):
        t = jnp.pad(t.astype(jnp.float32), ((0, 0), (0, s_pad - s), (0, 0), (0, 0)))
        t = t.reshape(b, s_pad // dil, dil, h, dh).transpose(0, 2, 1, 3, 4)
        return t.reshape(b, dil, nb, BAND, h, dh)

    def with_prev(t):
        prev = jnp.pad(t[:, :, :-1], ((0, 0), (0, 0), (1, 0), (0, 0), (0, 0), (0, 0)))
        return jnp.concatenate([prev, t], axis=3)

    qb = to_blocks(q)
    kk = with_prev(to_blocks(k))
    vv = with_prev(to_blocks(v))
    scores = jnp.einsum('brnqhd,brnkhd->brnhqk', qb, kk) * (HEAD_DIM ** -0.5)
    qi = jnp.arange(BAND)[:, None]
    ki = jnp.arange(2 * BAND)[None, :]
    dist = BAND + qi - ki
    band = (dist >= 0) & (dist <= r_max)
    blk = jnp.arange(nb)[:, None, None]
    valid = band[None] & ((blk > 0) | (ki >= BAND)[None])
    scores = jnp.where(valid[None, None, :, None], scores, NEG_INF)
    m = jnp.max(scores, -1, keepdims=True)
    e = jnp.exp(scores - m)
    den = jnp.sum(e, -1, keepdims=True)
    o = jnp.einsum('brnhqk,brnkhd->brnhqd', e, vv) / den
    lse = (m + jnp.log(den))[..., 0]
    o = o.transpose(0, 2, 4, 1, 3, 5).reshape(b, s_pad, h, dh)[:, :s]
    lse = lse.transpose(0, 2, 4, 1, 3).reshape(b, s_pad, h)[:, :s]
    return o, lse


def dilated_attn_sample(q, kv_all, window, dil):
    t = q.shape[1]
    L = kv_all.shape[1] - t
    r_max = window // dil
    idx = (L + jnp.arange(t))[:, None] - dil * jnp.arange(r_max + 1)[None, :]
    valid = idx >= 0
    g = kv_all[:, jnp.maximum(idx, 0)].astype(jnp.float32)
    kg, vg = g[:, :, :, 0], g[:, :, :, 1]
    scores = jnp.einsum('bqhd,bqkhd->bhqk', q.astype(jnp.float32), kg) * (HEAD_DIM ** -0.5)
    scores = jnp.where(valid[None, None], scores, NEG_INF)
    m = jnp.max(scores, -1, keepdims=True)
    e = jnp.exp(scores - m)
    den = jnp.sum(e, -1, keepdims=True)
    o = jnp.einsum('bhqk,bqkhd->bqhd', e, vg) / den[..., 0].transpose(0, 2, 1)[..., None]
    lse = (m + jnp.log(den))[..., 0].transpose(0, 2, 1)
    return o, lse


def merge_patterns(outs, lses):
    o = jnp.stack(outs, 0)
    w = jax.nn.softmax(jnp.stack(lses, 0), axis=0)
    return jnp.sum(o * w[..., None], 0)


def pool_mix(u_ext, pos, w_pool, pool_scale):
    uf = u_ext.astype(jnp.float32)
    outs = []
    for gi, win in enumerate(POOL_WINDOWS):
        ug = uf[..., gi * POOL_GROUP:(gi + 1) * POOL_GROUP]
        c = jnp.cumsum(ug, axis=1)
        c_shift = jnp.pad(c[:, :-win], ((0, 0), (win, 0), (0, 0)))
        cnt = jnp.minimum(pos + 1, win).astype(jnp.float32)
        mean = (c - c_shift) / cnt[None, :, None]
        outs.append(jnp.einsum('btc,cd->btd', mean - ug, w_pool[gi].astype(jnp.float32)))
    return (jnp.concatenate(outs, -1) * pool_scale).astype(u_ext.dtype)


def prompt_mixer(x, w_in_i, w_pool_i, pool_scale_i):
    b, s, _ = x.shape
    pos = jnp.arange(s, dtype=jnp.int32)
    qkv, u = in_proj(x, w_in_i)
    outs, lses, kv_states = [], [], []
    for pi, (window, dil) in enumerate(DILATED_PATTERNS):
        q = partial_rope(qkv[:, :, pi, 0], pos)
        k = partial_rope(qkv[:, :, pi, 1], pos)
        v = qkv[:, :, pi, 2]
        o, l = dilated_attn_prompt(q, k, v, window, dil)
        outs.append(o)
        lses.append(l)
        keep = min(window, s)
        kv_states.append(jnp.stack([k[:, s - keep:], v[:, s - keep:]], axis=2))
    attn = merge_patterns(outs, lses).reshape(b, s, ATTN_WIDTH).astype(x.dtype)
    pool = pool_mix(u, pos, w_pool_i, pool_scale_i)
    return jnp.concatenate([attn, pool], -1), kv_states, u[:, s - POOL_STATE:]


def sample_mixer(x, kv_bufs, pool_buf, w_in_i, w_pool_i, pool_scale_i):
    b, t, _ = x.shape
    pos = PAST_LEN + jnp.arange(t, dtype=jnp.int32)
    qkv, u = in_proj(x, w_in_i)
    outs, lses, kv_states = [], [], []
    for pi, (window, dil) in enumerate(DILATED_PATTERNS):
        q = partial_rope(qkv[:, :, pi, 0], pos)
        k = partial_rope(qkv[:, :, pi, 1], pos)
        v = qkv[:, :, pi, 2]
        kv_all = jnp.concatenate([kv_bufs[pi], jnp.stack([k, v], axis=2).astype(kv_bufs[pi].dtype)], axis=1)
        o, l = dilated_attn_sample(q, kv_all, window, dil)
        outs.append(o)
        lses.append(l)
        keep = min(window, PAST_LEN + t)
        kv_states.append(kv_all[:, kv_all.shape[1] - keep:])
    attn = merge_patterns(outs, lses).reshape(b, t, ATTN_WIDTH).astype(x.dtype)
    u_ext = jnp.concatenate([pool_buf.astype(u.dtype), u], axis=1)
    pos_ext = PAST_LEN - POOL_STATE + jnp.arange(POOL_STATE + t, dtype=jnp.int32)
    pool = pool_mix(u_ext, pos_ext, w_pool_i, pool_scale_i)[:, POOL_STATE:]
    return jnp.concatenate([attn, pool], -1), kv_states, u_ext[:, t:]


def hier_moe(x, w_group_router, b_group_router, w_expert_router, b_expert_router, w_gate, w_up, w_down):
    b, t, d = x.shape
    xf = x.reshape(b * t, d)
    g_logits = jnp.einsum('nd,dg->ng', xf, w_group_router).astype(jnp.float32) + b_group_router
    g_sel = jnp.argmax(g_logits, -1)
    g_gate = jnp.take_along_axis(jax.nn.softmax(g_logits, -1), g_sel[:, None], -1)
    e_logits = (jnp.einsum('nd,de->ne', xf, w_expert_router).astype(jnp.float32) + b_expert_router)
    e_logits = e_logits.reshape(-1, N_GROUPS, EXPERTS_PER_GROUP)
    e_in = jnp.take_along_axis(e_logits, g_sel[:, None, None], 1)[:, 0]
    top_v, top_i = lax.top_k(e_in, TOP_K_IN_GROUP)
    top_w = jax.nn.softmax(top_v, -1) * g_gate
    e_idx = g_sel[:, None] * EXPERTS_PER_GROUP + top_i
    combine = jnp.sum(jax.nn.one_hot(e_idx, N_EXPERTS, dtype=jnp.float32) * top_w[..., None], 1)
    h = jax.nn.silu(jnp.einsum('nd,edf->nef', xf, w_gate)) * jnp.einsum('nd,edf->nef', xf, w_up)
    y = jnp.einsum('nef,efd->nd', h * combine[..., None].astype(h.dtype), w_down)
    return y.reshape(b, t, d)


def finish_layer(x, mix, p, w_out, ln1_g, ln1_b, w_group_router, b_group_router, w_expert_router,
                 b_expert_router, w_gate, w_up, w_down, ln2_g, ln2_b, w_ple, w_ple_gate):
    h = jnp.einsum('bte,ed->btd', mix, w_out)
    x1 = layer_norm(ALPHA * x + h, ln1_g, ln1_b)
    y = hier_moe(x1, w_group_router, b_group_router, w_expert_router, b_expert_router, w_gate, w_up, w_down)
    x2 = layer_norm(ALPHA * x1 + y, ln2_g, ln2_b)
    gate = jax.nn.sigmoid(jnp.einsum('btd,de->bte', x2, w_ple_gate))
    return x2 + gate * jnp.einsum('btp,pd->btd', p, w_ple)


def setup_inputs(seed: int = 0) -> dict:
    key = jax.random.key(seed)
    ks = jax.random.split(key, 32)
    f32 = jnp.float32

    def nrm(k, shape, scale):
        return jax.random.normal(k, shape, f32) * scale

    x_prompt = nrm(ks[0], (BATCH, SEQ, D_MODEL), 1.0)
    x_sample = nrm(ks[1], (DEC_BATCH, DEC_SEQ, D_MODEL), 1.0)
    caches = []
    for i, (window, dil) in enumerate(DILATED_PATTERNS):
        L = min(window, PAST_LEN)
        caches.append(nrm(ks[2 + i], (DEPTH, DEC_BATCH, L, 2, N_ATTN_HEADS, HEAD_DIM), 1.0))
    state_pool = nrm(ks[5], (DEPTH, DEC_BATCH, POOL_STATE, POOL_WIDTH), 1.0)
    p_prompt = nrm(ks[6], (DEPTH, BATCH, SEQ, PLE_DIM), 1.0)
    p_sample = nrm(ks[7], (DEPTH, DEC_BATCH, DEC_SEQ, PLE_DIM), 1.0)
    v_scale = jnp.array([1.0, 1.0, BETA], f32)
    col_scale = jnp.concatenate([
        jnp.broadcast_to(v_scale[None, :, None], (N_PATTERNS, 3, ATTN_WIDTH)).reshape(-1),
        jnp.ones((POOL_WIDTH,), f32)])
    w_in = nrm(ks[8], (DEPTH, D_MODEL, IN_WIDTH), D_MODEL ** -0.5) * col_scale
    w_out = nrm(ks[9], (DEPTH, D_MODEL, D_MODEL), BETA * D_MODEL ** -0.5)
    w_pool = nrm(ks[10], (DEPTH, len(POOL_WINDOWS), POOL_GROUP, POOL_GROUP), POOL_GROUP ** -0.5)
    pool_scale = 1.0 + nrm(ks[11], (DEPTH, POOL_WIDTH), 0.1)
    ln1_g = 1.0 + nrm(ks[12], (DEPTH, D_MODEL), 0.05)
    ln1_b = nrm(ks[13], (DEPTH, D_MODEL), 0.02)
    w_group_router = nrm(ks[14], (DEPTH, D_MODEL, N_GROUPS), D_MODEL ** -0.5)
    b_group_router = nrm(ks[15], (DEPTH, N_GROUPS), 0.01)
    w_expert_router = nrm(ks[16], (DEPTH, D_MODEL, N_EXPERTS), D_MODEL ** -0.5)
    b_expert_router = nrm(ks[17], (DEPTH, N_EXPERTS), 0.01)
    w_gate = nrm(ks[18], (DEPTH, N_EXPERTS, D_MODEL, D_FF_EXPERT), D_MODEL ** -0.5)
    w_up = nrm(ks[19], (DEPTH, N_EXPERTS, D_MODEL, D_FF_EXPERT), D_MODEL ** -0.5)
    w_down = nrm(ks[20], (DEPTH, N_EXPERTS, D_FF_EXPERT, D_MODEL), BETA * D_FF_EXPERT ** -0.5)
    ln2_g = 1.0 + nrm(ks[21], (DEPTH, D_MODEL), 0.05)
    ln2_b = nrm(ks[22], (DEPTH, D_MODEL), 0.02)
    w_ple = nrm(ks[23], (DEPTH, PLE_DIM, D_MODEL), PLE_DIM ** -0.5)
    w_ple_gate = nrm(ks[24], (DEPTH, D_MODEL, D_MODEL), D_MODEL ** -0.5)
    return {"x_prompt": x_prompt, "x_sample": x_sample,
            "cache_kv_w128_d1": caches[0], "cache_kv_w512_d4": caches[1], "cache_kv_w2048_d16": caches[2],
            "state_pool": state_pool, "p_prompt": p_prompt, "p_sample": p_sample,
            "w_in": w_in, "w_out": w_out, "w_pool": w_pool, "pool_scale": pool_scale,
            "ln1_g": ln1_g, "ln1_b": ln1_b, "w_group_router": w_group_router, "b_group_router": b_group_router,
            "w_expert_router": w_expert_router, "b_expert_router": b_expert_router,
            "w_gate": w_gate, "w_up": w_up, "w_down": w_down, "ln2_g": ln2_g, "ln2_b": ln2_b,
            "w_ple": w_ple, "w_ple_gate": w_ple_gate}


def reference(x_prompt, x_sample, cache_kv_w128_d1, cache_kv_w512_d4, cache_kv_w2048_d16, state_pool,
              p_prompt, p_sample, w_in, w_out, w_pool, pool_scale, ln1_g, ln1_b, w_group_router,
              b_group_router, w_expert_router, b_expert_router, w_gate, w_up, w_down, ln2_g, ln2_b,
              w_ple, w_ple_gate):
    caches = (cache_kv_w128_d1, cache_kv_w512_d4, cache_kv_w2048_d16)
    xp, xs = x_prompt, x_sample
    kvp = [[] for _ in range(N_PATTERNS)]
    kvs = [[] for _ in range(N_PATTERNS)]
    pool_p, pool_s = [], []
    for i in range(DEPTH):
        mix_p, kv_new_p, pool_new_p = prompt_mixer(xp, w_in[i], w_pool[i], pool_scale[i])
        mix_s, kv_new_s, pool_new_s = sample_mixer(xs, [c[i] for c in caches], state_pool[i],
                                                   w_in[i], w_pool[i], pool_scale[i])
        xp = finish_layer(xp, mix_p, p_prompt[i], w_out[i], ln1_g[i], ln1_b[i], w_group_router[i],
                          b_group_router[i], w_expert_router[i], b_expert_router[i], w_gate[i], w_up[i],
                          w_down[i], ln2_g[i], ln2_b[i], w_ple[i], w_ple_gate[i])
        xs = finish_layer(xs, mix_s, p_sample[i], w_out[i], ln1_g[i], ln1_b[i], w_group_router[i],
                          b_group_router[i], w_expert_router[i], b_expert_router[i], w_gate[i], w_up[i],
                          w_down[i], ln2_g[i], ln2_b[i], w_ple[i], w_ple_gate[i])
        for pi in range(N_PATTERNS):
            kvp[pi].append(kv_new_p[pi])
            kvs[pi].append(kv_new_s[pi])
        pool_p.append(pool_new_p)
        pool_s.append(pool_new_s)
    return (xp, xs, jnp.stack(kvp[0]), jnp.stack(kvp[1]), jnp.stack(kvp[2]), jnp.stack(pool_p),
            jnp.stack(kvs[0]), jnp.stack(kvs[1]), jnp.stack(kvs[2]), jnp.stack(pool_s))
```

```python
import functools

import jax
import jax.numpy as jnp
from jax import lax
from jax.experimental import pallas as pl
from jax.experimental.pallas import tpu as pltpu

F32 = jnp.float32
BF16 = jnp.bfloat16
HIGHEST = lax.Precision.HIGHEST

PAST_LEN = 8192
HEAD_DIM = 64
N_HEADS = 16
ATTN_WIDTH = N_HEADS * HEAD_DIM
DILATED_PATTERNS = ((128, 1), (512, 4), (2048, 16))
N_PATTERNS = len(DILATED_PATTERNS)
BAND = 128
POOL_WINDOWS = (2, 4, 8, 16)
POOL_GROUP = 256
POOL_WIDTH = POOL_GROUP * len(POOL_WINDOWS)
POOL_STATE = max(POOL_WINDOWS) - 1
ROT_DIM = HEAD_DIM // 4
ROPE_THETA = 500000.0
QKV_WIDTH = N_PATTERNS * 3 * ATTN_WIDTH
N_GROUPS = 4
EXPERTS_PER_GROUP = 8
N_EXPERTS = N_GROUPS * EXPERTS_PER_GROUP
DEPTH = 1
ALPHA = (2.0 * DEPTH) ** 0.25
LN_EPS = 1e-5
NEG_INF = -1e30

LANES = 128
SUBLANES = 8
VMEM_LIMIT = 48 * 1024 * 1024

MOE_TILE = 256
HALO = 16


def _cparams(semantics):
    return pltpu.CompilerParams(dimension_semantics=semantics, vmem_limit_bytes=VMEM_LIMIT)


def _rope_tables(pos):
    inv_freq = ROPE_THETA ** (-jnp.arange(0, ROT_DIM, 2, dtype=F32) / ROT_DIM)
    ang = pos.astype(F32)[:, None] * inv_freq[None, :]
    cos, sin = jnp.cos(ang), jnp.sin(ang)
    t = pos.shape[0]
    rest = HEAD_DIM - ROT_DIM
    c64 = jnp.concatenate([cos, cos, jnp.ones((t, rest), F32)], -1)
    s64 = jnp.concatenate([-sin, sin, jnp.zeros((t, rest), F32)], -1)
    return jnp.tile(c64, (1, LANES // HEAD_DIM)), jnp.tile(s64, (1, LANES // HEAD_DIM))


def _rope_chunk(xc, cos, sin):
    half = ROT_DIM // 2
    lane = lax.broadcasted_iota(jnp.int32, xc.shape, 1) & (HEAD_DIM - 1)
    upper = pltpu.roll(xc, LANES - half, 1)
    lower = pltpu.roll(xc, half, 1)
    partner = jnp.where(lane < half, upper, lower)
    return xc * cos + partner * sin


def _split_bf16(a):
    hi = a.astype(BF16)
    return hi, (a - hi.astype(F32)).astype(BF16)


def _mm(a, w):
    if w.dtype == BF16:
        return jnp.dot(a.astype(BF16), w, preferred_element_type=F32)
    ah, al = _split_bf16(a.astype(F32))
    wh, wl = _split_bf16(w)
    return (jnp.dot(ah, wh, preferred_element_type=F32) + jnp.dot(al, wh, preferred_element_type=F32)
            + jnp.dot(ah, wl, preferred_element_type=F32))


def _in_proj_kernel(x_ref, w_ref, cos_ref, sin_ref, qkv_ref, u_ref, xb_ref, *, tn):
    j = pl.program_id(1)
    n_qkv_tiles = QKV_WIDTH // tn
    tiles_per_role = ATTN_WIDTH // tn

    if w_ref.dtype == BF16:
        @pl.when(j == 0)
        def _():
            xb_ref[...] = x_ref[...].astype(BF16)

        acc = jnp.dot(xb_ref[...], w_ref[...], preferred_element_type=F32)
    else:
        acc = _mm(x_ref[...], w_ref[...])
    role = (j // tiles_per_role) % 3
    is_qkv = j < n_qkv_tiles

    @pl.when(is_qkv & (role < 2))
    def _():
        cos = cos_ref[...]
        sin = sin_ref[...]
        for c in range(tn // LANES):
            sl = slice(c * LANES, (c + 1) * LANES)
            qkv_ref[:, sl] = _rope_chunk(acc[:, sl], cos, sin).astype(qkv_ref.dtype)

    @pl.when(is_qkv & (role == 2))
    def _():
        qkv_ref[...] = acc.astype(qkv_ref.dtype)

    @pl.when(j >= n_qkv_tiles)
    def _():
        u_ref[...] = acc


def _in_proj(x, w_bf16, cos_t, sin_t, *, tm, tn, qkv_dtype):
    m, d = x.shape
    n = w_bf16.shape[1]
    assert m % tm == 0 and n % tn == 0 and ATTN_WIDTH % tn == 0 and POOL_WIDTH == tn
    n_qkv_tiles = QKV_WIDTH // tn
    grid = (m // tm, n // tn)
    return pl.pallas_call(
        functools.partial(_in_proj_kernel, tn=tn),
        grid=grid,
        in_specs=[
            pl.BlockSpec((tm, d), lambda i, j: (i, 0)),
            pl.BlockSpec((d, tn), lambda i, j: (0, j)),
            pl.BlockSpec((tm, LANES), lambda i, j: (i, 0)),
            pl.BlockSpec((tm, LANES), lambda i, j: (i, 0)),
        ],
        out_specs=[
            pl.BlockSpec((tm, tn), lambda i, j: (i, jnp.minimum(j, n_qkv_tiles - 1))),
            pl.BlockSpec((tm, tn), lambda i, j: (i, 0)),
        ],
        out_shape=[
            jax.ShapeDtypeStruct((m, QKV_WIDTH), qkv_dtype),
            jax.ShapeDtypeStruct((m, POOL_WIDTH), F32),
        ],
        scratch_shapes=[pltpu.VMEM((tm, d), BF16)],
        compiler_params=_cparams(("arbitrary", "arbitrary")),
        name="in_proj",
    )(x, w_bf16, cos_t, sin_t)


def _attn_kernel(q_ref, kp_ref, kc_ref, vp_ref, vc_ref, o_ref, lse_ref, lim_ref, *, r_max):
    n = pl.program_id(1)
    qi = lax.broadcasted_iota(jnp.int32, (BAND, 2 * BAND), 0)
    ki = lax.broadcasted_iota(jnp.int32, (BAND, 2 * BAND), 1)
    dist = BAND + qi - ki
    valid = (dist >= 0) & (dist <= r_max) & ((n > 0) | (ki >= BAND))
    lim_ref[...] = jnp.where(valid, jnp.float32(jnp.finfo(F32).max), jnp.float32(NEG_INF))

    lane = lax.broadcasted_iota(jnp.int32, (BAND, LANES), 1)
    heads_per_chunk = LANES // HEAD_DIM
    lane_head = lane >> (HEAD_DIM.bit_length() - 1)
    qscale = [jnp.where(lane_head == hh, HEAD_DIM ** -0.5, 0.0).astype(BF16) for hh in range(heads_per_chunk)]
    lse_acc = jnp.zeros((BAND, LANES), F32)
    for c in range(ATTN_WIDTH // LANES):
        sl = slice(c * LANES, (c + 1) * LANES)
        q2 = q_ref[:, sl]
        kp, kc, vp, vc = kp_ref[:, sl], kc_ref[:, sl], vp_ref[:, sl], vc_ref[:, sl]
        o2 = jnp.zeros((BAND, LANES), F32)
        for hh in range(heads_per_chunk):
            in_head = lane_head == hh
            qm = q2 * qscale[hh]
            nt = (((1,), (1,)), ((), ()))
            s = jnp.concatenate(
                [lax.dot_general(qm, kp, nt, preferred_element_type=F32),
                 lax.dot_general(qm, kc, nt, preferred_element_type=F32)], axis=1)
            s = jnp.minimum(s, lim_ref[...])
            m = jnp.max(s, axis=-1, keepdims=True)
            e = jnp.exp(s - m)
            den = jnp.sum(e, axis=-1, keepdims=True)
            eb = e.astype(BF16)
            pv = (jnp.dot(eb[:, :BAND], vp, preferred_element_type=F32)
                  + jnp.dot(eb[:, BAND:], vc, preferred_element_type=F32))
            o2 = jnp.where(in_head, pv / den, o2)
            lse_acc = jnp.where(lane == c * heads_per_chunk + hh, m + jnp.log(den), lse_acc)
        o_ref[:, sl] = o2.astype(o_ref.dtype)
    lse_ref[...] = lse_acc


def _attn_pattern(qkv, pi, dil, window):
    s = qkv.shape[0]
    assert s % (dil * BAND) == 0
    nb = s // (dil * BAND)
    r_max = window // dil
    blocks_per_pos = QKV_WIDTH // ATTN_WIDTH
    qkv_v = qkv.reshape(s // dil, dil * QKV_WIDTH)
    base = pi * 3

    def col(role):
        return lambda r, n: (n, r * blocks_per_pos + base + role)

    def col_prev(role):
        return lambda r, n: (jnp.maximum(n - 1, 0), r * blocks_per_pos + base + role)

    blk = (BAND, ATTN_WIDTH)
    o, lse = pl.pallas_call(
        functools.partial(_attn_kernel, r_max=r_max),
        grid=(dil, nb),
        in_specs=[
            pl.BlockSpec(blk, col(0)),
            pl.BlockSpec(blk, col_prev(1)),
            pl.BlockSpec(blk, col(1)),
            pl.BlockSpec(blk, col_prev(2)),
            pl.BlockSpec(blk, col(2)),
        ],
        out_specs=[
            pl.BlockSpec(blk, lambda r, n: (n, r)),
            pl.BlockSpec((BAND, LANES), lambda r, n: (n, r)),
        ],
        out_shape=[
            jax.ShapeDtypeStruct((s // dil, dil * ATTN_WIDTH), BF16),
            jax.ShapeDtypeStruct((s // dil, dil * LANES), F32),
        ],
        scratch_shapes=[pltpu.VMEM((BAND, 2 * BAND), F32)],
        compiler_params=_cparams(("arbitrary", "arbitrary")),
        name=f"attn_p{pi}",
    )(qkv_v, qkv_v, qkv_v, qkv_v, qkv_v)
    return o.reshape(s, ATTN_WIDTH), lse.reshape(s, LANES)


def _pool_groups(buf_ref, tm, pos0, w_pool_ref, scale_ref, store):
    row = lax.broadcasted_iota(jnp.int32, (tm, 1), 0) + pos0
    for g, win in enumerate(POOL_WINDOWS):
        sl = slice(g * POOL_GROUP, (g + 1) * POOL_GROUP)
        cur = buf_ref[HALO:HALO + tm, sl]
        acc = cur
        for jj in range(1, win):
            acc = acc + buf_ref[HALO - jj:HALO - jj + tm, sl]
        cnt = jnp.minimum(row + 1, win).astype(F32)
        d = acc / cnt - cur
        y = _mm(d, w_pool_ref[g])
        store(g, y * scale_ref[:, sl])


def _pool_kernel(u_ref, uprev_ref, w_pool_ref, scale_ref, o_ref, buf_ref, *, tm):
    i = pl.program_id(0)
    buf_ref[0:HALO, :] = jnp.where(i > 0, uprev_ref[...], jnp.zeros_like(uprev_ref))
    buf_ref[HALO:HALO + tm, :] = u_ref[...]

    def store(g, val):
        o_ref[:, g * POOL_GROUP:(g + 1) * POOL_GROUP] = val.astype(o_ref.dtype)

    _pool_groups(buf_ref, tm, i * tm, w_pool_ref, scale_ref, store)


def _pool_prompt(u, w_pool_bf16, pool_scale, *, tm):
    s = u.shape[0]
    assert s % tm == 0 and tm % HALO == 0
    per = tm // HALO
    return pl.pallas_call(
        functools.partial(_pool_kernel, tm=tm),
        grid=(s // tm,),
        in_specs=[
            pl.BlockSpec((tm, POOL_WIDTH), lambda i: (i, 0)),
            pl.BlockSpec((HALO, POOL_WIDTH), lambda i: (jnp.maximum(i * per - 1, 0), 0)),
            pl.BlockSpec((len(POOL_WINDOWS), POOL_GROUP, POOL_GROUP), lambda i: (0, 0, 0)),
            pl.BlockSpec((1, POOL_WIDTH), lambda i: (0, 0)),
        ],
        out_specs=pl.BlockSpec((tm, POOL_WIDTH), lambda i: (i, 0)),
        out_shape=jax.ShapeDtypeStruct((s, POOL_WIDTH), BF16),
        scratch_shapes=[pltpu.VMEM((HALO + tm, POOL_WIDTH), F32)],
        compiler_params=_cparams(("arbitrary",)),
        name="pool",
    )(u, u, w_pool_bf16, pool_scale)


def _sample_attn_kernel(zt_ref, c_ref, cn_ref, ot_ref, lt_ref, *, dil, hps, length):
    b = pl.program_id(0)
    g = pl.program_id(1)

    @pl.when((b == 0) & (g == 0))
    def _():
        ot_ref[...] = jnp.zeros(ot_ref.shape, F32)
        lt_ref[...] = jnp.zeros(lt_ref.shape, F32)

    rows = hps * HEAD_DIM
    base = pl.multiple_of(g * rows, rows)
    lane_z = lax.broadcasted_iota(jnp.int32, (rows, zt_ref.shape[1]), 1)

    def column(role):
        blk = zt_ref[pl.ds(role * ATTN_WIDTH + base, rows), :]
        return jnp.sum(jnp.where(lane_z == b, blk, 0.0), axis=-1, keepdims=True)

    qc, knc, vnc = column(0), column(1), column(2)
    scale = HEAD_DIM ** -0.5
    lane = lax.broadcasted_iota(jnp.int32, (1, length), 1)
    key_ok = (lane & (dil - 1)) == 0
    last = lane == length - 1
    lane_o = lax.broadcasted_iota(jnp.int32, (HEAD_DIM, LANES), 1)
    lane_l = lax.broadcasted_iota(jnp.int32, (1, LANES), 1)
    for h in range(hps):
        sl = slice(h * HEAD_DIM, (h + 1) * HEAD_DIM)
        q, kn, vn = qc[sl], knc[sl], vnc[sl]
        k = c_ref[0, h]
        v = c_ref[1, h]
        s = jnp.where(key_ok, jnp.sum(k * q, axis=0, keepdims=True) * scale, NEG_INF)
        s_n = jnp.sum(kn * q, axis=0, keepdims=True) * scale
        m = jnp.maximum(jnp.max(s, axis=-1, keepdims=True), s_n)
        e = jnp.exp(s - m)
        e_n = jnp.exp(s_n - m)
        den = jnp.sum(e, axis=-1, keepdims=True) + e_n
        o = (jnp.sum(v * e, axis=-1, keepdims=True) + vn * e_n) / den
        lse = m + jnp.log(den)
        cn_ref[0, h] = jnp.where(last, kn, pltpu.roll(k, length - 1, 1))
        cn_ref[1, h] = jnp.where(last, vn, pltpu.roll(v, length - 1, 1))
        r0 = pl.multiple_of(base + h * HEAD_DIM, HEAD_DIM)
        ot_ref[pl.ds(r0, HEAD_DIM), :] = jnp.where(lane_o == b, o, ot_ref[pl.ds(r0, HEAD_DIM), :])
        head = g * hps + h
        lt_ref[pl.ds(head, 1), :] = jnp.where(lane_l == b, lse, lt_ref[pl.ds(head, 1), :])


def _sample_attn(zt, c_t, pi, dil):
    db, _, _, _, length = c_t.shape
    assert db <= LANES and length == dil * BAND
    hps = min(N_HEADS, max(1, (4 * 2048) // length))
    blk = (None, 2, hps, HEAD_DIM, length)
    return pl.pallas_call(
        functools.partial(_sample_attn_kernel, dil=dil, hps=hps, length=length),
        grid=(db, N_HEADS // hps),
        in_specs=[
            pl.BlockSpec((3 * ATTN_WIDTH, db), lambda b, g: (pi, 0)),
            pl.BlockSpec(blk, lambda b, g: (b, 0, g, 0, 0)),
        ],
        out_specs=[
            pl.BlockSpec(blk, lambda b, g: (b, 0, g, 0, 0)),
            pl.BlockSpec((ATTN_WIDTH, LANES), lambda b, g: (0, 0)),
            pl.BlockSpec((LANES, LANES), lambda b, g: (0, 0)),
        ],
        out_shape=[
            jax.ShapeDtypeStruct(c_t.shape, c_t.dtype),
            jax.ShapeDtypeStruct((ATTN_WIDTH, LANES), F32),
            jax.ShapeDtypeStruct((LANES, LANES), F32),
        ],
        compiler_params=_cparams(("arbitrary", "arbitrary")),
        name=f"sample_attn_p{pi}",
    )(zt, c_t)


def _sample_pool_kernel(u_ref, sp_ref, w_pool_ref, scale_ref, pool_ref, buf_ref):
    buf_ref[...] = jnp.zeros(buf_ref.shape, F32)
    buf_ref[HALO - POOL_STATE:HALO, :] = sp_ref[...]
    buf_ref[HALO:HALO + 1, :] = u_ref[...]

    def store(g, val):
        pool_ref[:, g * POOL_GROUP:(g + 1) * POOL_GROUP] = val[0:1]

    _pool_groups(buf_ref, HALO, PAST_LEN, w_pool_ref, scale_ref, store)


def _head_expand():
    dim_head = jnp.arange(ATTN_WIDTH, dtype=jnp.int32) // HEAD_DIM
    return (jnp.arange(LANES, dtype=jnp.int32)[:, None] == dim_head[None, :]).astype(F32)


def _sample_pool(us, state_pool, w_pool_bf16, pool_scale):
    db = us.shape[0]
    pool = pl.pallas_call(
        _sample_pool_kernel,
        grid=(db,),
        in_specs=[
            pl.BlockSpec((None, 1, POOL_WIDTH), lambda b: (b, 0, 0)),
            pl.BlockSpec((None, POOL_STATE, POOL_WIDTH), lambda b: (b, 0, 0)),
            pl.BlockSpec((len(POOL_WINDOWS), POOL_GROUP, POOL_GROUP), lambda b: (0, 0, 0)),
            pl.BlockSpec((1, POOL_WIDTH), lambda b: (0, 0)),
        ],
        out_specs=pl.BlockSpec((None, 1, POOL_WIDTH), lambda b: (b, 0, 0)),
        out_shape=jax.ShapeDtypeStruct((db, 1, POOL_WIDTH), F32),
        scratch_shapes=[pltpu.VMEM((2 * HALO, POOL_WIDTH), F32)],
        compiler_params=_cparams(("arbitrary",)),
        name="sample_pool",
    )(us.reshape(db, 1, POOL_WIDTH), state_pool, w_pool_bf16, pool_scale)
    return pool.reshape(db, POOL_WIDTH)


def _layer_norm(xf, g, b):
    mu = jnp.mean(xf, axis=-1, keepdims=True)
    xc = xf - mu
    var = jnp.mean(xc * xc, axis=-1, keepdims=True)
    return xc * lax.rsqrt(var + LN_EPS) * g + b


def _route(logits):
    lane = lax.broadcasted_iota(jnp.int32, logits.shape, 1)
    lane_f = lane.astype(F32)
    neg = jnp.float32(-jnp.inf)

    def first_argmax(v, vmax):
        return jnp.min(jnp.where(v == vmax, lane_f, jnp.float32(LANES)), axis=-1, keepdims=True)

    gl = jnp.where(lane < N_GROUPS, logits, neg)
    gmax = jnp.max(gl, axis=-1, keepdims=True)
    g_sel = first_argmax(gl, gmax)
    g_gate = 1.0 / jnp.sum(jnp.exp(gl - gmax), axis=-1, keepdims=True)
    lane_group = (((lane + (EXPERTS_PER_GROUP - N_GROUPS)) >> 3) - 1).astype(F32)
    el = jnp.where(lane_group == g_sel, logits, neg)
    v1 = jnp.max(el, axis=-1, keepdims=True)
    i1 = first_argmax(el, v1)
    el2 = jnp.where(lane_f == i1, neg, el)
    v2 = jnp.max(el2, axis=-1, keepdims=True)
    i2 = first_argmax(el2, v2)
    t = jnp.exp(v2 - v1)
    w1 = (1.0 / (1.0 + t)) * g_gate
    w2 = (t / (1.0 + t)) * g_gate
    route = jnp.where(lane == 0, i1 - N_GROUPS, 0.0)
    route = jnp.where(lane == 1, i2 - N_GROUPS, route)
    route = jnp.where(lane == 2, w1, route)
    route = jnp.where(lane == 3, w2, route)
    return route


def _post_kernel(o0_ref, o1_ref, o2_ref, l0_ref, l1_ref, l2_ref, pool_ref, x_ref, hexp_ref,
                 wo_ref, g1_ref, b1_ref, wr_ref, br_ref, x1_ref, route_ref):
    l0, l1, l2 = l0_ref[...], l1_ref[...], l2_ref[...]
    lmax = jnp.maximum(jnp.maximum(l0, l1), l2)
    e0, e1, e2 = jnp.exp(l0 - lmax), jnp.exp(l1 - lmax), jnp.exp(l2 - lmax)
    den = e0 + e1 + e2
    hexp = hexp_ref[...]
    attn = jnp.zeros(o0_ref.shape, F32)
    for o_ref, e in ((o0_ref, e0), (o1_ref, e1), (o2_ref, e2)):
        w_x = jnp.dot(e / den, hexp, precision=HIGHEST, preferred_element_type=F32)
        attn = attn + o_ref[...].astype(F32) * w_x
    h = _mm(attn, wo_ref[0:ATTN_WIDTH, :]) + _mm(pool_ref[...], wo_ref[ATTN_WIDTH:, :])
    x1 = _layer_norm(ALPHA * x_ref[...] + h, g1_ref[...], b1_ref[...])
    x1_ref[...] = x1
    logits = jnp.dot(x1, wr_ref[...], precision=HIGHEST, preferred_element_type=F32) + br_ref[...]
    route_ref[...] = _route(logits)


def _row_spec(tm, width):
    return pl.BlockSpec((tm, width), lambda i: (i, 0))


def _const_spec(shape):
    return pl.BlockSpec(shape, lambda i: (0,) * len(shape), pipeline_mode=pl.Buffered(1))


def _post(os_, lses, pool, x, wo, g1, b1, wr, br, *, tm, name):
    m, d = x.shape
    hexp = _head_expand()
    return pl.pallas_call(
        _post_kernel,
        grid=(m // tm,),
        in_specs=[_row_spec(tm, ATTN_WIDTH)] * 3 + [_row_spec(tm, LANES)] * 3
        + [_row_spec(tm, POOL_WIDTH), _row_spec(tm, d), _const_spec((LANES, ATTN_WIDTH)),
           _const_spec((d, d)), _const_spec((1, d)), _const_spec((1, d)),
           _const_spec((d, LANES)), _const_spec((1, LANES))],
        out_specs=[_row_spec(tm, d), _row_spec(tm, LANES)],
        out_shape=[jax.ShapeDtypeStruct((m, d), F32), jax.ShapeDtypeStruct((m, LANES), F32)],
        compiler_params=_cparams(("arbitrary",)),
        name=name,
    )(*os_, *lses, pool, x, hexp, wo, g1, b1, wr, br)


def _routing_plan(pair_expert, n_tiles):
    p = pair_expert.shape[0]
    experts = jnp.arange(N_EXPERTS, dtype=jnp.int32)
    onehot = (pair_expert[:, None] == experts[None, :]).astype(jnp.int32)
    csum = jnp.cumsum(onehot, axis=0)
    rank = jnp.take_along_axis(csum, pair_expert[:, None], axis=1)[:, 0] - 1
    counts = csum[-1]
    tiles_per = (counts + MOE_TILE - 1) // MOE_TILE
    tile_end = jnp.cumsum(tiles_per)
    slot_start = (tile_end - tiles_per) * MOE_TILE
    dest = slot_start[pair_expert] + rank
    n_used = tile_end[-1]
    tile_ids = jnp.arange(n_tiles, dtype=jnp.int32)
    tile_expert = jnp.sum((tile_ids[:, None] >= tile_end[None, :]).astype(jnp.int32), axis=1)
    last_expert = jnp.sum((n_used - 1 >= tile_end).astype(jnp.int32))
    tile_expert = jnp.where(tile_ids < n_used, tile_expert, last_expert).astype(jnp.int32)
    slot_token = jnp.zeros((n_tiles * MOE_TILE,), jnp.int32).at[dest].set(
        jnp.arange(p, dtype=jnp.int32) // 2)
    return dest.astype(jnp.int32), slot_token, tile_expert, n_used.reshape(1).astype(jnp.int32)


def _moe_kernel(te_ref, nu_ref, st_ref, xp_ref, xs_ref, wg_ref, wu_ref, wd_ref, ys_ref,
                buf_ref, wgb_ref, wub_ref, wdb_ref, sem_ref, *, n_prompt):
    c = pl.program_id(0)
    n_tiles = pl.num_programs(0)
    n_used = nu_ref[0]
    d = buf_ref.shape[-1]

    def row_copy(tile, slot, i):
        t = st_ref[tile * MOE_TILE + i]
        dst = buf_ref.at[slot, pl.ds(i, 1), :]
        return t, dst

    def issue(tile, slot):
        def body(i, carry):
            t, dst = row_copy(tile, slot, i)

            @pl.when(t < n_prompt)
            def _():
                pltpu.make_async_copy(xp_ref.at[pl.ds(t, 1), :], dst, sem_ref.at[slot]).start()

            @pl.when(t >= n_prompt)
            def _():
                pltpu.make_async_copy(xs_ref.at[pl.ds(t - n_prompt, 1), :], dst, sem_ref.at[slot]).start()

            return carry

        lax.fori_loop(0, MOE_TILE, body, 0)

    def wait(slot):
        def body(i, carry):
            pltpu.make_async_copy(xp_ref.at[pl.ds(0, 1), :], buf_ref.at[slot, pl.ds(i, 1), :],
                                  sem_ref.at[slot]).wait()
            return carry

        lax.fori_loop(0, MOE_TILE, body, 0)

    slot = c % 2

    @pl.when((c == 0) & (n_used > 0))
    def _():
        issue(0, 0)

    @pl.when((c + 1 < n_tiles) & (c + 1 < n_used))
    def _():
        issue(c + 1, 1 - slot)

    prev_expert = te_ref[jnp.maximum(c - 1, 0)]

    @pl.when((c == 0) | (te_ref[c] != prev_expert))
    def _():
        wgb_ref[...] = wg_ref[...].astype(BF16)
        wub_ref[...] = wu_ref[...].astype(BF16)
        wdb_ref[...] = wd_ref[...].astype(BF16)

    @pl.when(c < n_used)
    def _():
        wait(slot)
        x = buf_ref[slot].astype(BF16)
        gate = jnp.dot(x, wgb_ref[...], preferred_element_type=F32)
        up = jnp.dot(x, wub_ref[...], preferred_element_type=F32)
        h = jax.nn.silu(gate) * up
        ys_ref[...] = jnp.dot(h.astype(BF16), wdb_ref[...], preferred_element_type=F32)

    @pl.when(c >= n_used)
    def _():
        ys_ref[...] = jnp.zeros(ys_ref.shape, ys_ref.dtype)


def _moe(x1p, x1s, slot_token, tile_expert, n_used, w_gate, w_up, w_down, n_tiles):
    n_prompt, d = x1p.shape
    f = w_gate.shape[-1]
    grid_spec = pltpu.PrefetchScalarGridSpec(
        num_scalar_prefetch=3,
        grid=(n_tiles,),
        in_specs=[
            pl.BlockSpec(memory_space=pl.ANY),
            pl.BlockSpec(memory_space=pl.ANY),
            pl.BlockSpec((None, d, f), lambda c, te, nu, st: (te[c], 0, 0)),
            pl.BlockSpec((None, d, f), lambda c, te, nu, st: (te[c], 0, 0)),
            pl.BlockSpec((None, f, d), lambda c, te, nu, st: (te[c], 0, 0)),
        ],
        out_specs=pl.BlockSpec((MOE_TILE, d), lambda c, te, nu, st: (c, 0)),
        scratch_shapes=[
            pltpu.VMEM((2, MOE_TILE, d), F32),
            pltpu.VMEM((d, f), BF16),
            pltpu.VMEM((d, f), BF16),
            pltpu.VMEM((f, d), BF16),
            pltpu.SemaphoreType.DMA((2,)),
        ],
    )
    return pl.pallas_call(
        functools.partial(_moe_kernel, n_prompt=n_prompt),
        grid_spec=grid_spec,
        out_shape=jax.ShapeDtypeStruct((n_tiles * MOE_TILE, d), F32),
        compiler_params=_cparams(("arbitrary",)),
        name="moe",
    )(tile_expert, n_used, slot_token, x1p, x1s, w_gate, w_up, w_down)


def _final_kernel(dest_ref, ys_ref, x1_ref, route_ref, p_ref, g2_ref, b2_ref, wpg_ref, wp_ref, o_ref,
                  buf_ref, sem_ref, *, tm, pair0):
    i = pl.program_id(0)
    n_steps = pl.num_programs(0)

    def issue(step, slot):
        def body(r, carry):
            for k in range(2):
                dslot = dest_ref[pair0 + (step * tm + r) * 2 + k]
                pltpu.make_async_copy(ys_ref.at[pl.ds(dslot, 1), :], buf_ref.at[slot, k, pl.ds(r, 1), :],
                                      sem_ref.at[slot]).start()
            return carry

        lax.fori_loop(0, tm, body, 0)

    def wait(slot):
        def body(r, carry):
            for k in range(2):
                pltpu.make_async_copy(ys_ref.at[pl.ds(0, 1), :], buf_ref.at[slot, k, pl.ds(r, 1), :],
                                      sem_ref.at[slot]).wait()
            return carry

        lax.fori_loop(0, tm, body, 0)

    slot = i % 2

    @pl.when(i == 0)
    def _():
        issue(0, 0)

    @pl.when(i + 1 < n_steps)
    def _():
        issue(i + 1, 1 - slot)

    wait(slot)
    route = route_ref[...]
    y = route[:, 2:3] * buf_ref[slot, 0] + route[:, 3:4] * buf_ref[slot, 1]
    x2 = _layer_norm(ALPHA * x1_ref[...] + y, g2_ref[...], b2_ref[...])
    gate = jax.nn.sigmoid(jnp.dot(x2.astype(BF16), wpg_ref[...], preferred_element_type=F32))
    ple = jnp.dot(p_ref[...].astype(BF16), wp_ref[...], preferred_element_type=F32)
    o_ref[...] = x2 + gate * ple


def _final(dest, ys, x1, route, p, g2, b2, wpg, wp, *, tm, pair0):
    m, d = x1.shape
    pd = p.shape[1]
    assert m % tm == 0
    grid_spec = pltpu.PrefetchScalarGridSpec(
        num_scalar_prefetch=1,
        grid=(m // tm,),
        in_specs=[
            pl.BlockSpec(memory_space=pl.ANY),
            pl.BlockSpec((tm, d), lambda i, dst: (i, 0)),
            pl.BlockSpec((tm, LANES), lambda i, dst: (i, 0)),
            pl.BlockSpec((tm, pd), lambda i, dst: (i, 0)),
            pl.BlockSpec((1, d), lambda i, dst: (0, 0)),
            pl.BlockSpec((1, d), lambda i, dst: (0, 0)),
            pl.BlockSpec((d, d), lambda i, dst: (0, 0)),
            pl.BlockSpec((pd, d), lambda i, dst: (0, 0)),
        ],
        out_specs=pl.BlockSpec((tm, d), lambda i, dst: (i, 0)),
        scratch_shapes=[pltpu.VMEM((2, 2, tm, d), F32), pltpu.SemaphoreType.DMA((2,))],
    )
    return pl.pallas_call(
        functools.partial(_final_kernel, tm=tm, pair0=pair0),
        grid_spec=grid_spec,
        out_shape=jax.ShapeDtypeStruct((m, d), F32),
        compiler_params=_cparams(("arbitrary",)),
        name="final",
    )(dest, ys, x1, route, p, g2, b2, wpg, wp)


def _pick_tile(m, pref):
    t = pref
    while m % t:
        t //= 2
    return t


def _layer(xp, xs, caches, state_pool, pp, ps, w):
    s, d = xp.shape
    db = xs.shape[0]

    cos_p, sin_p = _rope_tables(jnp.arange(s, dtype=jnp.int32))
    qkv_p, u_p = _in_proj(xp, w["w_in"], cos_p, sin_p, tm=_pick_tile(s, 512), tn=ATTN_WIDTH, qkv_dtype=BF16)
    os_, lses = [], []
    for pi, (window, dil) in enumerate(DILATED_PATTERNS):
        o, l = _attn_pattern(qkv_p, pi, dil, window)
        os_.append(o)
        lses.append(l)
    pool_p = _pool_prompt(u_p, w["w_pool"], w["pool_scale"], tm=_pick_tile(s, 512))

    cos_s, sin_s = _rope_tables(jnp.full((db,), PAST_LEN, jnp.int32))
    qkv_s, u_s = _in_proj(xs, w["w_in_f32"], cos_s, sin_s, tm=db, tn=ATTN_WIDTH, qkv_dtype=F32)
    zt_s = qkv_s.T
    os_s, lses_s, kv_s = [], [], []
    for pi, ((window, dil), c) in enumerate(zip(DILATED_PATTERNS, caches)):
        cn, ot, lt = _sample_attn(zt_s, jnp.transpose(c, (0, 2, 3, 4, 1)), pi, dil)
        kv_s.append(jnp.transpose(cn, (0, 4, 1, 2, 3)))
        os_s.append(ot.T[:db])
        lses_s.append(lt.T[:db])
    pool_s = _sample_pool(u_s, state_pool, w["w_pool_f32"], w["pool_scale"])

    post_w = (w["ln1_g"], w["ln1_b"], w["w_router"], w["b_router"])
    x1_p, route_p = _post(os_, lses, pool_p, xp, w["w_out"], *post_w, tm=_pick_tile(s, 256), name="post_prompt")
    x1_s, route_s = _post(os_s, lses_s, pool_s, xs, w["w_out_f32"], *post_w, tm=db, name="post_sample")

    pair_expert = jnp.concatenate([route_p[:, 0:2].reshape(-1), route_s[:, 0:2].reshape(-1)]).astype(jnp.int32)
    n_pairs = pair_expert.shape[0]
    n_tiles = -(-n_pairs // MOE_TILE) + N_EXPERTS
    dest, slot_token, tile_expert, n_used = _routing_plan(pair_expert, n_tiles)
    ys = _moe(x1_p, x1_s, slot_token, tile_expert, n_used, w["w_gate"], w["w_up"], w["w_down"], n_tiles)

    fin_w = (w["ln2_g"], w["ln2_b"], w["w_ple_gate"], w["w_ple"])
    y_p = _final(dest, ys, x1_p, route_p, pp, *fin_w, tm=_pick_tile(s, 256), pair0=0)
    y_s = _final(dest, ys, x1_s, route_s, ps, *fin_w, tm=db, pair0=2 * s)

    kv_p = []
    for pi, (window, dil) in enumerate(DILATED_PATTERNS):
        keep = min(window, s)
        base = pi * 3 * ATTN_WIDTH
        kv = qkv_p[s - keep:, base + ATTN_WIDTH:base + 3 * ATTN_WIDTH].astype(F32)
        kv_p.append(kv.reshape(1, keep, 2, N_HEADS, HEAD_DIM))
    pool_state_p = u_p[s - POOL_STATE:][None]
    pool_state_s = jnp.concatenate([state_pool[:, 1:], u_s[:, None, :]], axis=1)
    return y_p, y_s, kv_p, pool_state_p, kv_s, pool_state_s


def kernel(x_prompt, x_sample, cache_kv_w128_d1, cache_kv_w512_d4, cache_kv_w2048_d16, state_pool, p_prompt, p_sample, w_in, w_out, w_pool, pool_scale, ln1_g, ln1_b, w_group_router, b_group_router, w_expert_router, b_expert_router, w_gate, w_up, w_down, ln2_g, ln2_b, w_ple, w_ple_gate):
    assert w_in.shape[0] == DEPTH == 1 and x_prompt.shape[0] == 1 and x_sample.shape[1] == 1
    d = x_prompt.shape[-1]
    pad = LANES - N_GROUPS - N_EXPERTS
    w = {
        "w_in": w_in[0].astype(BF16), "w_in_f32": w_in[0],
        "w_out": w_out[0].astype(BF16), "w_out_f32": w_out[0],
        "w_pool": w_pool[0].astype(BF16), "w_pool_f32": w_pool[0],
        "pool_scale": pool_scale[0].reshape(1, POOL_WIDTH),
        "ln1_g": ln1_g[0].reshape(1, d), "ln1_b": ln1_b[0].reshape(1, d),
        "ln2_g": ln2_g[0].reshape(1, d), "ln2_b": ln2_b[0].reshape(1, d),
        "w_router": jnp.concatenate([w_group_router[0], w_expert_router[0], jnp.zeros((d, pad), F32)], axis=1),
        "b_router": jnp.concatenate([b_group_router[0], b_expert_router[0], jnp.zeros((pad,), F32)]).reshape(1, LANES),
        "w_gate": w_gate[0], "w_up": w_up[0], "w_down": w_down[0],
        "w_ple": w_ple[0].astype(BF16),
        "w_ple_gate": w_ple_gate[0].astype(BF16),
    }
    caches = [cache_kv_w128_d1[0], cache_kv_w512_d4[0], cache_kv_w2048_d16[0]]
    y_p, y_s, kv_p, pool_p, kv_s, pool_s = _layer(
        x_prompt[0], x_sample[:, 0], caches, state_pool[0], p_prompt[0, 0], p_sample[0, :, 0], w)
    return (y_p[None], y_s[:, None], kv_p[0][None], kv_p[1][None], kv_p[2][None], pool_p[None],
            kv_s[0][None], kv_s[1][None], kv_s[2][None], pool_s[None])
```

```python
import functools

import jax
import jax.numpy as jnp
from jax import lax
from jax.experimental import pallas as pl
from jax.experimental.pallas import tpu as pltpu

F32 = jnp.float32
BF16 = jnp.bfloat16
HIGHEST = lax.Precision.HIGHEST

PAST_LEN = 8192
HEAD_DIM = 64
N_HEADS = 16
ATTN_WIDTH = N_HEADS * HEAD_DIM
DILATED_PATTERNS = ((128, 1), (512, 4), (2048, 16))
N_PATTERNS = len(DILATED_PATTERNS)
BAND = 128
POOL_WINDOWS = (2, 4, 8, 16)
POOL_GROUP = 256
POOL_WIDTH = POOL_GROUP * len(POOL_WINDOWS)
POOL_STATE = max(POOL_WINDOWS) - 1
ROT_DIM = HEAD_DIM // 4
ROPE_THETA = 500000.0
QKV_WIDTH = N_PATTERNS * 3 * ATTN_WIDTH
N_GROUPS = 4
EXPERTS_PER_GROUP = 8
N_EXPERTS = N_GROUPS * EXPERTS_PER_GROUP
DEPTH = 1
ALPHA = (2.0 * DEPTH) ** 0.25
LN_EPS = 1e-5
NEG_INF = -1e30

LANES = 128
SUBLANES = 8
VMEM_LIMIT = 48 * 1024 * 1024

MOE_TILE = 256
HALO = 16
GATHER_UNROLL = 8


def _cparams(semantics):
    return pltpu.CompilerParams(dimension_semantics=semantics, vmem_limit_bytes=VMEM_LIMIT)


def _rope_tables(pos):
    inv_freq = ROPE_THETA ** (-jnp.arange(0, ROT_DIM, 2, dtype=F32) / ROT_DIM)
    ang = pos.astype(F32)[:, None] * inv_freq[None, :]
    cos, sin = jnp.cos(ang), jnp.sin(ang)
    t = pos.shape[0]
    rest = HEAD_DIM - ROT_DIM
    c64 = jnp.concatenate([cos, cos, jnp.ones((t, rest), F32)], -1)
    s64 = jnp.concatenate([-sin, sin, jnp.zeros((t, rest), F32)], -1)
    return jnp.tile(c64, (1, LANES // HEAD_DIM)), jnp.tile(s64, (1, LANES // HEAD_DIM))


def _rope_chunk(xc, cos, sin):
    half = ROT_DIM // 2
    lane = lax.broadcasted_iota(jnp.int32, xc.shape, 1) & (HEAD_DIM - 1)
    upper = pltpu.roll(xc, LANES - half, 1)
    lower = pltpu.roll(xc, half, 1)
    partner = jnp.where(lane < half, upper, lower)
    return xc * cos + partner * sin


def _split_bf16(a):
    hi = a.astype(BF16)
    return hi, (a - hi.astype(F32)).astype(BF16)


def _mm(a, w):
    if w.dtype == BF16:
        return jnp.dot(a.astype(BF16), w, preferred_element_type=F32)
    ah, al = _split_bf16(a.astype(F32))
    wh, wl = _split_bf16(w)
    return (jnp.dot(ah, wh, preferred_element_type=F32) + jnp.dot(al, wh, preferred_element_type=F32)
            + jnp.dot(ah, wl, preferred_element_type=F32))


ROLE_TILES = 3


def _in_proj_kernel(x_ref, w_ref, cos_ref, sin_ref, *refs, dils):
    n_out = 1 if dils is None else len(dils)
    qkv_refs, u_ref, xb_ref, stage_ref = refs[:n_out], refs[n_out], refs[n_out + 1], refs[n_out + 2]
    j = pl.program_id(1)
    n_chunks, tm, _ = stage_ref.shape
    n_qkv_tiles = N_PATTERNS * ROLE_TILES

    def chunk(c):
        return slice(c * LANES, (c + 1) * LANES)

    if w_ref.dtype == BF16:
        @pl.when(j == 0)
        def _():
            xb_ref[...] = x_ref[...].astype(BF16)

        acc = jnp.dot(xb_ref[...], w_ref[...], preferred_element_type=F32)
    else:
        acc = _mm(x_ref[...], w_ref[...])
    role = j % ROLE_TILES
    is_qkv = j < n_qkv_tiles

    @pl.when(is_qkv & (role < 2))
    def _():
        cos = cos_ref[...]
        sin = sin_ref[...]
        for c in range(n_chunks):
            stage_ref[c] = _rope_chunk(acc[:, chunk(c)], cos, sin)

    @pl.when(is_qkv & (role == 2))
    def _():
        for c in range(n_chunks):
            stage_ref[c] = acc[:, chunk(c)]

    if dils is None:
        @pl.when(is_qkv)
        def _():
            for c in range(n_chunks):
                qkv_refs[0][:, chunk(c)] = stage_ref[c].astype(qkv_refs[0].dtype)
    else:
        for pi, dil in enumerate(dils):
            @pl.when(is_qkv & (j // ROLE_TILES == pi))
            def _(pi=pi, dil=dil):
                for r in range(dil):
                    for c in range(n_chunks):
                        rows = stage_ref[c, pl.ds(r, tm // dil, stride=dil), :] if dil > 1 else stage_ref[c]
                        qkv_refs[pi][r, :, chunk(c)] = rows.astype(qkv_refs[pi].dtype)

    @pl.when(j >= n_qkv_tiles)
    def _():
        u_ref[...] = acc


def _in_proj(x, w, cos_t, sin_t, *, tm, dils, qkv_dtype):
    m, d = x.shape
    tn = ATTN_WIDTH
    assert m % tm == 0 and w.shape[1] == QKV_WIDTH + POOL_WIDTH and POOL_WIDTH == tn
    n_qkv_tiles = N_PATTERNS * ROLE_TILES
    if dils is None:
        qkv_specs = [pl.BlockSpec((tm, tn), lambda i, j: (i, jnp.minimum(j, n_qkv_tiles - 1)))]
        qkv_shapes = [jax.ShapeDtypeStruct((m, QKV_WIDTH), qkv_dtype)]
    else:
        qkv_specs, qkv_shapes = [], []
        for pi, dil in enumerate(dils):
            assert tm % (dil * 2 * SUBLANES) == 0
            qkv_specs.append(pl.BlockSpec(
                (dil, tm // dil, tn),
                lambda i, j, pi=pi: (0, i, jnp.clip(j - pi * ROLE_TILES, 0, ROLE_TILES - 1))))
            qkv_shapes.append(jax.ShapeDtypeStruct((dil, m // dil, ROLE_TILES * tn), qkv_dtype))
    outs = pl.pallas_call(
        functools.partial(_in_proj_kernel, dils=dils),
        grid=(m // tm, n_qkv_tiles + 1),
        in_specs=[
            pl.BlockSpec((tm, d), lambda i, j: (i, 0)),
            pl.BlockSpec((d, tn), lambda i, j: (0, j)),
            pl.BlockSpec((tm, LANES), lambda i, j: (i, 0)),
            pl.BlockSpec((tm, LANES), lambda i, j: (i, 0)),
        ],
        out_specs=qkv_specs + [pl.BlockSpec((tm, tn), lambda i, j: (i, 0))],
        out_shape=qkv_shapes + [jax.ShapeDtypeStruct((m, POOL_WIDTH), F32)],
        scratch_shapes=[pltpu.VMEM((tm, d), BF16), pltpu.VMEM((tn // LANES, tm, LANES), F32)],
        compiler_params=_cparams(("arbitrary", "arbitrary")),
        name="in_proj",
    )(x, w, cos_t, sin_t)
    return outs[:-1], outs[-1]


def _attn_kernel(q_ref, kp_ref, kc_ref, vp_ref, vc_ref, o_ref, lse_ref,
                 lim_ref, s_ref, e_ref, m_ref, den_ref, *, r_max, dil):
    n = pl.program_id(0)
    r = pl.program_id(1)
    qi = lax.broadcasted_iota(jnp.int32, (BAND, 2 * BAND), 0)
    ki = lax.broadcasted_iota(jnp.int32, (BAND, 2 * BAND), 1)
    dist = BAND + qi - ki
    valid = (dist >= 0) & (dist <= r_max) & ((n > 0) | (ki >= BAND))
    lim_ref[...] = jnp.where(valid, jnp.float32(jnp.finfo(F32).max), jnp.float32(NEG_INF))

    lane = lax.broadcasted_iota(jnp.int32, (BAND, LANES), 1)
    heads_per_chunk = LANES // HEAD_DIM
    n_chunks = ATTN_WIDTH // LANES
    lane_head = lane >> (HEAD_DIM.bit_length() - 1)
    qscale = [jnp.where(lane_head == hh, HEAD_DIM ** -0.5, 0.0).astype(BF16) for hh in range(heads_per_chunk)]
    nt = (((1,), (1,)), ((), ()))

    for c in range(n_chunks):
        sl = slice(c * LANES, (c + 1) * LANES)
        q2 = q_ref[:, sl]
        kp, kc = kp_ref[:, sl], kc_ref[:, sl]
        for hh in range(heads_per_chunk):
            h = c * heads_per_chunk + hh
            qm = q2 * qscale[hh]
            s = jnp.concatenate(
                [lax.dot_general(qm, kp, nt, preferred_element_type=F32),
                 lax.dot_general(qm, kc, nt, preferred_element_type=F32)], axis=1)
            s = jnp.minimum(s, lim_ref[...])
            s_ref[h] = s
            m_ref[h] = jnp.broadcast_to(jnp.max(s, axis=-1, keepdims=True), (BAND, LANES))

    lse_acc = jnp.zeros((BAND, LANES), F32)
    for h in range(N_HEADS):
        m = m_ref[h]
        e = jnp.exp(s_ref[h] - jnp.concatenate([m, m], axis=1))
        e_ref[h] = e.astype(BF16)
        den = jnp.broadcast_to(jnp.sum(e, axis=-1, keepdims=True), (BAND, LANES))
        den_ref[h] = den
        lse_acc = jnp.where(lane == h, m + jnp.log(den), lse_acc)

    rows = slice(None) if dil == 1 else pl.ds(r, BAND, stride=dil)
    lse_ref[rows, :] = lse_acc
    for c in range(n_chunks):
        sl = slice(c * LANES, (c + 1) * LANES)
        vp, vc = vp_ref[:, sl], vc_ref[:, sl]
        o2 = jnp.zeros((BAND, LANES), F32)
        for hh in range(heads_per_chunk):
            h = c * heads_per_chunk + hh
            pv = (jnp.dot(e_ref[h, :, :BAND], vp, preferred_element_type=F32)
                  + jnp.dot(e_ref[h, :, BAND:], vc, preferred_element_type=F32))
            o2 = jnp.where(lane_head == hh, pv / den_ref[h], o2)
        o_ref[c, rows, :] = o2


def _attn_pattern(qkv, pi, dil, window):
    s = qkv.shape[0] * qkv.shape[1]
    assert qkv.shape[0] == dil and s % (dil * BAND) == 0
    nb = s // (dil * BAND)
    r_max = window // dil

    def cur(role):
        return lambda n, r: (r, n, role)

    def prev(role):
        return lambda n, r: (r, jnp.maximum(n - 1, 0), role)

    blk = (None, BAND, ATTN_WIDTH)
    span = BAND * dil
    return pl.pallas_call(
        functools.partial(_attn_kernel, r_max=r_max, dil=dil),
        grid=(nb, dil),
        in_specs=[
            pl.BlockSpec(blk, cur(0)),
            pl.BlockSpec(blk, prev(1)),
            pl.BlockSpec(blk, cur(1)),
            pl.BlockSpec(blk, prev(2)),
            pl.BlockSpec(blk, cur(2)),
        ],
        out_specs=[
            pl.BlockSpec((ATTN_WIDTH // LANES, span, LANES), lambda n, r: (0, n, 0)),
            pl.BlockSpec((span, LANES), lambda n, r: (n, 0)),
        ],
        out_shape=[
            jax.ShapeDtypeStruct((ATTN_WIDTH // LANES, s, LANES), F32),
            jax.ShapeDtypeStruct((s, LANES), F32),
        ],
        scratch_shapes=[
            pltpu.VMEM((BAND, 2 * BAND), F32),
            pltpu.VMEM((N_HEADS, BAND, 2 * BAND), F32),
            pltpu.VMEM((N_HEADS, BAND, 2 * BAND), BF16),
            pltpu.VMEM((N_HEADS, BAND, LANES), F32),
            pltpu.VMEM((N_HEADS, BAND, LANES), F32),
        ],
        compiler_params=_cparams(("arbitrary", "arbitrary")),
        name=f"attn_p{pi}",
    )(qkv, qkv, qkv, qkv, qkv)


def _pool_groups(buf_ref, tm, pos0, w_pool_ref, scale_ref, store):
    row = lax.broadcasted_iota(jnp.int32, (tm, 1), 0) + pos0
    for g, win in enumerate(POOL_WINDOWS):
        sl = slice(g * POOL_GROUP, (g + 1) * POOL_GROUP)
        cur = buf_ref[HALO:HALO + tm, sl]
        acc = cur
        for jj in range(1, win):
            acc = acc + buf_ref[HALO - jj:HALO - jj + tm, sl]
        cnt = jnp.minimum(row + 1, win).astype(F32)
        d = acc / cnt - cur
        y = _mm(d, w_pool_ref[g])
        store(g, y * scale_ref[:, sl])


def _pool_kernel(u_ref, uprev_ref, w_pool_ref, scale_ref, o_ref, buf_ref, *, tm):
    i = pl.program_id(0)
    buf_ref[0:HALO, :] = jnp.where(i > 0, uprev_ref[...], jnp.zeros_like(uprev_ref))
    buf_ref[HALO:HALO + tm, :] = u_ref[...]

    def store(g, val):
        o_ref[:, g * POOL_GROUP:(g + 1) * POOL_GROUP] = val.astype(o_ref.dtype)

    _pool_groups(buf_ref, tm, i * tm, w_pool_ref, scale_ref, store)


def _pool_prompt(u, w_pool_bf16, pool_scale, *, tm):
    s = u.shape[0]
    assert s % tm == 0 and tm % HALO == 0
    per = tm // HALO
    return pl.pallas_call(
        functools.partial(_pool_kernel, tm=tm),
        grid=(s // tm,),
        in_specs=[
            pl.BlockSpec((tm, POOL_WIDTH), lambda i: (i, 0)),
            pl.BlockSpec((HALO, POOL_WIDTH), lambda i: (jnp.maximum(i * per - 1, 0), 0)),
            pl.BlockSpec((len(POOL_WINDOWS), POOL_GROUP, POOL_GROUP), lambda i: (0, 0, 0)),
            pl.BlockSpec((1, POOL_WIDTH), lambda i: (0, 0)),
        ],
        out_specs=pl.BlockSpec((tm, POOL_WIDTH), lambda i: (i, 0)),
        out_shape=jax.ShapeDtypeStruct((s, POOL_WIDTH), BF16),
        scratch_shapes=[pltpu.VMEM((HALO + tm, POOL_WIDTH), F32)],
        compiler_params=_cparams(("arbitrary",)),
        name="pool",
    )(u, u, w_pool_bf16, pool_scale)


def _sample_attn_kernel(zt_ref, c_ref, cn_ref, ot_ref, lt_ref, *, dil, hps, length):
    b = pl.program_id(0)
    g = pl.program_id(1)

    @pl.when((b == 0) & (g == 0))
    def _():
        ot_ref[...] = jnp.zeros(ot_ref.shape, F32)
        lt_ref[...] = jnp.zeros(lt_ref.shape, F32)

    rows = hps * HEAD_DIM
    base = pl.multiple_of(g * rows, rows)
    lane_z = lax.broadcasted_iota(jnp.int32, (rows, zt_ref.shape[1]), 1)

    def column(role):
        blk = zt_ref[pl.ds(role * ATTN_WIDTH + base, rows), :]
        return jnp.sum(jnp.where(lane_z == b, blk, 0.0), axis=-1, keepdims=True)

    qc, knc, vnc = column(0), column(1), column(2)
    scale = HEAD_DIM ** -0.5
    lane = lax.broadcasted_iota(jnp.int32, (1, length), 1)
    key_ok = (lane & (dil - 1)) == 0
    last = lane == length - 1
    lane_o = lax.broadcasted_iota(jnp.int32, (HEAD_DIM, LANES), 1)
    lane_l = lax.broadcasted_iota(jnp.int32, (1, LANES), 1)
    for h in range(hps):
        sl = slice(h * HEAD_DIM, (h + 1) * HEAD_DIM)
        q, kn, vn = qc[sl], knc[sl], vnc[sl]
        k = c_ref[0, h]
        v = c_ref[1, h]
        s = jnp.where(key_ok, jnp.sum(k * q, axis=0, keepdims=True) * scale, NEG_INF)
        s_n = jnp.sum(kn * q, axis=0, keepdims=True) * scale
        m = jnp.maximum(jnp.max(s, axis=-1, keepdims=True), s_n)
        e = jnp.exp(s - m)
        e_n = jnp.exp(s_n - m)
        den = jnp.sum(e, axis=-1, keepdims=True) + e_n
        o = (jnp.sum(v * e, axis=-1, keepdims=True) + vn * e_n) / den
        lse = m + jnp.log(den)
        cn_ref[0, h] = jnp.where(last, kn, pltpu.roll(k, length - 1, 1))
        cn_ref[1, h] = jnp.where(last, vn, pltpu.roll(v, length - 1, 1))
        r0 = pl.multiple_of(base + h * HEAD_DIM, HEAD_DIM)
        ot_ref[pl.ds(r0, HEAD_DIM), :] = jnp.where(lane_o == b, o, ot_ref[pl.ds(r0, HEAD_DIM), :])
        head = g * hps + h
        lt_ref[pl.ds(head, 1), :] = jnp.where(lane_l == b, lse, lt_ref[pl.ds(head, 1), :])


def _sample_attn(zt, c_t, pi, dil):
    db, _, _, _, length = c_t.shape
    assert db <= LANES and length == dil * BAND
    hps = min(N_HEADS, max(1, (4 * 2048) // length))
    blk = (None, 2, hps, HEAD_DIM, length)
    return pl.pallas_call(
        functools.partial(_sample_attn_kernel, dil=dil, hps=hps, length=length),
        grid=(db, N_HEADS // hps),
        in_specs=[
            pl.BlockSpec((3 * ATTN_WIDTH, db), lambda b, g: (pi, 0)),
            pl.BlockSpec(blk, lambda b, g: (b, 0, g, 0, 0)),
        ],
        out_specs=[
            pl.BlockSpec(blk, lambda b, g: (b, 0, g, 0, 0)),
            pl.BlockSpec((ATTN_WIDTH, LANES), lambda b, g: (0, 0)),
            pl.BlockSpec((LANES, LANES), lambda b, g: (0, 0)),
        ],
        out_shape=[
            jax.ShapeDtypeStruct(c_t.shape, c_t.dtype),
            jax.ShapeDtypeStruct((ATTN_WIDTH, LANES), F32),
            jax.ShapeDtypeStruct((LANES, LANES), F32),
        ],
        compiler_params=_cparams(("arbitrary", "arbitrary")),
        name=f"sample_attn_p{pi}",
    )(zt, c_t)


def _sample_pool_kernel(u_ref, sp_ref, w_pool_ref, scale_ref, pool_ref, buf_ref):
    buf_ref[...] = jnp.zeros(buf_ref.shape, F32)
    buf_ref[HALO - POOL_STATE:HALO, :] = sp_ref[...]
    buf_ref[HALO:HALO + 1, :] = u_ref[...]

    def store(g, val):
        pool_ref[:, g * POOL_GROUP:(g + 1) * POOL_GROUP] = val[0:1]

    _pool_groups(buf_ref, HALO, PAST_LEN, w_pool_ref, scale_ref, store)


def _head_expand():
    dim_head = jnp.arange(ATTN_WIDTH, dtype=jnp.int32) // HEAD_DIM
    return (jnp.arange(LANES, dtype=jnp.int32)[:, None] == dim_head[None, :]).astype(F32)


def _sample_pool(us, state_pool, w_pool_bf16, pool_scale):
    db = us.shape[0]
    pool = pl.pallas_call(
        _sample_pool_kernel,
        grid=(db,),
        in_specs=[
            pl.BlockSpec((None, 1, POOL_WIDTH), lambda b: (b, 0, 0)),
            pl.BlockSpec((None, POOL_STATE, POOL_WIDTH), lambda b: (b, 0, 0)),
            pl.BlockSpec((len(POOL_WINDOWS), POOL_GROUP, POOL_GROUP), lambda b: (0, 0, 0)),
            pl.BlockSpec((1, POOL_WIDTH), lambda b: (0, 0)),
        ],
        out_specs=pl.BlockSpec((None, 1, POOL_WIDTH), lambda b: (b, 0, 0)),
        out_shape=jax.ShapeDtypeStruct((db, 1, POOL_WIDTH), F32),
        scratch_shapes=[pltpu.VMEM((2 * HALO, POOL_WIDTH), F32)],
        compiler_params=_cparams(("arbitrary",)),
        name="sample_pool",
    )(us.reshape(db, 1, POOL_WIDTH), state_pool, w_pool_bf16, pool_scale)
    return pool.reshape(db, POOL_WIDTH)


def _layer_norm(xf, g, b):
    mu = jnp.mean(xf, axis=-1, keepdims=True)
    xc = xf - mu
    var = jnp.mean(xc * xc, axis=-1, keepdims=True)
    return xc * lax.rsqrt(var + LN_EPS) * g + b


def _route(logits):
    lane = lax.broadcasted_iota(jnp.int32, logits.shape, 1)
    lane_f = lane.astype(F32)
    neg = jnp.float32(-jnp.inf)

    def first_argmax(v, vmax):
        return jnp.min(jnp.where(v == vmax, lane_f, jnp.float32(LANES)), axis=-1, keepdims=True)

    gl = jnp.where(lane < N_GROUPS, logits, neg)
    gmax = jnp.max(gl, axis=-1, keepdims=True)
    g_sel = first_argmax(gl, gmax)
    g_gate = 1.0 / jnp.sum(jnp.exp(gl - gmax), axis=-1, keepdims=True)
    lane_group = (((lane + (EXPERTS_PER_GROUP - N_GROUPS)) >> 3) - 1).astype(F32)
    el = jnp.where(lane_group == g_sel, logits, neg)
    v1 = jnp.max(el, axis=-1, keepdims=True)
    i1 = first_argmax(el, v1)
    el2 = jnp.where(lane_f == i1, neg, el)
    v2 = jnp.max(el2, axis=-1, keepdims=True)
    i2 = first_argmax(el2, v2)
    t = jnp.exp(v2 - v1)
    w1 = (1.0 / (1.0 + t)) * g_gate
    w2 = (t / (1.0 + t)) * g_gate
    route = jnp.where(lane == 0, i1 - N_GROUPS, 0.0)
    route = jnp.where(lane == 1, i2 - N_GROUPS, route)
    route = jnp.where(lane == 2, w1, route)
    route = jnp.where(lane == 3, w2, route)
    return route


def _post_kernel(o0_ref, o1_ref, o2_ref, l0_ref, l1_ref, l2_ref, pool_ref, x_ref, hexp_ref,
                 wo_ref, g1_ref, b1_ref, wr_ref, br_ref, x1_ref, route_ref):
    l0, l1, l2 = l0_ref[...], l1_ref[...], l2_ref[...]
    lmax = jnp.maximum(jnp.maximum(l0, l1), l2)
    e0, e1, e2 = jnp.exp(l0 - lmax), jnp.exp(l1 - lmax), jnp.exp(l2 - lmax)
    den = e0 + e1 + e2
    hexp = hexp_ref[...]
    attn = jnp.zeros((x_ref.shape[0], ATTN_WIDTH), F32)
    for o_ref, e in ((o0_ref, e0), (o1_ref, e1), (o2_ref, e2)):
        w_x = jnp.dot(e / den, hexp, precision=HIGHEST, preferred_element_type=F32)
        o = jnp.concatenate([o_ref[c] for c in range(ATTN_WIDTH // LANES)], axis=1)
        attn = attn + o * w_x
    h = _mm(attn, wo_ref[0:ATTN_WIDTH, :]) + _mm(pool_ref[...], wo_ref[ATTN_WIDTH:, :])
    x1 = _layer_norm(ALPHA * x_ref[...] + h, g1_ref[...], b1_ref[...])
    x1_ref[...] = x1
    logits = jnp.dot(x1, wr_ref[...], precision=HIGHEST, preferred_element_type=F32) + br_ref[...]
    route_ref[...] = _route(logits)


def _row_spec(tm, width):
    return pl.BlockSpec((tm, width), lambda i: (i, 0))


def _const_spec(shape):
    return pl.BlockSpec(shape, lambda i: (0,) * len(shape), pipeline_mode=pl.Buffered(1))


def _post(os_, lses, pool, x, wo, g1, b1, wr, br, *, tm, name):
    m, d = x.shape
    hexp = _head_expand()
    return pl.pallas_call(
        _post_kernel,
        grid=(m // tm,),
        in_specs=[pl.BlockSpec((ATTN_WIDTH // LANES, tm, LANES), lambda i: (0, i, 0))] * 3
        + [_row_spec(tm, LANES)] * 3
        + [_row_spec(tm, POOL_WIDTH), _row_spec(tm, d), _const_spec((LANES, ATTN_WIDTH)),
           _const_spec((d, d)), _const_spec((1, d)), _const_spec((1, d)),
           _const_spec((d, LANES)), _const_spec((1, LANES))],
        out_specs=[_row_spec(tm, d), _row_spec(tm, LANES)],
        out_shape=[jax.ShapeDtypeStruct((m, d), F32), jax.ShapeDtypeStruct((m, LANES), F32)],
        compiler_params=_cparams(("arbitrary",)),
        name=name,
    )(*os_, *lses, pool, x, hexp, wo, g1, b1, wr, br)


def _routing_plan(pair_expert, n_tiles, n_prompt_pairs):
    p = pair_expert.shape[0]
    experts = jnp.arange(N_EXPERTS, dtype=jnp.int32)
    onehot = (pair_expert[:, None] == experts[None, :]).astype(jnp.int32)
    csum = jnp.cumsum(onehot, axis=0)
    rank = jnp.take_along_axis(csum, pair_expert[:, None], axis=1)[:, 0] - 1
    counts = csum[-1]
    prompt_counts = csum[n_prompt_pairs - 1]
    tiles_per = (counts + MOE_TILE - 1) // MOE_TILE
    tile_end = jnp.cumsum(tiles_per)
    tile_start = tile_end - tiles_per
    dest = tile_start[pair_expert] * MOE_TILE + rank
    n_used = tile_end[-1]
    tile_ids = jnp.arange(n_tiles, dtype=jnp.int32)
    tile_expert = jnp.sum((tile_ids[:, None] >= tile_end[None, :]).astype(jnp.int32), axis=1)
    last_expert = jnp.sum((n_used - 1 >= tile_end).astype(jnp.int32))
    tile_expert = jnp.where(tile_ids < n_used, tile_expert, last_expert).astype(jnp.int32)
    tile_prompt_rows = jnp.clip(
        prompt_counts[tile_expert] - (tile_ids - tile_start[tile_expert]) * MOE_TILE, 0, MOE_TILE)
    tile_prompt_rows = jnp.where(tile_ids < n_used, tile_prompt_rows, 0).astype(jnp.int32)
    first_decode_token = n_prompt_pairs // 2
    slot_token = jnp.full((n_tiles * MOE_TILE,), first_decode_token, jnp.int32).at[dest].set(
        jnp.arange(p, dtype=jnp.int32) // 2)
    return (dest.astype(jnp.int32), slot_token, tile_expert, n_used.reshape(1).astype(jnp.int32),
            tile_prompt_rows)


def _moe_kernel(te_ref, nu_ref, st_ref, npr_ref, xp_ref, xs_ref, wg_ref, wu_ref, wd_ref, ys_ref,
                buf_ref, wgb_ref, wub_ref, wdb_ref, sem_ref, *, n_prompt):
    c = pl.program_id(0)
    n_tiles = pl.num_programs(0)
    n_used = nu_ref[0]
    d = buf_ref.shape[-1]

    def issue(tile, slot):
        split = npr_ref[tile]

        def row(src_ref, offset):
            def body(i, carry):
                t = st_ref[tile * MOE_TILE + i] - offset
                pltpu.make_async_copy(src_ref.at[pl.ds(t, 1), :], buf_ref.at[slot, pl.ds(i, 1), :],
                                      sem_ref.at[slot]).start()
                return carry
            return body

        lax.fori_loop(0, split, row(xp_ref, 0), 0)
        lax.fori_loop(split, MOE_TILE, row(xs_ref, n_prompt), 0)

    def wait(slot):
        pltpu.make_async_copy(xp_ref.at[pl.ds(0, MOE_TILE), :], buf_ref.at[slot], sem_ref.at[slot]).wait()

    slot = c % 2

    @pl.when((c == 0) & (n_used > 0))
    def _():
        issue(0, 0)

    @pl.when((c + 1 < n_tiles) & (c + 1 < n_used))
    def _():
        issue(c + 1, 1 - slot)

    prev_expert = te_ref[jnp.maximum(c - 1, 0)]

    @pl.when((c == 0) | (te_ref[c] != prev_expert))
    def _():
        wgb_ref[...] = wg_ref[...].astype(BF16)
        wub_ref[...] = wu_ref[...].astype(BF16)
        wdb_ref[...] = wd_ref[...].astype(BF16)

    @pl.when(c < n_used)
    def _():
        wait(slot)
        x = buf_ref[slot].astype(BF16)
        gate = jnp.dot(x, wgb_ref[...], preferred_element_type=F32)
        up = jnp.dot(x, wub_ref[...], preferred_element_type=F32)
        h = jax.nn.silu(gate) * up
        ys_ref[...] = jnp.dot(h.astype(BF16), wdb_ref[...], preferred_element_type=F32)

    @pl.when(c >= n_used)
    def _():
        ys_ref[...] = jnp.zeros(ys_ref.shape, ys_ref.dtype)


def _moe(x1p, x1s, slot_token, tile_expert, n_used, tile_prompt_rows, w_gate, w_up, w_down, n_tiles):
    n_prompt, d = x1p.shape
    f = w_gate.shape[-1]
    grid_spec = pltpu.PrefetchScalarGridSpec(
        num_scalar_prefetch=4,
        grid=(n_tiles,),
        in_specs=[
            pl.BlockSpec(memory_space=pl.ANY),
            pl.BlockSpec(memory_space=pl.ANY),
            pl.BlockSpec((None, d, f), lambda c, te, nu, st, npr: (te[c], 0, 0)),
            pl.BlockSpec((None, d, f), lambda c, te, nu, st, npr: (te[c], 0, 0)),
            pl.BlockSpec((None, f, d), lambda c, te, nu, st, npr: (te[c], 0, 0)),
        ],
        out_specs=pl.BlockSpec((MOE_TILE, d), lambda c, te, nu, st, npr: (c, 0)),
        scratch_shapes=[
            pltpu.VMEM((2, MOE_TILE, d), F32),
            pltpu.VMEM((d, f), BF16),
            pltpu.VMEM((d, f), BF16),
            pltpu.VMEM((f, d), BF16),
            pltpu.SemaphoreType.DMA((2,)),
        ],
    )
    return pl.pallas_call(
        functools.partial(_moe_kernel, n_prompt=n_prompt),
        grid_spec=grid_spec,
        out_shape=jax.ShapeDtypeStruct((n_tiles * MOE_TILE, d), F32),
        compiler_params=_cparams(("arbitrary",)),
        name="moe",
    )(tile_expert, n_used, slot_token, tile_prompt_rows, x1p, x1s, w_gate, w_up, w_down)


def _final_kernel(dest_ref, ys_ref, x1_ref, route_ref, p_ref, g2_ref, b2_ref, wpg_ref, wp_ref, o_ref,
                  buf_ref, sem_ref, *, tm, pair0):
    i = pl.program_id(0)
    n_steps = pl.num_programs(0)

    def issue(step, slot):
        def body(r, carry):
            for k in range(2):
                dslot = dest_ref[pair0 + (step * tm + r) * 2 + k]
                pltpu.make_async_copy(ys_ref.at[pl.ds(dslot, 1), :], buf_ref.at[slot, k, pl.ds(r, 1), :],
                                      sem_ref.at[slot]).start()
            return carry

        lax.fori_loop(0, tm, body, 0, unroll=GATHER_UNROLL)

    def wait(slot):
        for k in range(2):
            pltpu.make_async_copy(ys_ref.at[pl.ds(0, tm), :], buf_ref.at[slot, k], sem_ref.at[slot]).wait()

    slot = i % 2

    @pl.when(i == 0)
    def _():
        issue(0, 0)

    @pl.when(i + 1 < n_steps)
    def _():
        issue(i + 1, 1 - slot)

    wait(slot)
    route = route_ref[...]
    y = route[:, 2:3] * buf_ref[slot, 0] + route[:, 3:4] * buf_ref[slot, 1]
    x2 = _layer_norm(ALPHA * x1_ref[...] + y, g2_ref[...], b2_ref[...])
    gate = jax.nn.sigmoid(jnp.dot(x2.astype(BF16), wpg_ref[...], preferred_element_type=F32))
    ple = jnp.dot(p_ref[...].astype(BF16), wp_ref[...], preferred_element_type=F32)
    o_ref[...] = x2 + gate * ple


def _final(dest, ys, x1, route, p, g2, b2, wpg, wp, *, tm, pair0):
    m, d = x1.shape
    pd = p.shape[1]
    assert m % tm == 0
    grid_spec = pltpu.PrefetchScalarGridSpec(
        num_scalar_prefetch=1,
        grid=(m // tm,),
        in_specs=[
            pl.BlockSpec(memory_space=pl.ANY),
            pl.BlockSpec((tm, d), lambda i, dst: (i, 0)),
            pl.BlockSpec((tm, LANES), lambda i, dst: (i, 0)),
            pl.BlockSpec((tm, pd), lambda i, dst: (i, 0)),
            pl.BlockSpec((1, d), lambda i, dst: (0, 0)),
            pl.BlockSpec((1, d), lambda i, dst: (0, 0)),
            pl.BlockSpec((d, d), lambda i, dst: (0, 0)),
            pl.BlockSpec((pd, d), lambda i, dst: (0, 0)),
        ],
        out_specs=pl.BlockSpec((tm, d), lambda i, dst: (i, 0)),
        scratch_shapes=[pltpu.VMEM((2, 2, tm, d), F32), pltpu.SemaphoreType.DMA((2,))],
    )
    return pl.pallas_call(
        functools.partial(_final_kernel, tm=tm, pair0=pair0),
        grid_spec=grid_spec,
        out_shape=jax.ShapeDtypeStruct((m, d), F32),
        compiler_params=_cparams(("arbitrary",)),
        name="final",
    )(dest, ys, x1, route, p, g2, b2, wpg, wp)


def _pick_tile(m, pref):
    t = pref
    while m % t:
        t //= 2
    return t


def _layer(xp, xs, caches, state_pool, pp, ps, w):
    s, d = xp.shape
    db = xs.shape[0]

    cos_p, sin_p = _rope_tables(jnp.arange(s, dtype=jnp.int32))
    dils = tuple(dil for _, dil in DILATED_PATTERNS)
    qkv_p, u_p = _in_proj(xp, w["w_in"], cos_p, sin_p, tm=_pick_tile(s, 512), dils=dils, qkv_dtype=BF16)
    os_, lses = [], []
    for pi, (window, dil) in enumerate(DILATED_PATTERNS):
        o, l = _attn_pattern(qkv_p[pi], pi, dil, window)
        os_.append(o)
        lses.append(l)
    pool_p = _pool_prompt(u_p, w["w_pool"], w["pool_scale"], tm=_pick_tile(s, 512))

    cos_s, sin_s = _rope_tables(jnp.full((db,), PAST_LEN, jnp.int32))
    (qkv_s,), u_s = _in_proj(xs, w["w_in_f32"], cos_s, sin_s, tm=db, dils=None, qkv_dtype=F32)
    zt_s = qkv_s.T
    os_s, lses_s, kv_s = [], [], []
    for pi, ((window, dil), c) in enumerate(zip(DILATED_PATTERNS, caches)):
        cn, ot, lt = _sample_attn(zt_s, jnp.transpose(c, (0, 2, 3, 4, 1)), pi, dil)
        kv_s.append(jnp.transpose(cn, (0, 4, 1, 2, 3)))
        os_s.append(jnp.transpose(ot.T[:db].reshape(db, ATTN_WIDTH // LANES, LANES), (1, 0, 2)))
        lses_s.append(lt.T[:db])
    pool_s = _sample_pool(u_s, state_pool, w["w_pool_f32"], w["pool_scale"])

    post_w = (w["ln1_g"], w["ln1_b"], w["w_router"], w["b_router"])
    x1_p, route_p = _post(os_, lses, pool_p, xp, w["w_out"], *post_w, tm=_pick_tile(s, 256), name="post_prompt")
    x1_s, route_s = _post(os_s, lses_s, pool_s, xs, w["w_out_f32"], *post_w, tm=db, name="post_sample")

    pair_expert = jnp.concatenate([route_p[:, 0:2].reshape(-1), route_s[:, 0:2].reshape(-1)]).astype(jnp.int32)
    n_pairs = pair_expert.shape[0]
    n_tiles = -(-n_pairs // MOE_TILE) + N_EXPERTS
    dest, slot_token, tile_expert, n_used, tile_prompt_rows = _routing_plan(pair_expert, n_tiles, 2 * s)
    ys = _moe(x1_p, x1_s, slot_token, tile_expert, n_used, tile_prompt_rows,
              w["w_gate"], w["w_up"], w["w_down"], n_tiles)

    fin_w = (w["ln2_g"], w["ln2_b"], w["w_ple_gate"], w["w_ple"])
    y_p = _final(dest, ys, x1_p, route_p, pp, *fin_w, tm=_pick_tile(s, 256), pair0=0)
    y_s = _final(dest, ys, x1_s, route_s, ps, *fin_w, tm=db, pair0=2 * s)

    kv_p = []
    for pi, (window, dil) in enumerate(DILATED_PATTERNS):
        keep = min(window, s)
        kv = qkv_p[pi][:, (s - keep) // dil:, ATTN_WIDTH:3 * ATTN_WIDTH].astype(F32)
        kv = jnp.transpose(kv, (1, 0, 2))
        kv_p.append(kv.reshape(1, keep, 2, N_HEADS, HEAD_DIM))
    pool_state_p = u_p[s - POOL_STATE:][None]
    pool_state_s = jnp.concatenate([state_pool[:, 1:], u_s[:, None, :]], axis=1)
    return y_p, y_s, kv_p, pool_state_p, kv_s, pool_state_s


def kernel(x_prompt, x_sample, cache_kv_w128_d1, cache_kv_w512_d4, cache_kv_w2048_d16, state_pool, p_prompt, p_sample, w_in, w_out, w_pool, pool_scale, ln1_g, ln1_b, w_group_router, b_group_router, w_expert_router, b_expert_router, w_gate, w_up, w_down, ln2_g, ln2_b, w_ple, w_ple_gate):
    assert w_in.shape[0] == DEPTH == 1 and x_prompt.shape[0] == 1 and x_sample.shape[1] == 1
    d = x_prompt.shape[-1]
    pad = LANES - N_GROUPS - N_EXPERTS
    w = {
        "w_in": w_in[0].astype(BF16), "w_in_f32": w_in[0],
        "w_out": w_out[0].astype(BF16), "w_out_f32": w_out[0],
        "w_pool": w_pool[0].astype(BF16), "w_pool_f32": w_pool[0],
        "pool_scale": pool_scale[0].reshape(1, POOL_WIDTH),
        "ln1_g": ln1_g[0].reshape(1, d), "ln1_b": ln1_b[0].reshape(1, d),
        "ln2_g": ln2_g[0].reshape(1, d), "ln2_b": ln2_b[0].reshape(1, d),
        "w_router": jnp.concatenate([w_group_router[0], w_expert_router[0], jnp.zeros((d, pad), F32)], axis=1),
        "b_router": jnp.concatenate([b_group_router[0], b_expert_router[0], jnp.zeros((pad,), F32)]).reshape(1, LANES),
        "w_gate": w_gate[0], "w_up": w_up[0], "w_down": w_down[0],
        "w_ple": w_ple[0].astype(BF16),
        "w_ple_gate": w_ple_gate[0].astype(BF16),
    }
    caches = [cache_kv_w128_d1[0], cache_kv_w512_d4[0], cache_kv_w2048_d16[0]]
    y_p, y_s, kv_p, pool_p, kv_s, pool_s = _layer(
        x_prompt[0], x_sample[:, 0], caches, state_pool[0], p_prompt[0, 0], p_sample[0, :, 0], w)
    return (y_p[None], y_s[:, None], kv_p[0][None], kv_p[1][None], kv_p[2][None], pool_p[None],
            kv_s[0][None], kv_s[1][None], kv_s[2][None], pool_s[None])
```

```python
import functools

import jax
import jax.numpy as jnp
from jax import lax
from jax.experimental import pallas as pl
from jax.experimental.pallas import tpu as pltpu

F32 = jnp.float32
BF16 = jnp.bfloat16

PAST_LEN = 8192
HEAD_DIM = 64
N_HEADS = 16
ATTN_WIDTH = N_HEADS * HEAD_DIM
DILATED_PATTERNS = ((128, 1), (512, 4), (2048, 16))
N_PATTERNS = len(DILATED_PATTERNS)
BAND = 128
POOL_WINDOWS = (2, 4, 8, 16)
POOL_GROUP = 256
POOL_WIDTH = POOL_GROUP * len(POOL_WINDOWS)
POOL_STATE = max(POOL_WINDOWS) - 1
ROT_DIM = HEAD_DIM // 4
ROPE_THETA = 500000.0
QKV_WIDTH = N_PATTERNS * 3 * ATTN_WIDTH
N_GROUPS = 4
EXPERTS_PER_GROUP = 8
N_EXPERTS = N_GROUPS * EXPERTS_PER_GROUP
DEPTH = 1
ALPHA = (2.0 * DEPTH) ** 0.25
LN_EPS = 1e-5
NEG_INF = -1e30

LANES = 128
SUBLANES = 8
VMEM_LIMIT = 48 * 1024 * 1024

MOE_TILE = 256
HALO = 16
GATHER_UNROLL = 8


def _cparams(semantics):
    return pltpu.CompilerParams(dimension_semantics=semantics, vmem_limit_bytes=VMEM_LIMIT)


def _rope_tables(pos):
    inv_freq = ROPE_THETA ** (-jnp.arange(0, ROT_DIM, 2, dtype=F32) / ROT_DIM)
    ang = pos.astype(F32)[:, None] * inv_freq[None, :]
    cos, sin = jnp.cos(ang), jnp.sin(ang)
    t = pos.shape[0]
    rest = HEAD_DIM - ROT_DIM
    c64 = jnp.concatenate([cos, cos, jnp.ones((t, rest), F32)], -1)
    s64 = jnp.concatenate([-sin, sin, jnp.zeros((t, rest), F32)], -1)
    return jnp.tile(c64, (1, LANES // HEAD_DIM)), jnp.tile(s64, (1, LANES // HEAD_DIM))


def _rope_chunk(xc, cos, sin):
    half = ROT_DIM // 2
    lane = lax.broadcasted_iota(jnp.int32, xc.shape, 1) & (HEAD_DIM - 1)
    upper = pltpu.roll(xc, LANES - half, 1)
    lower = pltpu.roll(xc, half, 1)
    partner = jnp.where(lane < half, upper, lower)
    return xc * cos + partner * sin


def _split_bf16(a):
    hi = a.astype(BF16)
    return hi, (a - hi.astype(F32)).astype(BF16)


def _mm(a, w):
    if w.dtype == BF16:
        return jnp.dot(a.astype(BF16), w, preferred_element_type=F32)
    ah, al = _split_bf16(a.astype(F32))
    wh, wl = _split_bf16(w)
    return (jnp.dot(ah, wh, preferred_element_type=F32) + jnp.dot(al, wh, preferred_element_type=F32)
            + jnp.dot(ah, wl, preferred_element_type=F32))


ROLE_TILES = 3


def _in_proj_kernel(x_ref, w_ref, cos_ref, sin_ref, *refs, dils):
    n_out = 1 if dils is None else len(dils)
    qkv_refs, u_ref, xb_ref, stage_ref = refs[:n_out], refs[n_out], refs[n_out + 1], refs[n_out + 2]
    j = pl.program_id(1)
    n_chunks, tm, _ = stage_ref.shape
    n_qkv_tiles = N_PATTERNS * ROLE_TILES

    def chunk(c):
        return slice(c * LANES, (c + 1) * LANES)

    if w_ref.dtype == BF16:
        @pl.when(j == 0)
        def _():
            xb_ref[...] = x_ref[...].astype(BF16)

        acc = jnp.dot(xb_ref[...], w_ref[...], preferred_element_type=F32)
    else:
        acc = _mm(x_ref[...], w_ref[...])
    role = j % ROLE_TILES
    is_qkv = j < n_qkv_tiles

    @pl.when(is_qkv & (role < 2))
    def _():
        cos = cos_ref[...]
        sin = sin_ref[...]
        for c in range(n_chunks):
            stage_ref[c] = _rope_chunk(acc[:, chunk(c)], cos, sin)

    @pl.when(is_qkv & (role == 2))
    def _():
        for c in range(n_chunks):
            stage_ref[c] = acc[:, chunk(c)]

    if dils is None:
        @pl.when(is_qkv)
        def _():
            for c in range(n_chunks):
                qkv_refs[0][:, chunk(c)] = stage_ref[c].astype(qkv_refs[0].dtype)
    else:
        for pi, dil in enumerate(dils):
            @pl.when(is_qkv & (j // ROLE_TILES == pi))
            def _(pi=pi, dil=dil):
                for r in range(dil):
                    for c in range(n_chunks):
                        rows = stage_ref[c, pl.ds(r, tm // dil, stride=dil), :] if dil > 1 else stage_ref[c]
                        qkv_refs[pi][r, :, chunk(c)] = rows.astype(qkv_refs[pi].dtype)

    @pl.when(j >= n_qkv_tiles)
    def _():
        u_ref[...] = acc


def _in_proj(x, w, cos_t, sin_t, *, tm, dils, qkv_dtype):
    m, d = x.shape
    tn = ATTN_WIDTH
    assert m % tm == 0 and w.shape[1] == QKV_WIDTH + POOL_WIDTH and POOL_WIDTH == tn
    n_qkv_tiles = N_PATTERNS * ROLE_TILES
    if dils is None:
        qkv_specs = [pl.BlockSpec((tm, tn), lambda i, j: (i, jnp.minimum(j, n_qkv_tiles - 1)))]
        qkv_shapes = [jax.ShapeDtypeStruct((m, QKV_WIDTH), qkv_dtype)]
    else:
        qkv_specs, qkv_shapes = [], []
        for pi, dil in enumerate(dils):
            assert tm % (dil * 2 * SUBLANES) == 0
            qkv_specs.append(pl.BlockSpec(
                (dil, tm // dil, tn),
                lambda i, j, pi=pi: (0, i, jnp.clip(j - pi * ROLE_TILES, 0, ROLE_TILES - 1))))
            qkv_shapes.append(jax.ShapeDtypeStruct((dil, m // dil, ROLE_TILES * tn), qkv_dtype))
    outs = pl.pallas_call(
        functools.partial(_in_proj_kernel, dils=dils),
        grid=(m // tm, n_qkv_tiles + 1),
        in_specs=[
            pl.BlockSpec((tm, d), lambda i, j: (i, 0)),
            pl.BlockSpec((d, tn), lambda i, j: (0, j)),
            pl.BlockSpec((tm, LANES), lambda i, j: (i, 0)),
            pl.BlockSpec((tm, LANES), lambda i, j: (i, 0)),
        ],
        out_specs=qkv_specs + [pl.BlockSpec((tm, tn), lambda i, j: (i, 0))],
        out_shape=qkv_shapes + [jax.ShapeDtypeStruct((m, POOL_WIDTH), F32)],
        scratch_shapes=[pltpu.VMEM((tm, d), BF16), pltpu.VMEM((tn // LANES, tm, LANES), F32)],
        compiler_params=_cparams(("arbitrary", "arbitrary")),
        name="in_proj",
    )(x, w, cos_t, sin_t)
    return outs[:-1], outs[-1]


def _attn_kernel(q_ref, kp_ref, kc_ref, vp_ref, vc_ref, o_ref, lse_ref,
                 lim_ref, s_ref, e_ref, m_ref, den_ref, *, r_max, dil):
    n = pl.program_id(0)
    r = pl.program_id(1)
    qi = lax.broadcasted_iota(jnp.int32, (BAND, 2 * BAND), 0)
    ki = lax.broadcasted_iota(jnp.int32, (BAND, 2 * BAND), 1)
    dist = BAND + qi - ki
    valid = (dist >= 0) & (dist <= r_max) & ((n > 0) | (ki >= BAND))
    lim_ref[...] = jnp.where(valid, jnp.float32(jnp.finfo(F32).max), jnp.float32(NEG_INF))

    lane = lax.broadcasted_iota(jnp.int32, (BAND, LANES), 1)
    heads_per_chunk = LANES // HEAD_DIM
    n_chunks = ATTN_WIDTH // LANES
    lane_head = lane >> (HEAD_DIM.bit_length() - 1)
    qscale = [jnp.where(lane_head == hh, HEAD_DIM ** -0.5, 0.0).astype(BF16) for hh in range(heads_per_chunk)]
    nt = (((1,), (1,)), ((), ()))

    for c in range(n_chunks):
        sl = slice(c * LANES, (c + 1) * LANES)
        q2 = q_ref[:, sl]
        kp, kc = kp_ref[:, sl], kc_ref[:, sl]
        for hh in range(heads_per_chunk):
            h = c * heads_per_chunk + hh
            qm = q2 * qscale[hh]
            s = jnp.concatenate(
                [lax.dot_general(qm, kp, nt, preferred_element_type=F32),
                 lax.dot_general(qm, kc, nt, preferred_element_type=F32)], axis=1)
            s = jnp.minimum(s, lim_ref[...])
            s_ref[h] = s
            m_ref[h] = jnp.broadcast_to(jnp.max(s, axis=-1, keepdims=True), (BAND, LANES))

    lse_acc = jnp.zeros((BAND, LANES), F32)
    for h in range(N_HEADS):
        m = m_ref[h]
        e = jnp.exp(s_ref[h] - jnp.concatenate([m, m], axis=1))
        e_ref[h] = e.astype(BF16)
        den = jnp.broadcast_to(jnp.sum(e, axis=-1, keepdims=True), (BAND, LANES))
        den_ref[h] = den
        lse_acc = jnp.where(lane == h, m + jnp.log(den), lse_acc)

    rows = slice(None) if dil == 1 else pl.ds(r, BAND, stride=dil)
    lse_ref[rows, :] = lse_acc
    for c in range(n_chunks):
        sl = slice(c * LANES, (c + 1) * LANES)
        vp, vc = vp_ref[:, sl], vc_ref[:, sl]
        o2 = jnp.zeros((BAND, LANES), F32)
        for hh in range(heads_per_chunk):
            h = c * heads_per_chunk + hh
            pv = (jnp.dot(e_ref[h, :, :BAND], vp, preferred_element_type=F32)
                  + jnp.dot(e_ref[h, :, BAND:], vc, preferred_element_type=F32))
            o2 = jnp.where(lane_head == hh, pv / den_ref[h], o2)
        o_ref[c, rows, :] = o2


def _attn_pattern(qkv, pi, dil, window):
    s = qkv.shape[0] * qkv.shape[1]
    assert qkv.shape[0] == dil and s % (dil * BAND) == 0
    nb = s // (dil * BAND)
    r_max = window // dil

    def cur(role):
        return lambda n, r: (r, n, role)

    def prev(role):
        return lambda n, r: (r, jnp.maximum(n - 1, 0), role)

    blk = (None, BAND, ATTN_WIDTH)
    span = BAND * dil
    return pl.pallas_call(
        functools.partial(_attn_kernel, r_max=r_max, dil=dil),
        grid=(nb, dil),
        in_specs=[
            pl.BlockSpec(blk, cur(0)),
            pl.BlockSpec(blk, prev(1)),
            pl.BlockSpec(blk, cur(1)),
            pl.BlockSpec(blk, prev(2)),
            pl.BlockSpec(blk, cur(2)),
        ],
        out_specs=[
            pl.BlockSpec((ATTN_WIDTH // LANES, span, LANES), lambda n, r: (0, n, 0)),
            pl.BlockSpec((span, LANES), lambda n, r: (n, 0)),
        ],
        out_shape=[
            jax.ShapeDtypeStruct((ATTN_WIDTH // LANES, s, LANES), F32),
            jax.ShapeDtypeStruct((s, LANES), F32),
        ],
        scratch_shapes=[
            pltpu.VMEM((BAND, 2 * BAND), F32),
            pltpu.VMEM((N_HEADS, BAND, 2 * BAND), F32),
            pltpu.VMEM((N_HEADS, BAND, 2 * BAND), BF16),
            pltpu.VMEM((N_HEADS, BAND, LANES), F32),
            pltpu.VMEM((N_HEADS, BAND, LANES), F32),
        ],
        compiler_params=_cparams(("arbitrary", "arbitrary")),
        name=f"attn_p{pi}",
    )(qkv, qkv, qkv, qkv, qkv)


def _pool_groups(buf_ref, tm, pos0, w_pool_ref, scale_ref, store):
    row = lax.broadcasted_iota(jnp.int32, (tm, 1), 0) + pos0
    for g, win in enumerate(POOL_WINDOWS):
        sl = slice(g * POOL_GROUP, (g + 1) * POOL_GROUP)
        cur = buf_ref[HALO:HALO + tm, sl]
        acc = cur
        for jj in range(1, win):
            acc = acc + buf_ref[HALO - jj:HALO - jj + tm, sl]
        cnt = jnp.minimum(row + 1, win).astype(F32)
        d = acc / cnt - cur
        y = _mm(d, w_pool_ref[g])
        store(g, y * scale_ref[:, sl])


def _pool_kernel(u_ref, uprev_ref, w_pool_ref, scale_ref, o_ref, buf_ref, *, tm):
    i = pl.program_id(0)
    buf_ref[0:HALO, :] = jnp.where(i > 0, uprev_ref[...], jnp.zeros_like(uprev_ref))
    buf_ref[HALO:HALO + tm, :] = u_ref[...]

    def store(g, val):
        o_ref[:, g * POOL_GROUP:(g + 1) * POOL_GROUP] = val.astype(o_ref.dtype)

    _pool_groups(buf_ref, tm, i * tm, w_pool_ref, scale_ref, store)


def _pool_prompt(u, w_pool_bf16, pool_scale, *, tm):
    s = u.shape[0]
    assert s % tm == 0 and tm % HALO == 0
    per = tm // HALO
    return pl.pallas_call(
        functools.partial(_pool_kernel, tm=tm),
        grid=(s // tm,),
        in_specs=[
            pl.BlockSpec((tm, POOL_WIDTH), lambda i: (i, 0)),
            pl.BlockSpec((HALO, POOL_WIDTH), lambda i: (jnp.maximum(i * per - 1, 0), 0)),
            pl.BlockSpec((len(POOL_WINDOWS), POOL_GROUP, POOL_GROUP), lambda i: (0, 0, 0)),
            pl.BlockSpec((1, POOL_WIDTH), lambda i: (0, 0)),
        ],
        out_specs=pl.BlockSpec((tm, POOL_WIDTH), lambda i: (i, 0)),
        out_shape=jax.ShapeDtypeStruct((s, POOL_WIDTH), BF16),
        scratch_shapes=[pltpu.VMEM((HALO + tm, POOL_WIDTH), F32)],
        compiler_params=_cparams(("arbitrary",)),
        name="pool",
    )(u, u, w_pool_bf16, pool_scale)


def _sample_attn_kernel(zt_ref, c_ref, cn_ref, ot_ref, lt_ref, *, dil, hps, length):
    b = pl.program_id(0)
    g = pl.program_id(1)

    @pl.when((b == 0) & (g == 0))
    def _():
        ot_ref[...] = jnp.zeros(ot_ref.shape, F32)
        lt_ref[...] = jnp.zeros(lt_ref.shape, F32)

    rows = hps * HEAD_DIM
    base = pl.multiple_of(g * rows, rows)
    lane_z = lax.broadcasted_iota(jnp.int32, (rows, zt_ref.shape[1]), 1)

    def column(role):
        blk = zt_ref[pl.ds(role * ATTN_WIDTH + base, rows), :]
        return jnp.sum(jnp.where(lane_z == b, blk, 0.0), axis=-1, keepdims=True)

    qc, knc, vnc = column(0), column(1), column(2)
    scale = HEAD_DIM ** -0.5
    lane = lax.broadcasted_iota(jnp.int32, (1, length), 1)
    key_ok = (lane & (dil - 1)) == 0
    last = lane == length - 1
    lane_o = lax.broadcasted_iota(jnp.int32, (HEAD_DIM, LANES), 1)
    lane_l = lax.broadcasted_iota(jnp.int32, (1, LANES), 1)
    for h in range(hps):
        sl = slice(h * HEAD_DIM, (h + 1) * HEAD_DIM)
        q, kn, vn = qc[sl], knc[sl], vnc[sl]
        k = c_ref[0, h]
        v = c_ref[1, h]
        s = jnp.where(key_ok, jnp.sum(k * q, axis=0, keepdims=True) * scale, NEG_INF)
        s_n = jnp.sum(kn * q, axis=0, keepdims=True) * scale
        m = jnp.maximum(jnp.max(s, axis=-1, keepdims=True), s_n)
        e = jnp.exp(s - m)
        e_n = jnp.exp(s_n - m)
        den = jnp.sum(e, axis=-1, keepdims=True) + e_n
        o = (jnp.sum(v * e, axis=-1, keepdims=True) + vn * e_n) / den
        lse = m + jnp.log(den)
        cn_ref[0, h] = jnp.where(last, kn, pltpu.roll(k, length - 1, 1))
        cn_ref[1, h] = jnp.where(last, vn, pltpu.roll(v, length - 1, 1))
        r0 = pl.multiple_of(base + h * HEAD_DIM, HEAD_DIM)
        ot_ref[pl.ds(r0, HEAD_DIM), :] = jnp.where(lane_o == b, o, ot_ref[pl.ds(r0, HEAD_DIM), :])
        head = g * hps + h
        lt_ref[pl.ds(head, 1), :] = jnp.where(lane_l == b, lse, lt_ref[pl.ds(head, 1), :])


def _sample_attn(zt, c_t, pi, dil):
    db, _, _, _, length = c_t.shape
    assert db <= LANES and length == dil * BAND
    hps = min(N_HEADS, max(1, (4 * 2048) // length))
    blk = (None, 2, hps, HEAD_DIM, length)
    return pl.pallas_call(
        functools.partial(_sample_attn_kernel, dil=dil, hps=hps, length=length),
        grid=(db, N_HEADS // hps),
        in_specs=[
            pl.BlockSpec((3 * ATTN_WIDTH, db), lambda b, g: (pi, 0)),
            pl.BlockSpec(blk, lambda b, g: (b, 0, g, 0, 0)),
        ],
        out_specs=[
            pl.BlockSpec(blk, lambda b, g: (b, 0, g, 0, 0)),
            pl.BlockSpec((ATTN_WIDTH, LANES), lambda b, g: (0, 0)),
            pl.BlockSpec((LANES, LANES), lambda b, g: (0, 0)),
        ],
        out_shape=[
            jax.ShapeDtypeStruct(c_t.shape, c_t.dtype),
            jax.ShapeDtypeStruct((ATTN_WIDTH, LANES), F32),
            jax.ShapeDtypeStruct((LANES, LANES), F32),
        ],
        compiler_params=_cparams(("arbitrary", "arbitrary")),
        name=f"sample_attn_p{pi}",
    )(zt, c_t)


def _sample_pool_kernel(u_ref, sp_ref, w_pool_ref, scale_ref, pool_ref, buf_ref):
    buf_ref[...] = jnp.zeros(buf_ref.shape, F32)
    buf_ref[HALO - POOL_STATE:HALO, :] = sp_ref[...]
    buf_ref[HALO:HALO + 1, :] = u_ref[...]

    def store(g, val):
        pool_ref[:, g * POOL_GROUP:(g + 1) * POOL_GROUP] = val[0:1]

    _pool_groups(buf_ref, HALO, PAST_LEN, w_pool_ref, scale_ref, store)


def _head_expand():
    dim_head = jnp.arange(ATTN_WIDTH, dtype=jnp.int32) // HEAD_DIM
    return (jnp.arange(LANES, dtype=jnp.int32)[:, None] == dim_head[None, :]).astype(F32)


def _sample_pool(us, state_pool, w_pool_bf16, pool_scale):
    db = us.shape[0]
    pool = pl.pallas_call(
        _sample_pool_kernel,
        grid=(db,),
        in_specs=[
            pl.BlockSpec((None, 1, POOL_WIDTH), lambda b: (b, 0, 0)),
            pl.BlockSpec((None, POOL_STATE, POOL_WIDTH), lambda b: (b, 0, 0)),
            pl.BlockSpec((len(POOL_WINDOWS), POOL_GROUP, POOL_GROUP), lambda b: (0, 0, 0)),
            pl.BlockSpec((1, POOL_WIDTH), lambda b: (0, 0)),
        ],
        out_specs=pl.BlockSpec((None, 1, POOL_WIDTH), lambda b: (b, 0, 0)),
        out_shape=jax.ShapeDtypeStruct((db, 1, POOL_WIDTH), F32),
        scratch_shapes=[pltpu.VMEM((2 * HALO, POOL_WIDTH), F32)],
        compiler_params=_cparams(("arbitrary",)),
        name="sample_pool",
    )(us.reshape(db, 1, POOL_WIDTH), state_pool, w_pool_bf16, pool_scale)
    return pool.reshape(db, POOL_WIDTH)


def _layer_norm(xf, g, b):
    mu = jnp.mean(xf, axis=-1, keepdims=True)
    xc = xf - mu
    var = jnp.mean(xc * xc, axis=-1, keepdims=True)
    return xc * lax.rsqrt(var + LN_EPS) * g + b


def _route(logits):
    lane = lax.broadcasted_iota(jnp.int32, logits.shape, 1)
    lane_f = lane.astype(F32)
    neg = jnp.float32(-jnp.inf)

    def first_argmax(v, vmax):
        return jnp.min(jnp.where(v == vmax, lane_f, jnp.float32(LANES)), axis=-1, keepdims=True)

    gl = jnp.where(lane < N_GROUPS, logits, neg)
    gmax = jnp.max(gl, axis=-1, keepdims=True)
    g_sel = first_argmax(gl, gmax)
    g_gate = 1.0 / jnp.sum(jnp.exp(gl - gmax), axis=-1, keepdims=True)
    lane_group = (((lane + (EXPERTS_PER_GROUP - N_GROUPS)) >> 3) - 1).astype(F32)
    el = jnp.where(lane_group == g_sel, logits, neg)
    v1 = jnp.max(el, axis=-1, keepdims=True)
    i1 = first_argmax(el, v1)
    el2 = jnp.where(lane_f == i1, neg, el)
    v2 = jnp.max(el2, axis=-1, keepdims=True)
    i2 = first_argmax(el2, v2)
    t = jnp.exp(v2 - v1)
    w1 = (1.0 / (1.0 + t)) * g_gate
    w2 = (t / (1.0 + t)) * g_gate
    route = jnp.where(lane == 0, i1 - N_GROUPS, 0.0)
    route = jnp.where(lane == 1, i2 - N_GROUPS, route)
    route = jnp.where(lane == 2, w1, route)
    route = jnp.where(lane == 3, w2, route)
    return route


def _post_kernel(o0_ref, o1_ref, o2_ref, l0_ref, l1_ref, l2_ref, pool_ref, x_ref, hexp_ref,
                 wo_ref, g1_ref, b1_ref, wrh_ref, wrl_ref, br_ref, *rest, n_row_tiles):
    x1_ref, route_ref = rest[-2:]
    i = pl.program_id(0)

    @pl.when(i < n_row_tiles)
    def _():
        l0, l1, l2 = l0_ref[...], l1_ref[...], l2_ref[...]
        lmax = jnp.maximum(jnp.maximum(l0, l1), l2)
        e0, e1, e2 = jnp.exp(l0 - lmax), jnp.exp(l1 - lmax), jnp.exp(l2 - lmax)
        den = e0 + e1 + e2
        hexp = hexp_ref[...]
        attn = jnp.zeros((x_ref.shape[0], ATTN_WIDTH), F32)
        for o_ref, e in ((o0_ref, e0), (o1_ref, e1), (o2_ref, e2)):
            wh, wl = _split_bf16(e / den)
            w_x = (jnp.dot(wh, hexp, preferred_element_type=F32) + jnp.dot(wl, hexp, preferred_element_type=F32))
            o = jnp.concatenate([o_ref[c] for c in range(ATTN_WIDTH // LANES)], axis=1)
            attn = attn + o * w_x
        h = _mm(attn, wo_ref[0:ATTN_WIDTH, :]) + _mm(pool_ref[...], wo_ref[ATTN_WIDTH:, :])
        x1 = _layer_norm(ALPHA * x_ref[...] + h, g1_ref[...], b1_ref[...])
        x1_ref[...] = x1
        xh, xl = _split_bf16(x1)
        wrh = wrh_ref[...]
        logits = (jnp.dot(xh, wrh, preferred_element_type=F32) + jnp.dot(xl, wrh, preferred_element_type=F32)
                  + jnp.dot(xh, wrl_ref[...], preferred_element_type=F32)) + br_ref[...]
        route_ref[...] = _route(logits)

    @pl.when(i >= n_row_tiles)
    def _():
        x1_ref[...] = jnp.zeros(x1_ref.shape, F32)


def _row_spec(tm, width):
    return pl.BlockSpec((tm, width), lambda i: (i, 0))


def _const_spec(shape):
    return pl.BlockSpec(shape, lambda i: (0,) * len(shape), pipeline_mode=pl.Buffered(1))


def _post(os_, lses, pool, x, wo, g1, b1, wrh, wrl, br, *, tm, name, filler_tiles=0, x1_into=None, row0=0):
    m, d = x.shape
    n = m // tm
    hexp = _head_expand().astype(BF16)
    last = n - 1

    def rows(width):
        return pl.BlockSpec((tm, width), lambda i: (jnp.minimum(i, last), 0))

    in_specs = ([pl.BlockSpec((ATTN_WIDTH // LANES, tm, LANES), lambda i: (0, jnp.minimum(i, last), 0))] * 3
                + [rows(LANES)] * 3
                + [rows(POOL_WIDTH), rows(d), _const_spec((LANES, ATTN_WIDTH)),
                   _const_spec((d, d)), _const_spec((1, d)), _const_spec((1, d)),
                   _const_spec((d, LANES)), _const_spec((d, LANES)), _const_spec((1, LANES))])
    operands = [*os_, *lses, pool, x, hexp, wo, g1, b1, wrh, wrl, br]
    aliases = {}
    if x1_into is None:
        x1_shape = jax.ShapeDtypeStruct((m + filler_tiles * tm, d), F32)
    else:
        assert filler_tiles == 0 and row0 % tm == 0
        in_specs.append(pl.BlockSpec(memory_space=pl.ANY))
        operands.append(x1_into)
        aliases = {len(operands) - 1: 0}
        x1_shape = jax.ShapeDtypeStruct(x1_into.shape, F32)
    return pl.pallas_call(
        functools.partial(_post_kernel, n_row_tiles=n),
        grid=(n + filler_tiles,),
        in_specs=in_specs,
        out_specs=[pl.BlockSpec((tm, d), lambda i: (row0 // tm + i, 0)), rows(LANES)],
        out_shape=[x1_shape, jax.ShapeDtypeStruct((m, LANES), F32)],
        input_output_aliases=aliases,
        compiler_params=_cparams(("arbitrary",)),
        name=name,
    )(*operands)


def _routing_plan(pair_expert, n_tiles):
    p = pair_expert.shape[0]
    experts = jnp.arange(N_EXPERTS, dtype=jnp.int32)
    onehot = (pair_expert[:, None] == experts[None, :]).astype(jnp.int32)
    csum = jnp.cumsum(onehot, axis=0)
    rank = jnp.take_along_axis(csum, pair_expert[:, None], axis=1)[:, 0] - 1
    counts = csum[-1]
    tiles_per = (counts + MOE_TILE - 1) // MOE_TILE
    tile_end = jnp.cumsum(tiles_per)
    tile_start = tile_end - tiles_per
    dest = (tile_start[pair_expert] * MOE_TILE + rank).astype(jnp.int32)
    n_used = tile_end[-1]
    tile_ids = jnp.arange(n_tiles, dtype=jnp.int32)
    tile_expert = jnp.sum((tile_ids[:, None] >= tile_end[None, :]).astype(jnp.int32), axis=1)
    last_expert = jnp.sum((n_used - 1 >= tile_end).astype(jnp.int32))
    tile_expert = jnp.where(tile_ids < n_used, tile_expert, last_expert).astype(jnp.int32)
    slot_token = jnp.zeros((n_tiles * MOE_TILE,), jnp.int32).at[dest].set(
        jnp.arange(p, dtype=jnp.int32) // 2, unique_indices=True, mode="promise_in_bounds")
    return dest, slot_token, tile_expert, n_used.reshape(1).astype(jnp.int32)


def _moe_kernel(te_ref, nu_ref, st_ref, x_ref, wg_ref, wu_ref, wd_ref, ys_ref,
                buf_ref, wgb_ref, wub_ref, wdb_ref, sem_ref):
    c = pl.program_id(0)
    n_tiles = pl.num_programs(0)
    n_used = nu_ref[0]

    def issue(tile, slot):
        def body(i, carry):
            t = st_ref[tile * MOE_TILE + i]
            pltpu.make_async_copy(x_ref.at[pl.ds(t, 1), :], buf_ref.at[slot, pl.ds(i, 1), :],
                                  sem_ref.at[slot]).start()
            return carry

        lax.fori_loop(0, MOE_TILE, body, 0, unroll=GATHER_UNROLL)

    def wait(slot):
        pltpu.make_async_copy(x_ref.at[pl.ds(0, MOE_TILE), :], buf_ref.at[slot], sem_ref.at[slot]).wait()

    slot = c % 2

    @pl.when((c == 0) & (n_used > 0))
    def _():
        issue(0, 0)

    @pl.when((c + 1 < n_tiles) & (c + 1 < n_used))
    def _():
        issue(c + 1, 1 - slot)

    prev_expert = te_ref[jnp.maximum(c - 1, 0)]

    @pl.when((c == 0) | (te_ref[c] != prev_expert))
    def _():
        wgb_ref[...] = wg_ref[...].astype(BF16)
        wub_ref[...] = wu_ref[...].astype(BF16)
        wdb_ref[...] = wd_ref[...].astype(BF16)

    @pl.when(c < n_used)
    def _():
        wait(slot)
        x = buf_ref[slot].astype(BF16)
        gate = jnp.dot(x, wgb_ref[...], preferred_element_type=F32)
        up = jnp.dot(x, wub_ref[...], preferred_element_type=F32)
        h = jax.nn.silu(gate) * up
        ys_ref[...] = jnp.dot(h.astype(BF16), wdb_ref[...], preferred_element_type=F32)

    @pl.when(c >= n_used)
    def _():
        ys_ref[...] = jnp.zeros(ys_ref.shape, ys_ref.dtype)


def _moe(x1, slot_token, tile_expert, n_used, w_gate, w_up, w_down, n_tiles):
    d = x1.shape[1]
    f = w_gate.shape[-1]
    grid_spec = pltpu.PrefetchScalarGridSpec(
        num_scalar_prefetch=3,
        grid=(n_tiles,),
        in_specs=[
            pl.BlockSpec(memory_space=pl.ANY),
            pl.BlockSpec((None, d, f), lambda c, te, nu, st: (te[c], 0, 0)),
            pl.BlockSpec((None, d, f), lambda c, te, nu, st: (te[c], 0, 0)),
            pl.BlockSpec((None, f, d), lambda c, te, nu, st: (te[c], 0, 0)),
        ],
        out_specs=pl.BlockSpec((MOE_TILE, d), lambda c, te, nu, st: (c, 0)),
        scratch_shapes=[
            pltpu.VMEM((2, MOE_TILE, d), F32),
            pltpu.VMEM((d, f), BF16),
            pltpu.VMEM((d, f), BF16),
            pltpu.VMEM((f, d), BF16),
            pltpu.SemaphoreType.DMA((2,)),
        ],
    )
    return pl.pallas_call(
        _moe_kernel,
        grid_spec=grid_spec,
        out_shape=jax.ShapeDtypeStruct((n_tiles * MOE_TILE, d), F32),
        compiler_params=_cparams(("arbitrary",)),
        name="moe",
    )(tile_expert, n_used, slot_token, x1, w_gate, w_up, w_down)


def _final_kernel(dest_ref, ys_ref, x1_ref, route_ref, p_ref, g2_ref, b2_ref, wpg_ref, wp_ref, o_ref,
                  buf_ref, sem_ref, *, tm, pair0):
    i = pl.program_id(0)
    n_steps = pl.num_programs(0)

    def issue(step, slot):
        def body(r, carry):
            for k in range(2):
                dslot = dest_ref[pair0 + (step * tm + r) * 2 + k]
                pltpu.make_async_copy(ys_ref.at[pl.ds(dslot, 1), :], buf_ref.at[slot, k, pl.ds(r, 1), :],
                                      sem_ref.at[slot]).start()
            return carry

        lax.fori_loop(0, tm, body, 0, unroll=GATHER_UNROLL)

    def wait(slot):
        for k in range(2):
            pltpu.make_async_copy(ys_ref.at[pl.ds(0, tm), :], buf_ref.at[slot, k], sem_ref.at[slot]).wait()

    slot = i % 2

    @pl.when(i == 0)
    def _():
        issue(0, 0)

    @pl.when(i + 1 < n_steps)
    def _():
        issue(i + 1, 1 - slot)

    wait(slot)
    route = route_ref[...]
    y = route[:, 2:3] * buf_ref[slot, 0] + route[:, 3:4] * buf_ref[slot, 1]
    x2 = _layer_norm(ALPHA * x1_ref[...] + y, g2_ref[...], b2_ref[...])
    gate = jax.nn.sigmoid(jnp.dot(x2.astype(BF16), wpg_ref[...], preferred_element_type=F32))
    ple = jnp.dot(p_ref[...].astype(BF16), wp_ref[...], preferred_element_type=F32)
    o_ref[...] = x2 + gate * ple


def _final(dest, ys, x1, route, p, g2, b2, wpg, wp, *, tm, row0):
    m, pd = p.shape
    d = x1.shape[1]
    assert m % tm == 0 and row0 % tm == 0
    pair0 = 2 * row0
    grid_spec = pltpu.PrefetchScalarGridSpec(
        num_scalar_prefetch=1,
        grid=(m // tm,),
        in_specs=[
            pl.BlockSpec(memory_space=pl.ANY),
            pl.BlockSpec((tm, d), lambda i, dst: (row0 // tm + i, 0)),
            pl.BlockSpec((tm, LANES), lambda i, dst: (i, 0)),
            pl.BlockSpec((tm, pd), lambda i, dst: (i, 0)),
            pl.BlockSpec((1, d), lambda i, dst: (0, 0)),
            pl.BlockSpec((1, d), lambda i, dst: (0, 0)),
            pl.BlockSpec((d, d), lambda i, dst: (0, 0)),
            pl.BlockSpec((pd, d), lambda i, dst: (0, 0)),
        ],
        out_specs=pl.BlockSpec((tm, d), lambda i, dst: (i, 0)),
        scratch_shapes=[pltpu.VMEM((2, 2, tm, d), F32), pltpu.SemaphoreType.DMA((2,))],
    )
    return pl.pallas_call(
        functools.partial(_final_kernel, tm=tm, pair0=pair0),
        grid_spec=grid_spec,
        out_shape=jax.ShapeDtypeStruct((m, d), F32),
        compiler_params=_cparams(("arbitrary",)),
        name="final",
    )(dest, ys, x1, route, p, g2, b2, wpg, wp)


def _pick_tile(m, pref):
    t = pref
    while m % t:
        t //= 2
    return t


def _layer(xp, xs, caches, state_pool, pp, ps, w):
    s, d = xp.shape
    db = xs.shape[0]

    cos_p, sin_p = _rope_tables(jnp.arange(s, dtype=jnp.int32))
    dils = tuple(dil for _, dil in DILATED_PATTERNS)
    qkv_p, u_p = _in_proj(xp, w["w_in"], cos_p, sin_p, tm=_pick_tile(s, 512), dils=dils, qkv_dtype=BF16)
    os_, lses = [], []
    for pi, (window, dil) in enumerate(DILATED_PATTERNS):
        o, l = _attn_pattern(qkv_p[pi], pi, dil, window)
        os_.append(o)
        lses.append(l)
    pool_p = _pool_prompt(u_p, w["w_pool"], w["pool_scale"], tm=_pick_tile(s, 512))

    cos_s, sin_s = _rope_tables(jnp.full((db,), PAST_LEN, jnp.int32))
    (qkv_s,), u_s = _in_proj(xs, w["w_in_f32"], cos_s, sin_s, tm=db, dils=None, qkv_dtype=F32)
    zt_s = qkv_s.T
    os_s, lses_s, kv_s = [], [], []
    for pi, ((window, dil), c) in enumerate(zip(DILATED_PATTERNS, caches)):
        cn, ot, lt = _sample_attn(zt_s, jnp.transpose(c, (0, 2, 3, 4, 1)), pi, dil)
        kv_s.append(jnp.transpose(cn, (0, 4, 1, 2, 3)))
        os_s.append(jnp.transpose(ot.T[:db].reshape(db, ATTN_WIDTH // LANES, LANES), (1, 0, 2)))
        lses_s.append(lt.T[:db])
    pool_s = _sample_pool(u_s, state_pool, w["w_pool_f32"], w["pool_scale"])

    wr_hi, wr_lo = _split_bf16(w["w_router"])
    post_w = (w["ln1_g"], w["ln1_b"], wr_hi, wr_lo, w["b_router"])
    tm_p = _pick_tile(s, 256)
    assert s % db == 0 and tm_p >= db
    x1, route_p = _post(os_, lses, pool_p, xp, w["w_out"], *post_w, tm=tm_p, name="post_prompt",
                        filler_tiles=1)
    x1, route_s = _post(os_s, lses_s, pool_s, xs, w["w_out_f32"], *post_w, tm=db, name="post_sample",
                        x1_into=x1, row0=s)

    pair_expert = jnp.concatenate([route_p[:, 0:2].reshape(-1), route_s[:, 0:2].reshape(-1)]).astype(jnp.int32)
    n_pairs = pair_expert.shape[0]
    n_tiles = -(-n_pairs // MOE_TILE) + N_EXPERTS
    dest, slot_token, tile_expert, n_used = _routing_plan(pair_expert, n_tiles)
    ys = _moe(x1, slot_token, tile_expert, n_used, w["w_gate"], w["w_up"], w["w_down"], n_tiles)

    fin_w = (w["ln2_g"], w["ln2_b"], w["w_ple_gate"], w["w_ple"])
    y_p = _final(dest, ys, x1, route_p, pp, *fin_w, tm=tm_p, row0=0)
    y_s = _final(dest, ys, x1, route_s, ps, *fin_w, tm=db, row0=s)

    kv_p = []
    for pi, (window, dil) in enumerate(DILATED_PATTERNS):
        keep = min(window, s)
        kv = qkv_p[pi][:, (s - keep) // dil:, ATTN_WIDTH:3 * ATTN_WIDTH].astype(F32)
        kv = jnp.transpose(kv, (1, 0, 2))
        kv_p.append(kv.reshape(1, keep, 2, N_HEADS, HEAD_DIM))
    pool_state_p = u_p[s - POOL_STATE:][None]
    pool_state_s = jnp.concatenate([state_pool[:, 1:], u_s[:, None, :]], axis=1)
    return y_p, y_s, kv_p, pool_state_p, kv_s, pool_state_s


def kernel(x_prompt, x_sample, cache_kv_w128_d1, cache_kv_w512_d4, cache_kv_w2048_d16, state_pool, p_prompt, p_sample, w_in, w_out, w_pool, pool_scale, ln1_g, ln1_b, w_group_router, b_group_router, w_expert_router, b_expert_router, w_gate, w_up, w_down, ln2_g, ln2_b, w_ple, w_ple_gate):
    assert w_in.shape[0] == DEPTH == 1 and x_prompt.shape[0] == 1 and x_sample.shape[1] == 1
    d = x_prompt.shape[-1]
    pad = LANES - N_GROUPS - N_EXPERTS
    w = {
        "w_in": w_in[0].astype(BF16), "w_in_f32": w_in[0],
        "w_out": w_out[0].astype(BF16), "w_out_f32": w_out[0],
        "w_pool": w_pool[0].astype(BF16), "w_pool_f32": w_pool[0],
        "pool_scale": pool_scale[0].reshape(1, POOL_WIDTH),
        "ln1_g": ln1_g[0].reshape(1, d), "ln1_b": ln1_b[0].reshape(1, d),
        "ln2_g": ln2_g[0].reshape(1, d), "ln2_b": ln2_b[0].reshape(1, d),
        "w_router": jnp.concatenate([w_group_router[0], w_expert_router[0], jnp.zeros((d, pad), F32)], axis=1),
        "b_router": jnp.concatenate([b_group_router[0], b_expert_router[0], jnp.zeros((pad,), F32)]).reshape(1, LANES),
        "w_gate": w_gate[0], "w_up": w_up[0], "w_down": w_down[0],
        "w_ple": w_ple[0].astype(BF16),
        "w_ple_gate": w_ple_gate[0].astype(BF16),
    }
    caches = [cache_kv_w128_d1[0], cache_kv_w512_d4[0], cache_kv_w2048_d16[0]]
    y_p, y_s, kv_p, pool_p, kv_s, pool_s = _layer(
        x_prompt[0], x_sample[:, 0], caches, state_pool[0], p_prompt[0, 0], p_sample[0, :, 0], w)
    return (y_p[None], y_s[:, None], kv_p[0][None], kv_p[1][None], kv_p[2][None], pool_p[None],
            kv_s[0][None], kv_s[1][None], kv_s[2][None], pool_s[None])
```

```python
import functools

import jax
import jax.numpy as jnp
from jax import lax
from jax.experimental import pallas as pl
from jax.experimental.pallas import tpu as pltpu

F32 = jnp.float32
BF16 = jnp.bfloat16

PAST_LEN = 8192
HEAD_DIM = 64
N_HEADS = 16
ATTN_WIDTH = N_HEADS * HEAD_DIM
DILATED_PATTERNS = ((128, 1), (512, 4), (2048, 16))
N_PATTERNS = len(DILATED_PATTERNS)
BAND = 128
POOL_WINDOWS = (2, 4, 8, 16)
POOL_GROUP = 256
POOL_WIDTH = POOL_GROUP * len(POOL_WINDOWS)
POOL_STATE = max(POOL_WINDOWS) - 1
ROT_DIM = HEAD_DIM // 4
ROPE_THETA = 500000.0
QKV_WIDTH = N_PATTERNS * 3 * ATTN_WIDTH
N_GROUPS = 4
EXPERTS_PER_GROUP = 8
N_EXPERTS = N_GROUPS * EXPERTS_PER_GROUP
DEPTH = 1
ALPHA = (2.0 * DEPTH) ** 0.25
LN_EPS = 1e-5
NEG_INF = -1e30

LANES = 128
SUBLANES = 8
VMEM_LIMIT = 48 * 1024 * 1024

MOE_TILE = 256
HALO = 16
GATHER_UNROLL = 8


def _cparams(semantics):
    return pltpu.CompilerParams(dimension_semantics=semantics, vmem_limit_bytes=VMEM_LIMIT)


def _rope_tables(pos):
    inv_freq = ROPE_THETA ** (-jnp.arange(0, ROT_DIM, 2, dtype=F32) / ROT_DIM)
    ang = pos.astype(F32)[:, None] * inv_freq[None, :]
    cos, sin = jnp.cos(ang), jnp.sin(ang)
    t = pos.shape[0]
    rest = HEAD_DIM - ROT_DIM
    c64 = jnp.concatenate([cos, cos, jnp.ones((t, rest), F32)], -1)
    s64 = jnp.concatenate([-sin, sin, jnp.zeros((t, rest), F32)], -1)
    return jnp.tile(c64, (1, LANES // HEAD_DIM)), jnp.tile(s64, (1, LANES // HEAD_DIM))


def _rope_chunk(xc, cos, sin):
    half = ROT_DIM // 2
    lane = lax.broadcasted_iota(jnp.int32, xc.shape, 1) & (HEAD_DIM - 1)
    upper = pltpu.roll(xc, LANES - half, 1)
    lower = pltpu.roll(xc, half, 1)
    partner = jnp.where(lane < half, upper, lower)
    return xc * cos + partner * sin


def _split_bf16(a):
    hi = a.astype(BF16)
    return hi, (a - hi.astype(F32)).astype(BF16)


def _mm(a, w):
    if w.dtype == BF16:
        return jnp.dot(a.astype(BF16), w, preferred_element_type=F32)
    ah, al = _split_bf16(a.astype(F32))
    wh, wl = _split_bf16(w)
    return (jnp.dot(ah, wh, preferred_element_type=F32) + jnp.dot(al, wh, preferred_element_type=F32)
            + jnp.dot(ah, wl, preferred_element_type=F32))


ROW_PAD = 1


def _row_pitch(d):
    return d // LANES + ROW_PAD


def _store_row_major(ref, val):
    n, d = val.shape
    nc, pitch = d // LANES, _row_pitch(d)
    for c in range(nc):
        ref[pl.ds(c, n, stride=pitch), :] = val[:, c * LANES:(c + 1) * LANES]
    for c in range(nc, pitch):
        ref[pl.ds(c, n, stride=pitch), :] = jnp.zeros((n, LANES), F32)


def _load_row_major(ref, n):
    pitch = ref.shape[0] // n
    return jnp.concatenate([ref[pl.ds(c, n, stride=pitch), :] for c in range(pitch - ROW_PAD)], axis=1)


ROLE_TILES = 3


def _in_proj_kernel(x_ref, w_ref, cos_ref, sin_ref, *refs, dils):
    n_out = 1 if dils is None else len(dils)
    qkv_refs, u_ref, xb_ref, stage_ref = refs[:n_out], refs[n_out], refs[n_out + 1], refs[n_out + 2]
    j = pl.program_id(1)
    n_chunks, tm, _ = stage_ref.shape
    n_qkv_tiles = N_PATTERNS * ROLE_TILES

    def chunk(c):
        return slice(c * LANES, (c + 1) * LANES)

    if w_ref.dtype == BF16:
        @pl.when(j == 0)
        def _():
            xb_ref[...] = x_ref[...].astype(BF16)

        acc = jnp.dot(xb_ref[...], w_ref[...], preferred_element_type=F32)
    else:
        acc = _mm(x_ref[...], w_ref[...])
    role = j % ROLE_TILES
    is_qkv = j < n_qkv_tiles

    @pl.when(is_qkv & (role < 2))
    def _():
        cos = cos_ref[...]
        sin = sin_ref[...]
        for c in range(n_chunks):
            stage_ref[c] = _rope_chunk(acc[:, chunk(c)], cos, sin)

    @pl.when(is_qkv & (role == 2))
    def _():
        for c in range(n_chunks):
            stage_ref[c] = acc[:, chunk(c)]

    if dils is None:
        @pl.when(is_qkv)
        def _():
            for c in range(n_chunks):
                qkv_refs[0][:, chunk(c)] = stage_ref[c].astype(qkv_refs[0].dtype)
    else:
        for pi, dil in enumerate(dils):
            @pl.when(is_qkv & (j // ROLE_TILES == pi))
            def _(pi=pi, dil=dil):
                for r in range(dil):
                    for c in range(n_chunks):
                        rows = stage_ref[c, pl.ds(r, tm // dil, stride=dil), :] if dil > 1 else stage_ref[c]
                        qkv_refs[pi][r, :, chunk(c)] = rows.astype(qkv_refs[pi].dtype)

    @pl.when(j >= n_qkv_tiles)
    def _():
        u_ref[...] = acc


def _in_proj(x, w, cos_t, sin_t, *, tm, dils, qkv_dtype):
    m, d = x.shape
    tn = ATTN_WIDTH
    assert m % tm == 0 and w.shape[1] == QKV_WIDTH + POOL_WIDTH and POOL_WIDTH == tn
    n_qkv_tiles = N_PATTERNS * ROLE_TILES
    if dils is None:
        qkv_specs = [pl.BlockSpec((tm, tn), lambda i, j: (i, jnp.minimum(j, n_qkv_tiles - 1)))]
        qkv_shapes = [jax.ShapeDtypeStruct((m, QKV_WIDTH), qkv_dtype)]
    else:
        qkv_specs, qkv_shapes = [], []
        for pi, dil in enumerate(dils):
            assert tm % (dil * 2 * SUBLANES) == 0
            qkv_specs.append(pl.BlockSpec(
                (dil, tm // dil, tn),
                lambda i, j, pi=pi: (0, i, jnp.clip(j - pi * ROLE_TILES, 0, ROLE_TILES - 1))))
            qkv_shapes.append(jax.ShapeDtypeStruct((dil, m // dil, ROLE_TILES * tn), qkv_dtype))
    outs = pl.pallas_call(
        functools.partial(_in_proj_kernel, dils=dils),
        grid=(m // tm, n_qkv_tiles + 1),
        in_specs=[
            pl.BlockSpec((tm, d), lambda i, j: (i, 0)),
            pl.BlockSpec((d, tn), lambda i, j: (0, j)),
            pl.BlockSpec((tm, LANES), lambda i, j: (i, 0)),
            pl.BlockSpec((tm, LANES), lambda i, j: (i, 0)),
        ],
        out_specs=qkv_specs + [pl.BlockSpec((tm, tn), lambda i, j: (i, 0))],
        out_shape=qkv_shapes + [jax.ShapeDtypeStruct((m, POOL_WIDTH), F32)],
        scratch_shapes=[pltpu.VMEM((tm, d), BF16), pltpu.VMEM((tn // LANES, tm, LANES), F32)],
        compiler_params=_cparams(("arbitrary", "arbitrary")),
        name="in_proj",
    )(x, w, cos_t, sin_t)
    return outs[:-1], outs[-1]


def _attn_kernel(q_ref, kp_ref, kc_ref, vp_ref, vc_ref, o_ref, lse_ref,
                 lim_ref, s_ref, e_ref, m_ref, den_ref, *, r_max, dil):
    n = pl.program_id(0)
    r = pl.program_id(1)
    qi = lax.broadcasted_iota(jnp.int32, (BAND, 2 * BAND), 0)
    ki = lax.broadcasted_iota(jnp.int32, (BAND, 2 * BAND), 1)
    dist = BAND + qi - ki
    valid = (dist >= 0) & (dist <= r_max) & ((n > 0) | (ki >= BAND))
    lim_ref[...] = jnp.where(valid, jnp.float32(jnp.finfo(F32).max), jnp.float32(NEG_INF))

    lane = lax.broadcasted_iota(jnp.int32, (BAND, LANES), 1)
    heads_per_chunk = LANES // HEAD_DIM
    n_chunks = ATTN_WIDTH // LANES
    lane_head = lane >> (HEAD_DIM.bit_length() - 1)
    qscale = [jnp.where(lane_head == hh, HEAD_DIM ** -0.5, 0.0).astype(BF16) for hh in range(heads_per_chunk)]
    nt = (((1,), (1,)), ((), ()))

    for c in range(n_chunks):
        sl = slice(c * LANES, (c + 1) * LANES)
        q2 = q_ref[:, sl]
        kp, kc = kp_ref[:, sl], kc_ref[:, sl]
        for hh in range(heads_per_chunk):
            h = c * heads_per_chunk + hh
            qm = q2 * qscale[hh]
            s = jnp.concatenate(
                [lax.dot_general(qm, kp, nt, preferred_element_type=F32),
                 lax.dot_general(qm, kc, nt, preferred_element_type=F32)], axis=1)
            s = jnp.minimum(s, lim_ref[...])
            s_ref[h] = s
            m_ref[h] = jnp.broadcast_to(jnp.max(s, axis=-1, keepdims=True), (BAND, LANES))

    lse_acc = jnp.zeros((BAND, LANES), F32)
    for h in range(N_HEADS):
        m = m_ref[h]
        e = jnp.exp(s_ref[h] - jnp.concatenate([m, m], axis=1))
        e_ref[h] = e.astype(BF16)
        den = jnp.broadcast_to(jnp.sum(e, axis=-1, keepdims=True), (BAND, LANES))
        den_ref[h] = den
        lse_acc = jnp.where(lane == h, m + jnp.log(den), lse_acc)

    rows = slice(None) if dil == 1 else pl.ds(r, BAND, stride=dil)
    lse_ref[rows, :] = lse_acc
    for c in range(n_chunks):
        sl = slice(c * LANES, (c + 1) * LANES)
        vp, vc = vp_ref[:, sl], vc_ref[:, sl]
        o2 = jnp.zeros((BAND, LANES), F32)
        for hh in range(heads_per_chunk):
            h = c * heads_per_chunk + hh
            pv = (jnp.dot(e_ref[h, :, :BAND], vp, preferred_element_type=F32)
                  + jnp.dot(e_ref[h, :, BAND:], vc, preferred_element_type=F32))
            o2 = jnp.where(lane_head == hh, pv / den_ref[h], o2)
        o_ref[c, rows, :] = o2


def _attn_pattern(qkv, pi, dil, window):
    s = qkv.shape[0] * qkv.shape[1]
    assert qkv.shape[0] == dil and s % (dil * BAND) == 0
    nb = s // (dil * BAND)
    r_max = window // dil

    def cur(role):
        return lambda n, r: (r, n, role)

    def prev(role):
        return lambda n, r: (r, jnp.maximum(n - 1, 0), role)

    blk = (None, BAND, ATTN_WIDTH)
    span = BAND * dil
    return pl.pallas_call(
        functools.partial(_attn_kernel, r_max=r_max, dil=dil),
        grid=(nb, dil),
        in_specs=[
            pl.BlockSpec(blk, cur(0)),
            pl.BlockSpec(blk, prev(1)),
            pl.BlockSpec(blk, cur(1)),
            pl.BlockSpec(blk, prev(2)),
            pl.BlockSpec(blk, cur(2)),
        ],
        out_specs=[
            pl.BlockSpec((ATTN_WIDTH // LANES, span, LANES), lambda n, r: (0, n, 0)),
            pl.BlockSpec((span, LANES), lambda n, r: (n, 0)),
        ],
        out_shape=[
            jax.ShapeDtypeStruct((ATTN_WIDTH // LANES, s, LANES), F32),
            jax.ShapeDtypeStruct((s, LANES), F32),
        ],
        scratch_shapes=[
            pltpu.VMEM((BAND, 2 * BAND), F32),
            pltpu.VMEM((N_HEADS, BAND, 2 * BAND), F32),
            pltpu.VMEM((N_HEADS, BAND, 2 * BAND), BF16),
            pltpu.VMEM((N_HEADS, BAND, LANES), F32),
            pltpu.VMEM((N_HEADS, BAND, LANES), F32),
        ],
        compiler_params=_cparams(("arbitrary", "arbitrary")),
        name=f"attn_p{pi}",
    )(qkv, qkv, qkv, qkv, qkv)


def _pool_groups(buf_ref, tm, pos0, w_pool_ref, scale_ref, store):
    row = lax.broadcasted_iota(jnp.int32, (tm, 1), 0) + pos0
    for g, win in enumerate(POOL_WINDOWS):
        sl = slice(g * POOL_GROUP, (g + 1) * POOL_GROUP)
        cur = buf_ref[HALO:HALO + tm, sl]
        acc = cur
        for jj in range(1, win):
            acc = acc + buf_ref[HALO - jj:HALO - jj + tm, sl]
        cnt = jnp.minimum(row + 1, win).astype(F32)
        d = acc / cnt - cur
        y = _mm(d, w_pool_ref[g])
        store(g, y * scale_ref[:, sl])


def _pool_kernel(u_ref, uprev_ref, w_pool_ref, scale_ref, o_ref, buf_ref, *, tm):
    i = pl.program_id(0)
    buf_ref[0:HALO, :] = jnp.where(i > 0, uprev_ref[...], jnp.zeros_like(uprev_ref))
    buf_ref[HALO:HALO + tm, :] = u_ref[...]

    def store(g, val):
        o_ref[:, g * POOL_GROUP:(g + 1) * POOL_GROUP] = val.astype(o_ref.dtype)

    _pool_groups(buf_ref, tm, i * tm, w_pool_ref, scale_ref, store)


def _pool_prompt(u, w_pool_bf16, pool_scale, *, tm):
    s = u.shape[0]
    assert s % tm == 0 and tm % HALO == 0
    per = tm // HALO
    return pl.pallas_call(
        functools.partial(_pool_kernel, tm=tm),
        grid=(s // tm,),
        in_specs=[
            pl.BlockSpec((tm, POOL_WIDTH), lambda i: (i, 0)),
            pl.BlockSpec((HALO, POOL_WIDTH), lambda i: (jnp.maximum(i * per - 1, 0), 0)),
            pl.BlockSpec((len(POOL_WINDOWS), POOL_GROUP, POOL_GROUP), lambda i: (0, 0, 0)),
            pl.BlockSpec((1, POOL_WIDTH), lambda i: (0, 0)),
        ],
        out_specs=pl.BlockSpec((tm, POOL_WIDTH), lambda i: (i, 0)),
        out_shape=jax.ShapeDtypeStruct((s, POOL_WIDTH), BF16),
        scratch_shapes=[pltpu.VMEM((HALO + tm, POOL_WIDTH), F32)],
        compiler_params=_cparams(("arbitrary",)),
        name="pool",
    )(u, u, w_pool_bf16, pool_scale)


def _sample_attn_kernel(zt_ref, c_ref, cn_ref, ot_ref, lt_ref, *, dil, hps, length):
    b = pl.program_id(0)
    g = pl.program_id(1)

    @pl.when((b == 0) & (g == 0))
    def _():
        ot_ref[...] = jnp.zeros(ot_ref.shape, F32)
        lt_ref[...] = jnp.zeros(lt_ref.shape, F32)

    rows = hps * HEAD_DIM
    base = pl.multiple_of(g * rows, rows)
    lane_z = lax.broadcasted_iota(jnp.int32, (rows, zt_ref.shape[1]), 1)

    def column(role):
        blk = zt_ref[pl.ds(role * ATTN_WIDTH + base, rows), :]
        return jnp.sum(jnp.where(lane_z == b, blk, 0.0), axis=-1, keepdims=True)

    qc, knc, vnc = column(0), column(1), column(2)
    scale = HEAD_DIM ** -0.5
    lane = lax.broadcasted_iota(jnp.int32, (1, length), 1)
    key_ok = (lane & (dil - 1)) == 0
    last = lane == length - 1
    lane_o = lax.broadcasted_iota(jnp.int32, (HEAD_DIM, LANES), 1)
    lane_l = lax.broadcasted_iota(jnp.int32, (1, LANES), 1)
    for h in range(hps):
        sl = slice(h * HEAD_DIM, (h + 1) * HEAD_DIM)
        q, kn, vn = qc[sl], knc[sl], vnc[sl]
        k = c_ref[0, h]
        v = c_ref[1, h]
        s = jnp.where(key_ok, jnp.sum(k * q, axis=0, keepdims=True) * scale, NEG_INF)
        s_n = jnp.sum(kn * q, axis=0, keepdims=True) * scale
        m = jnp.maximum(jnp.max(s, axis=-1, keepdims=True), s_n)
        e = jnp.exp(s - m)
        e_n = jnp.exp(s_n - m)
        den = jnp.sum(e, axis=-1, keepdims=True) + e_n
        o = (jnp.sum(v * e, axis=-1, keepdims=True) + vn * e_n) / den
        lse = m + jnp.log(den)
        cn_ref[0, h] = jnp.where(last, kn, pltpu.roll(k, length - 1, 1))
        cn_ref[1, h] = jnp.where(last, vn, pltpu.roll(v, length - 1, 1))
        r0 = pl.multiple_of(base + h * HEAD_DIM, HEAD_DIM)
        ot_ref[pl.ds(r0, HEAD_DIM), :] = jnp.where(lane_o == b, o, ot_ref[pl.ds(r0, HEAD_DIM), :])
        head = g * hps + h
        lt_ref[pl.ds(head, 1), :] = jnp.where(lane_l == b, lse, lt_ref[pl.ds(head, 1), :])


def _sample_attn(zt, c_t, pi, dil):
    db, _, _, _, length = c_t.shape
    assert db <= LANES and length == dil * BAND
    hps = min(N_HEADS, max(1, (4 * 2048) // length))
    blk = (None, 2, hps, HEAD_DIM, length)
    return pl.pallas_call(
        functools.partial(_sample_attn_kernel, dil=dil, hps=hps, length=length),
        grid=(db, N_HEADS // hps),
        in_specs=[
            pl.BlockSpec((3 * ATTN_WIDTH, db), lambda b, g: (pi, 0)),
            pl.BlockSpec(blk, lambda b, g: (b, 0, g, 0, 0)),
        ],
        out_specs=[
            pl.BlockSpec(blk, lambda b, g: (b, 0, g, 0, 0)),
            pl.BlockSpec((ATTN_WIDTH, LANES), lambda b, g: (0, 0)),
            pl.BlockSpec((LANES, LANES), lambda b, g: (0, 0)),
        ],
        out_shape=[
            jax.ShapeDtypeStruct(c_t.shape, c_t.dtype),
            jax.ShapeDtypeStruct((ATTN_WIDTH, LANES), F32),
            jax.ShapeDtypeStruct((LANES, LANES), F32),
        ],
        compiler_params=_cparams(("arbitrary", "arbitrary")),
        name=f"sample_attn_p{pi}",
    )(zt, c_t)


def _sample_pool_kernel(u_ref, sp_ref, w_pool_ref, scale_ref, pool_ref, buf_ref):
    buf_ref[...] = jnp.zeros(buf_ref.shape, F32)
    buf_ref[HALO - POOL_STATE:HALO, :] = sp_ref[...]
    buf_ref[HALO:HALO + 1, :] = u_ref[...]

    def store(g, val):
        pool_ref[:, g * POOL_GROUP:(g + 1) * POOL_GROUP] = val[0:1]

    _pool_groups(buf_ref, HALO, PAST_LEN, w_pool_ref, scale_ref, store)


def _head_expand():
    dim_head = jnp.arange(ATTN_WIDTH, dtype=jnp.int32) // HEAD_DIM
    return (jnp.arange(LANES, dtype=jnp.int32)[:, None] == dim_head[None, :]).astype(F32)


def _sample_pool(us, state_pool, w_pool_bf16, pool_scale):
    db = us.shape[0]
    pool = pl.pallas_call(
        _sample_pool_kernel,
        grid=(db,),
        in_specs=[
            pl.BlockSpec((None, 1, POOL_WIDTH), lambda b: (b, 0, 0)),
            pl.BlockSpec((None, POOL_STATE, POOL_WIDTH), lambda b: (b, 0, 0)),
            pl.BlockSpec((len(POOL_WINDOWS), POOL_GROUP, POOL_GROUP), lambda b: (0, 0, 0)),
            pl.BlockSpec((1, POOL_WIDTH), lambda b: (0, 0)),
        ],
        out_specs=pl.BlockSpec((None, 1, POOL_WIDTH), lambda b: (b, 0, 0)),
        out_shape=jax.ShapeDtypeStruct((db, 1, POOL_WIDTH), F32),
        scratch_shapes=[pltpu.VMEM((2 * HALO, POOL_WIDTH), F32)],
        compiler_params=_cparams(("arbitrary",)),
        name="sample_pool",
    )(us.reshape(db, 1, POOL_WIDTH), state_pool, w_pool_bf16, pool_scale)
    return pool.reshape(db, POOL_WIDTH)


def _layer_norm(xf, g, b):
    mu = jnp.mean(xf, axis=-1, keepdims=True)
    xc = xf - mu
    var = jnp.mean(xc * xc, axis=-1, keepdims=True)
    return xc * lax.rsqrt(var + LN_EPS) * g + b


def _route(logits):
    lane = lax.broadcasted_iota(jnp.int32, logits.shape, 1)
    lane_f = lane.astype(F32)
    neg = jnp.float32(-jnp.inf)

    def first_argmax(v, vmax):
        return jnp.min(jnp.where(v == vmax, lane_f, jnp.float32(LANES)), axis=-1, keepdims=True)

    gl = jnp.where(lane < N_GROUPS, logits, neg)
    gmax = jnp.max(gl, axis=-1, keepdims=True)
    g_sel = first_argmax(gl, gmax)
    g_gate = 1.0 / jnp.sum(jnp.exp(gl - gmax), axis=-1, keepdims=True)
    lane_group = (((lane + (EXPERTS_PER_GROUP - N_GROUPS)) >> 3) - 1).astype(F32)
    el = jnp.where(lane_group == g_sel, logits, neg)
    v1 = jnp.max(el, axis=-1, keepdims=True)
    i1 = first_argmax(el, v1)
    el2 = jnp.where(lane_f == i1, neg, el)
    v2 = jnp.max(el2, axis=-1, keepdims=True)
    i2 = first_argmax(el2, v2)
    t = jnp.exp(v2 - v1)
    w1 = (1.0 / (1.0 + t)) * g_gate
    w2 = (t / (1.0 + t)) * g_gate
    route = jnp.where(lane == 0, i1 - N_GROUPS, 0.0)
    route = jnp.where(lane == 1, i2 - N_GROUPS, route)
    route = jnp.where(lane == 2, w1, route)
    route = jnp.where(lane == 3, w2, route)
    return route


def _post_kernel(o0_ref, o1_ref, o2_ref, l0_ref, l1_ref, l2_ref, pool_ref, x_ref, hexp_ref,
                 wo_ref, g1_ref, b1_ref, wrh_ref, wrl_ref, br_ref, *rest, n_row_tiles):
    x1_ref, route_ref = rest[-2:]
    i = pl.program_id(0)

    @pl.when(i < n_row_tiles)
    def _():
        l0, l1, l2 = l0_ref[...], l1_ref[...], l2_ref[...]
        lmax = jnp.maximum(jnp.maximum(l0, l1), l2)
        e0, e1, e2 = jnp.exp(l0 - lmax), jnp.exp(l1 - lmax), jnp.exp(l2 - lmax)
        den = e0 + e1 + e2
        hexp = hexp_ref[...]
        attn = jnp.zeros((x_ref.shape[0], ATTN_WIDTH), F32)
        for o_ref, e in ((o0_ref, e0), (o1_ref, e1), (o2_ref, e2)):
            wh, wl = _split_bf16(e / den)
            w_x = (jnp.dot(wh, hexp, preferred_element_type=F32) + jnp.dot(wl, hexp, preferred_element_type=F32))
            o = jnp.concatenate([o_ref[c] for c in range(ATTN_WIDTH // LANES)], axis=1)
            attn = attn + o * w_x
        h = _mm(attn, wo_ref[0:ATTN_WIDTH, :]) + _mm(pool_ref[...], wo_ref[ATTN_WIDTH:, :])
        x1 = _layer_norm(ALPHA * x_ref[...] + h, g1_ref[...], b1_ref[...])
        _store_row_major(x1_ref, x1)
        xh, xl = _split_bf16(x1)
        wrh = wrh_ref[...]
        logits = (jnp.dot(xh, wrh, preferred_element_type=F32) + jnp.dot(xl, wrh, preferred_element_type=F32)
                  + jnp.dot(xh, wrl_ref[...], preferred_element_type=F32)) + br_ref[...]
        route_ref[...] = _route(logits)

    @pl.when(i >= n_row_tiles)
    def _():
        x1_ref[...] = jnp.zeros(x1_ref.shape, F32)


def _row_spec(tm, width):
    return pl.BlockSpec((tm, width), lambda i: (i, 0))


def _const_spec(shape):
    return pl.BlockSpec(shape, lambda i: (0,) * len(shape), pipeline_mode=pl.Buffered(1))


def _post(os_, lses, pool, x, wo, g1, b1, wrh, wrl, br, *, tm, name, filler_tiles=0, x1_into=None, row0=0):
    m, d = x.shape
    n = m // tm
    pitch = _row_pitch(d)
    assert (tm * pitch) % SUBLANES == 0
    hexp = _head_expand().astype(BF16)
    last = n - 1

    def rows(width):
        return pl.BlockSpec((tm, width), lambda i: (jnp.minimum(i, last), 0))

    in_specs = ([pl.BlockSpec((ATTN_WIDTH // LANES, tm, LANES), lambda i: (0, jnp.minimum(i, last), 0))] * 3
                + [rows(LANES)] * 3
                + [rows(POOL_WIDTH), rows(d), _const_spec((LANES, ATTN_WIDTH)),
                   _const_spec((d, d)), _const_spec((1, d)), _const_spec((1, d)),
                   _const_spec((d, LANES)), _const_spec((d, LANES)), _const_spec((1, LANES))])
    operands = [*os_, *lses, pool, x, hexp, wo, g1, b1, wrh, wrl, br]
    aliases = {}
    if x1_into is None:
        x1_shape = jax.ShapeDtypeStruct(((m + filler_tiles * tm) * pitch, LANES), F32)
    else:
        assert filler_tiles == 0 and row0 % tm == 0
        in_specs.append(pl.BlockSpec(memory_space=pl.ANY))
        operands.append(x1_into)
        aliases = {len(operands) - 1: 0}
        x1_shape = jax.ShapeDtypeStruct(x1_into.shape, F32)
    return pl.pallas_call(
        functools.partial(_post_kernel, n_row_tiles=n),
        grid=(n + filler_tiles,),
        in_specs=in_specs,
        out_specs=[pl.BlockSpec((tm * pitch, LANES), lambda i: (row0 // tm + i, 0)), rows(LANES)],
        out_shape=[x1_shape, jax.ShapeDtypeStruct((m, LANES), F32)],
        input_output_aliases=aliases,
        compiler_params=_cparams(("arbitrary",)),
        name=name,
    )(*operands)


def _routing_plan(pair_expert, n_tiles):
    p = pair_expert.shape[0]
    experts = jnp.arange(N_EXPERTS, dtype=jnp.int32)
    onehot = (pair_expert[:, None] == experts[None, :]).astype(jnp.int32)
    csum = jnp.cumsum(onehot, axis=0)
    rank = jnp.take_along_axis(csum, pair_expert[:, None], axis=1)[:, 0] - 1
    counts = csum[-1]
    tiles_per = (counts + MOE_TILE - 1) // MOE_TILE
    tile_end = jnp.cumsum(tiles_per)
    tile_start = tile_end - tiles_per
    dest = (tile_start[pair_expert] * MOE_TILE + rank).astype(jnp.int32)
    n_used = tile_end[-1]
    tile_ids = jnp.arange(n_tiles, dtype=jnp.int32)
    tile_expert = jnp.sum((tile_ids[:, None] >= tile_end[None, :]).astype(jnp.int32), axis=1)
    last_expert = jnp.sum((n_used - 1 >= tile_end).astype(jnp.int32))
    tile_expert = jnp.where(tile_ids < n_used, tile_expert, last_expert).astype(jnp.int32)
    slot_token = jnp.zeros((n_tiles * MOE_TILE,), jnp.int32).at[dest].set(
        jnp.arange(p, dtype=jnp.int32) // 2, unique_indices=True, mode="promise_in_bounds")
    return dest, slot_token, tile_expert, n_used.reshape(1).astype(jnp.int32)


def _moe_kernel(te_ref, nu_ref, st_ref, x_ref, wg_ref, wu_ref, wd_ref, ys_ref,
                buf_ref, wgb_ref, wub_ref, wdb_ref, sem_ref, *, nc, pitch):
    c = pl.program_id(0)
    n_tiles = pl.num_programs(0)
    n_used = nu_ref[0]

    def issue(tile, slot):
        def body(i, carry):
            src = st_ref[tile * MOE_TILE + i]
            pltpu.make_async_copy(x_ref.at[pl.ds(src, nc), :], buf_ref.at[slot, pl.ds(i * pitch, nc), :],
                                  sem_ref.at[slot]).start()
            return carry

        lax.fori_loop(0, MOE_TILE, body, 0, unroll=GATHER_UNROLL)

    def wait(slot):
        pltpu.make_async_copy(x_ref.at[pl.ds(0, MOE_TILE * nc), :], buf_ref.at[slot, pl.ds(0, MOE_TILE * nc), :],
                              sem_ref.at[slot]).wait()

    slot = c % 2

    @pl.when((c == 0) & (n_used > 0))
    def _():
        issue(0, 0)

    @pl.when((c + 1 < n_tiles) & (c + 1 < n_used))
    def _():
        issue(c + 1, 1 - slot)

    prev_expert = te_ref[jnp.maximum(c - 1, 0)]

    @pl.when((c == 0) | (te_ref[c] != prev_expert))
    def _():
        wgb_ref[...] = wg_ref[...].astype(BF16)
        wub_ref[...] = wu_ref[...].astype(BF16)
        wdb_ref[...] = wd_ref[...].astype(BF16)

    @pl.when(c < n_used)
    def _():
        wait(slot)
        x = _load_row_major(buf_ref.at[slot], MOE_TILE).astype(BF16)
        gate = jnp.dot(x, wgb_ref[...], preferred_element_type=F32)
        up = jnp.dot(x, wub_ref[...], preferred_element_type=F32)
        h = jax.nn.silu(gate) * up
        _store_row_major(ys_ref, jnp.dot(h.astype(BF16), wdb_ref[...], preferred_element_type=F32))

    @pl.when(c >= n_used)
    def _():
        ys_ref[...] = jnp.zeros(ys_ref.shape, ys_ref.dtype)


def _moe(x1, slot_row, tile_expert, n_used, w_gate, w_up, w_down, n_tiles):
    d = w_gate.shape[1]
    nc, pitch = d // LANES, _row_pitch(d)
    f = w_gate.shape[-1]
    grid_spec = pltpu.PrefetchScalarGridSpec(
        num_scalar_prefetch=3,
        grid=(n_tiles,),
        in_specs=[
            pl.BlockSpec(memory_space=pl.ANY),
            pl.BlockSpec((None, d, f), lambda c, te, nu, st: (te[c], 0, 0)),
            pl.BlockSpec((None, d, f), lambda c, te, nu, st: (te[c], 0, 0)),
            pl.BlockSpec((None, f, d), lambda c, te, nu, st: (te[c], 0, 0)),
        ],
        out_specs=pl.BlockSpec((MOE_TILE * pitch, LANES), lambda c, te, nu, st: (c, 0)),
        scratch_shapes=[
            pltpu.VMEM((2, MOE_TILE * pitch, LANES), F32),
            pltpu.VMEM((d, f), BF16),
            pltpu.VMEM((d, f), BF16),
            pltpu.VMEM((f, d), BF16),
            pltpu.SemaphoreType.DMA((2,)),
        ],
    )
    return pl.pallas_call(
        functools.partial(_moe_kernel, nc=nc, pitch=pitch),
        grid_spec=grid_spec,
        out_shape=jax.ShapeDtypeStruct((n_tiles * MOE_TILE * pitch, LANES), F32),
        compiler_params=_cparams(("arbitrary",)),
        name="moe",
    )(tile_expert, n_used, slot_row, x1, w_gate, w_up, w_down)


def _final_kernel(dest_ref, ys_ref, x1_ref, route_ref, p_ref, g2_ref, b2_ref, wpg_ref, wp_ref, o_ref,
                  buf_ref, sem_ref, *, tm, pair0, nc, pitch):
    i = pl.program_id(0)
    n_steps = pl.num_programs(0)

    def issue(step, slot):
        def body(r, carry):
            for k in range(2):
                src = dest_ref[pair0 + (step * tm + r) * 2 + k]
                pltpu.make_async_copy(ys_ref.at[pl.ds(src, nc), :], buf_ref.at[slot, k, pl.ds(r * pitch, nc), :],
                                      sem_ref.at[slot]).start()
            return carry

        lax.fori_loop(0, tm, body, 0, unroll=GATHER_UNROLL)

    def wait(slot):
        for k in range(2):
            pltpu.make_async_copy(ys_ref.at[pl.ds(0, tm * nc), :], buf_ref.at[slot, k, pl.ds(0, tm * nc), :],
                                  sem_ref.at[slot]).wait()

    slot = i % 2

    @pl.when(i == 0)
    def _():
        issue(0, 0)

    @pl.when(i + 1 < n_steps)
    def _():
        issue(i + 1, 1 - slot)

    wait(slot)
    route = route_ref[...]
    y = (route[:, 2:3] * _load_row_major(buf_ref.at[slot, 0], tm)
         + route[:, 3:4] * _load_row_major(buf_ref.at[slot, 1], tm))
    x2 = _layer_norm(ALPHA * _load_row_major(x1_ref, tm) + y, g2_ref[...], b2_ref[...])
    gate = jax.nn.sigmoid(jnp.dot(x2.astype(BF16), wpg_ref[...], preferred_element_type=F32))
    ple = jnp.dot(p_ref[...].astype(BF16), wp_ref[...], preferred_element_type=F32)
    o_ref[...] = x2 + gate * ple


def _final(dest, ys, x1, route, p, g2, b2, wpg, wp, *, tm, row0):
    m, pd = p.shape
    d = wpg.shape[0]
    nc, pitch = d // LANES, _row_pitch(d)
    assert m % tm == 0 and row0 % tm == 0
    pair0 = 2 * row0
    grid_spec = pltpu.PrefetchScalarGridSpec(
        num_scalar_prefetch=1,
        grid=(m // tm,),
        in_specs=[
            pl.BlockSpec(memory_space=pl.ANY),
            pl.BlockSpec((tm * pitch, LANES), lambda i, dst: (row0 // tm + i, 0)),
            pl.BlockSpec((tm, LANES), lambda i, dst: (i, 0)),
            pl.BlockSpec((tm, pd), lambda i, dst: (i, 0)),
            pl.BlockSpec((1, d), lambda i, dst: (0, 0)),
            pl.BlockSpec((1, d), lambda i, dst: (0, 0)),
            pl.BlockSpec((d, d), lambda i, dst: (0, 0)),
            pl.BlockSpec((pd, d), lambda i, dst: (0, 0)),
        ],
        out_specs=pl.BlockSpec((tm, d), lambda i, dst: (i, 0)),
        scratch_shapes=[pltpu.VMEM((2, 2, tm * pitch, LANES), F32), pltpu.SemaphoreType.DMA((2,))],
    )
    return pl.pallas_call(
        functools.partial(_final_kernel, tm=tm, pair0=pair0, nc=nc, pitch=pitch),
        grid_spec=grid_spec,
        out_shape=jax.ShapeDtypeStruct((m, d), F32),
        compiler_params=_cparams(("arbitrary",)),
        name="final",
    )(dest, ys, x1, route, p, g2, b2, wpg, wp)


def _pick_tile(m, pref):
    t = pref
    while m % t:
        t //= 2
    return t


def _layer(xp, xs, caches, state_pool, pp, ps, w):
    s, d = xp.shape
    db = xs.shape[0]

    cos_p, sin_p = _rope_tables(jnp.arange(s, dtype=jnp.int32))
    dils = tuple(dil for _, dil in DILATED_PATTERNS)
    qkv_p, u_p = _in_proj(xp, w["w_in"], cos_p, sin_p, tm=_pick_tile(s, 512), dils=dils, qkv_dtype=BF16)
    os_, lses = [], []
    for pi, (window, dil) in enumerate(DILATED_PATTERNS):
        o, l = _attn_pattern(qkv_p[pi], pi, dil, window)
        os_.append(o)
        lses.append(l)
    pool_p = _pool_prompt(u_p, w["w_pool"], w["pool_scale"], tm=_pick_tile(s, 512))

    cos_s, sin_s = _rope_tables(jnp.full((db,), PAST_LEN, jnp.int32))
    (qkv_s,), u_s = _in_proj(xs, w["w_in_f32"], cos_s, sin_s, tm=db, dils=None, qkv_dtype=F32)
    zt_s = qkv_s.T
    os_s, lses_s, kv_s = [], [], []
    for pi, ((window, dil), c) in enumerate(zip(DILATED_PATTERNS, caches)):
        cn, ot, lt = _sample_attn(zt_s, jnp.transpose(c, (0, 2, 3, 4, 1)), pi, dil)
        kv_s.append(jnp.transpose(cn, (0, 4, 1, 2, 3)))
        os_s.append(jnp.transpose(ot.T[:db].reshape(db, ATTN_WIDTH // LANES, LANES), (1, 0, 2)))
        lses_s.append(lt.T[:db])
    pool_s = _sample_pool(u_s, state_pool, w["w_pool_f32"], w["pool_scale"])

    wr_hi, wr_lo = _split_bf16(w["w_router"])
    post_w = (w["ln1_g"], w["ln1_b"], wr_hi, wr_lo, w["b_router"])
    tm_p = _pick_tile(s, 256)
    assert s % db == 0 and tm_p >= db
    x1, route_p = _post(os_, lses, pool_p, xp, w["w_out"], *post_w, tm=tm_p, name="post_prompt",
                        filler_tiles=1)
    x1, route_s = _post(os_s, lses_s, pool_s, xs, w["w_out_f32"], *post_w, tm=db, name="post_sample",
                        x1_into=x1, row0=s)

    pair_expert = jnp.concatenate([route_p[:, 0:2].reshape(-1), route_s[:, 0:2].reshape(-1)]).astype(jnp.int32)
    n_pairs = pair_expert.shape[0]
    n_tiles = -(-n_pairs // MOE_TILE) + N_EXPERTS
    dest, slot_token, tile_expert, n_used = _routing_plan(pair_expert, n_tiles)
    pitch = _row_pitch(d)
    ys = _moe(x1, slot_token * pitch, tile_expert, n_used, w["w_gate"], w["w_up"], w["w_down"], n_tiles)

    fin_w = (w["ln2_g"], w["ln2_b"], w["w_ple_gate"], w["w_ple"])
    y_p = _final(dest * pitch, ys, x1, route_p, pp, *fin_w, tm=tm_p, row0=0)
    y_s = _final(dest * pitch, ys, x1, route_s, ps, *fin_w, tm=db, row0=s)

    kv_p = []
    for pi, (window, dil) in enumerate(DILATED_PATTERNS):
        keep = min(window, s)
        kv = qkv_p[pi][:, (s - keep) // dil:, ATTN_WIDTH:3 * ATTN_WIDTH].astype(F32)
        kv = jnp.transpose(kv, (1, 0, 2))
        kv_p.append(kv.reshape(1, keep, 2, N_HEADS, HEAD_DIM))
    pool_state_p = u_p[s - POOL_STATE:][None]
    pool_state_s = jnp.concatenate([state_pool[:, 1:], u_s[:, None, :]], axis=1)
    return y_p, y_s, kv_p, pool_state_p, kv_s, pool_state_s


def kernel(x_prompt, x_sample, cache_kv_w128_d1, cache_kv_w512_d4, cache_kv_w2048_d16, state_pool, p_prompt, p_sample, w_in, w_out, w_pool, pool_scale, ln1_g, ln1_b, w_group_router, b_group_router, w_expert_router, b_expert_router, w_gate, w_up, w_down, ln2_g, ln2_b, w_ple, w_ple_gate):
    assert w_in.shape[0] == DEPTH == 1 and x_prompt.shape[0] == 1 and x_sample.shape[1] == 1
    d = x_prompt.shape[-1]
    pad = LANES - N_GROUPS - N_EXPERTS
    w = {
        "w_in": w_in[0].astype(BF16), "w_in_f32": w_in[0],
        "w_out": w_out[0].astype(BF16), "w_out_f32": w_out[0],
        "w_pool": w_pool[0].astype(BF16), "w_pool_f32": w_pool[0],
        "pool_scale": pool_scale[0].reshape(1, POOL_WIDTH),
        "ln1_g": ln1_g[0].reshape(1, d), "ln1_b": ln1_b[0].reshape(1, d),
        "ln2_g": ln2_g[0].reshape(1, d), "ln2_b": ln2_b[0].reshape(1, d),
        "w_router": jnp.concatenate([w_group_router[0], w_expert_router[0], jnp.zeros((d, pad), F32)], axis=1),
        "b_router": jnp.concatenate([b_group_router[0], b_expert_router[0], jnp.zeros((pad,), F32)]).reshape(1, LANES),
        "w_gate": w_gate[0], "w_up": w_up[0], "w_down": w_down[0],
        "w_ple": w_ple[0].astype(BF16),
        "w_ple_gate": w_ple_gate[0].astype(BF16),
    }
    caches = [cache_kv_w128_d1[0], cache_kv_w512_d4[0], cache_kv_w2048_d16[0]]
    y_p, y_s, kv_p, pool_p, kv_s, pool_s = _layer(
        x_prompt[0], x_sample[:, 0], caches, state_pool[0], p_prompt[0, 0], p_sample[0, :, 0], w)
    return (y_p[None], y_s[:, None], kv_p[0][None], kv_p[1][None], kv_p[2][None], pool_p[None],
            kv_s[0][None], kv_s[1][None], kv_s[2][None], pool_s[None])
```

```python
import functools

import jax
import jax.numpy as jnp
from jax import lax
from jax.experimental import pallas as pl
from jax.experimental.pallas import tpu as pltpu

F32 = jnp.float32
BF16 = jnp.bfloat16

PAST_LEN = 8192
HEAD_DIM = 64
N_HEADS = 16
ATTN_WIDTH = N_HEADS * HEAD_DIM
DILATED_PATTERNS = ((128, 1), (512, 4), (2048, 16))
N_PATTERNS = len(DILATED_PATTERNS)
BAND = 128
POOL_WINDOWS = (2, 4, 8, 16)
POOL_GROUP = 256
POOL_WIDTH = POOL_GROUP * len(POOL_WINDOWS)
POOL_STATE = max(POOL_WINDOWS) - 1
ROT_DIM = HEAD_DIM // 4
ROPE_THETA = 500000.0
QKV_WIDTH = N_PATTERNS * 3 * ATTN_WIDTH
N_GROUPS = 4
EXPERTS_PER_GROUP = 8
N_EXPERTS = N_GROUPS * EXPERTS_PER_GROUP
DEPTH = 1
ALPHA = (2.0 * DEPTH) ** 0.25
LN_EPS = 1e-5
NEG_INF = -1e30

LANES = 128
SUBLANES = 8
VMEM_LIMIT = 48 * 1024 * 1024

MOE_TILE = 256
HALO = 16
GATHER_UNROLL = 8


def _cparams(semantics):
    return pltpu.CompilerParams(dimension_semantics=semantics, vmem_limit_bytes=VMEM_LIMIT)


def _rope_tables(pos):
    inv_freq = ROPE_THETA ** (-jnp.arange(0, ROT_DIM, 2, dtype=F32) / ROT_DIM)
    ang = pos.astype(F32)[:, None] * inv_freq[None, :]
    cos, sin = jnp.cos(ang), jnp.sin(ang)
    t = pos.shape[0]
    rest = HEAD_DIM - ROT_DIM
    c64 = jnp.concatenate([cos, cos, jnp.ones((t, rest), F32)], -1)
    s64 = jnp.concatenate([-sin, sin, jnp.zeros((t, rest), F32)], -1)
    return jnp.tile(c64, (1, LANES // HEAD_DIM)), jnp.tile(s64, (1, LANES // HEAD_DIM))


def _rope_chunk(xc, cos, sin):
    half = ROT_DIM // 2
    lane = lax.broadcasted_iota(jnp.int32, xc.shape, 1) & (HEAD_DIM - 1)
    upper = pltpu.roll(xc, LANES - half, 1)
    lower = pltpu.roll(xc, half, 1)
    partner = jnp.where(lane < half, upper, lower)
    return xc * cos + partner * sin


def _split_bf16(a):
    hi = a.astype(BF16)
    return hi, (a - hi.astype(F32)).astype(BF16)


def _mm(a, w):
    if w.dtype == BF16:
        return jnp.dot(a.astype(BF16), w, preferred_element_type=F32)
    ah, al = _split_bf16(a.astype(F32))
    wh, wl = _split_bf16(w)
    return (jnp.dot(ah, wh, preferred_element_type=F32) + jnp.dot(al, wh, preferred_element_type=F32)
            + jnp.dot(ah, wl, preferred_element_type=F32))


ROW_PAD = 1


def _row_pitch(d):
    return d // LANES + ROW_PAD


def _store_row_major(ref, val):
    n, d = val.shape
    nc, pitch = d // LANES, _row_pitch(d)
    for c in range(nc):
        ref[pl.ds(c, n, stride=pitch), :] = val[:, c * LANES:(c + 1) * LANES]
    for c in range(nc, pitch):
        ref[pl.ds(c, n, stride=pitch), :] = jnp.zeros((n, LANES), F32)


def _load_row_major(ref, n):
    pitch = ref.shape[0] // n
    return jnp.concatenate([ref[pl.ds(c, n, stride=pitch), :] for c in range(pitch - ROW_PAD)], axis=1)


ROLE_TILES = 3


def _chunk(c):
    return slice(c * LANES, (c + 1) * LANES)


def _in_proj_rows_kernel(x_ref, w_ref, cos_ref, sin_ref, qkv_ref, u_ref):
    j = pl.program_id(0)
    n_qkv_tiles = N_PATTERNS * ROLE_TILES
    acc = _mm(x_ref[...], w_ref[...])
    role = j % ROLE_TILES
    is_qkv = j < n_qkv_tiles

    @pl.when(is_qkv & (role < 2))
    def _():
        cos = cos_ref[...]
        sin = sin_ref[...]
        for c in range(acc.shape[1] // LANES):
            qkv_ref[:, _chunk(c)] = _rope_chunk(acc[:, _chunk(c)], cos, sin)

    @pl.when(is_qkv & (role == 2))
    def _():
        qkv_ref[...] = acc

    @pl.when(j >= n_qkv_tiles)
    def _():
        u_ref[...] = acc


def _in_proj_rows(x, w, cos_t, sin_t):
    m, d = x.shape
    tn = ATTN_WIDTH
    assert w.shape[1] == QKV_WIDTH + POOL_WIDTH and POOL_WIDTH == tn
    n_qkv_tiles = N_PATTERNS * ROLE_TILES
    return pl.pallas_call(
        _in_proj_rows_kernel,
        grid=(n_qkv_tiles + 1,),
        in_specs=[
            pl.BlockSpec((m, d), lambda j: (0, 0)),
            pl.BlockSpec((d, tn), lambda j: (0, j)),
            pl.BlockSpec((m, LANES), lambda j: (0, 0)),
            pl.BlockSpec((m, LANES), lambda j: (0, 0)),
        ],
        out_specs=[pl.BlockSpec((m, tn), lambda j: (0, jnp.minimum(j, n_qkv_tiles - 1))),
                   pl.BlockSpec((m, tn), lambda j: (0, 0))],
        out_shape=[jax.ShapeDtypeStruct((m, QKV_WIDTH), F32), jax.ShapeDtypeStruct((m, POOL_WIDTH), F32)],
        compiler_params=_cparams(("arbitrary",)),
        name="in_proj_rows",
    )(x, w, cos_t, sin_t)


def _proj_pattern_kernel(x_ref, w_ref, cos_ref, sin_ref, out_ref, xb_ref, acc_ref, stage_ref, *, dil):
    t = pl.program_id(0)
    n_chunks, tm, _ = stage_ref.shape

    @pl.when(t == 0)
    def _():
        acc_ref[...] = jnp.zeros(acc_ref.shape, F32)

    @pl.when(t % ROLE_TILES == 0)
    def _():
        xb_ref[...] = x_ref[...].astype(BF16)

    prev = acc_ref.at[(t + 1) % 2]
    cos = cos_ref[...]
    sin = sin_ref[...]
    for c in range(n_chunks):
        stage_ref[c] = _rope_chunk(prev[:, _chunk(c)], cos, sin)
    for r in range(dil):
        for c in range(n_chunks):
            rows = stage_ref[c, pl.ds(r, tm // dil, stride=dil), :] if dil > 1 else stage_ref[c]
            out_ref[r, :, _chunk(c)] = rows.astype(out_ref.dtype)
    acc_ref[t % 2] = jnp.dot(xb_ref[...], w_ref[...], preferred_element_type=F32)


def _proj_pattern(x, w, tables, pi, dil, *, tm):
    m, d = x.shape
    tn = ATTN_WIDTH
    assert m % tm == 0 and tm % (dil * 2 * SUBLANES) == 0
    n = (m // tm) * ROLE_TILES

    def cur(t):
        return jnp.minimum(t, n - 1)

    def prv(t):
        return jnp.maximum(t - 1, 0)

    def table_map(t):
        identity = (prv(t) % ROLE_TILES == ROLE_TILES - 1).astype(jnp.int32)
        return (identity, prv(t) // ROLE_TILES, 0)

    return pl.pallas_call(
        functools.partial(_proj_pattern_kernel, dil=dil),
        grid=(n + 1,),
        in_specs=[
            pl.BlockSpec((tm, d), lambda t: (cur(t) // ROLE_TILES, 0)),
            pl.BlockSpec((d, tn), lambda t: (0, pi * ROLE_TILES + cur(t) % ROLE_TILES)),
            pl.BlockSpec((None, tm, LANES), table_map),
            pl.BlockSpec((None, tm, LANES), table_map),
        ],
        out_specs=pl.BlockSpec((dil, tm // dil, tn), lambda t: (0, prv(t) // ROLE_TILES, prv(t) % ROLE_TILES)),
        out_shape=jax.ShapeDtypeStruct((dil, m // dil, ROLE_TILES * tn), BF16),
        scratch_shapes=[pltpu.VMEM((tm, d), BF16), pltpu.VMEM((2, tm, tn), F32),
                        pltpu.VMEM((tn // LANES, tm, LANES), F32)],
        compiler_params=_cparams(("arbitrary",)),
        name=f"proj_p{pi}",
    )(x, w, *tables)


def _proj_u_kernel(x_ref, w_ref, u_ref):
    u_ref[...] = jnp.dot(x_ref[...].astype(BF16), w_ref[...], preferred_element_type=F32)


def _proj_u(x, w, *, tm):
    m, d = x.shape
    n_qkv_tiles = N_PATTERNS * ROLE_TILES
    return pl.pallas_call(
        _proj_u_kernel,
        grid=(m // tm,),
        in_specs=[pl.BlockSpec((tm, d), lambda i: (i, 0)),
                  pl.BlockSpec((d, POOL_WIDTH), lambda i: (0, n_qkv_tiles), pipeline_mode=pl.Buffered(1))],
        out_specs=pl.BlockSpec((tm, POOL_WIDTH), lambda i: (i, 0)),
        out_shape=jax.ShapeDtypeStruct((m, POOL_WIDTH), F32),
        compiler_params=_cparams(("arbitrary",)),
        name="proj_u",
    )(x, w)


def _attn_kernel(q_ref, kp_ref, kc_ref, vp_ref, vc_ref, o_ref, lse_ref,
                 lim_ref, s_ref, e_ref, m_ref, den_ref, *, r_max, dil):
    n = pl.program_id(0)
    r = pl.program_id(1)
    qi = lax.broadcasted_iota(jnp.int32, (BAND, 2 * BAND), 0)
    ki = lax.broadcasted_iota(jnp.int32, (BAND, 2 * BAND), 1)
    dist = BAND + qi - ki
    valid = (dist >= 0) & (dist <= r_max) & ((n > 0) | (ki >= BAND))
    lim_ref[...] = jnp.where(valid, jnp.float32(jnp.finfo(F32).max), jnp.float32(NEG_INF))

    lane = lax.broadcasted_iota(jnp.int32, (BAND, LANES), 1)
    heads_per_chunk = LANES // HEAD_DIM
    n_chunks = ATTN_WIDTH // LANES
    lane_head = lane >> (HEAD_DIM.bit_length() - 1)
    qscale = [jnp.where(lane_head == hh, HEAD_DIM ** -0.5, 0.0).astype(BF16) for hh in range(heads_per_chunk)]
    nt = (((1,), (1,)), ((), ()))

    for c in range(n_chunks):
        sl = slice(c * LANES, (c + 1) * LANES)
        q2 = q_ref[:, sl]
        kp, kc = kp_ref[:, sl], kc_ref[:, sl]
        for hh in range(heads_per_chunk):
            h = c * heads_per_chunk + hh
            qm = q2 * qscale[hh]
            s = jnp.concatenate(
                [lax.dot_general(qm, kp, nt, preferred_element_type=F32),
                 lax.dot_general(qm, kc, nt, preferred_element_type=F32)], axis=1)
            s = jnp.minimum(s, lim_ref[...])
            s_ref[h] = s
            m_ref[h] = jnp.broadcast_to(jnp.max(s, axis=-1, keepdims=True), (BAND, LANES))

    lse_acc = jnp.zeros((BAND, LANES), F32)
    for h in range(N_HEADS):
        m = m_ref[h]
        e = jnp.exp(s_ref[h] - jnp.concatenate([m, m], axis=1))
        e_ref[h] = e.astype(BF16)
        den = jnp.broadcast_to(jnp.sum(e, axis=-1, keepdims=True), (BAND, LANES))
        den_ref[h] = den
        lse_acc = jnp.where(lane == h, m + jnp.log(den), lse_acc)

    rows = slice(None) if dil == 1 else pl.ds(r, BAND, stride=dil)
    lse_ref[rows, :] = lse_acc
    for c in range(n_chunks):
        sl = slice(c * LANES, (c + 1) * LANES)
        vp, vc = vp_ref[:, sl], vc_ref[:, sl]
        o2 = jnp.zeros((BAND, LANES), F32)
        for hh in range(heads_per_chunk):
            h = c * heads_per_chunk + hh
            pv = (jnp.dot(e_ref[h, :, :BAND], vp, preferred_element_type=F32)
                  + jnp.dot(e_ref[h, :, BAND:], vc, preferred_element_type=F32))
            o2 = jnp.where(lane_head == hh, pv / den_ref[h], o2)
        o_ref[c, rows, :] = o2


def _attn_pattern(qkv, pi, dil, window):
    s = qkv.shape[0] * qkv.shape[1]
    assert qkv.shape[0] == dil and s % (dil * BAND) == 0
    nb = s // (dil * BAND)
    r_max = window // dil

    def cur(role):
        return lambda n, r: (r, n, role)

    def prev(role):
        return lambda n, r: (r, jnp.maximum(n - 1, 0), role)

    blk = (None, BAND, ATTN_WIDTH)
    span = BAND * dil
    return pl.pallas_call(
        functools.partial(_attn_kernel, r_max=r_max, dil=dil),
        grid=(nb, dil),
        in_specs=[
            pl.BlockSpec(blk, cur(0)),
            pl.BlockSpec(blk, prev(1)),
            pl.BlockSpec(blk, cur(1)),
            pl.BlockSpec(blk, prev(2)),
            pl.BlockSpec(blk, cur(2)),
        ],
        out_specs=[
            pl.BlockSpec((ATTN_WIDTH // LANES, span, LANES), lambda n, r: (0, n, 0)),
            pl.BlockSpec((span, LANES), lambda n, r: (n, 0)),
        ],
        out_shape=[
            jax.ShapeDtypeStruct((ATTN_WIDTH // LANES, s, LANES), F32),
            jax.ShapeDtypeStruct((s, LANES), F32),
        ],
        scratch_shapes=[
            pltpu.VMEM((BAND, 2 * BAND), F32),
            pltpu.VMEM((N_HEADS, BAND, 2 * BAND), F32),
            pltpu.VMEM((N_HEADS, BAND, 2 * BAND), BF16),
            pltpu.VMEM((N_HEADS, BAND, LANES), F32),
            pltpu.VMEM((N_HEADS, BAND, LANES), F32),
        ],
        compiler_params=_cparams(("arbitrary", "arbitrary")),
        name=f"attn_p{pi}",
    )(qkv, qkv, qkv, qkv, qkv)


def _pool_groups(buf_ref, tm, pos0, w_pool_ref, scale_ref, store):
    row = lax.broadcasted_iota(jnp.int32, (tm, 1), 0) + pos0
    for g, win in enumerate(POOL_WINDOWS):
        sl = slice(g * POOL_GROUP, (g + 1) * POOL_GROUP)
        cur = buf_ref[HALO:HALO + tm, sl]
        acc = cur
        for jj in range(1, win):
            acc = acc + buf_ref[HALO - jj:HALO - jj + tm, sl]
        cnt = jnp.minimum(row + 1, win).astype(F32)
        d = acc / cnt - cur
        y = _mm(d, w_pool_ref[g])
        store(g, y * scale_ref[:, sl])


def _pool_kernel(u_ref, uprev_ref, w_pool_ref, scale_ref, o_ref, buf_ref, *, tm):
    i = pl.program_id(0)
    buf_ref[0:HALO, :] = jnp.where(i > 0, uprev_ref[...], jnp.zeros_like(uprev_ref))
    buf_ref[HALO:HALO + tm, :] = u_ref[...]

    def store(g, val):
        o_ref[:, g * POOL_GROUP:(g + 1) * POOL_GROUP] = val.astype(o_ref.dtype)

    _pool_groups(buf_ref, tm, i * tm, w_pool_ref, scale_ref, store)


def _pool_prompt(u, w_pool_bf16, pool_scale, *, tm):
    s = u.shape[0]
    assert s % tm == 0 and tm % HALO == 0
    per = tm // HALO
    return pl.pallas_call(
        functools.partial(_pool_kernel, tm=tm),
        grid=(s // tm,),
        in_specs=[
            pl.BlockSpec((tm, POOL_WIDTH), lambda i: (i, 0)),
            pl.BlockSpec((HALO, POOL_WIDTH), lambda i: (jnp.maximum(i * per - 1, 0), 0)),
            pl.BlockSpec((len(POOL_WINDOWS), POOL_GROUP, POOL_GROUP), lambda i: (0, 0, 0)),
            pl.BlockSpec((1, POOL_WIDTH), lambda i: (0, 0)),
        ],
        out_specs=pl.BlockSpec((tm, POOL_WIDTH), lambda i: (i, 0)),
        out_shape=jax.ShapeDtypeStruct((s, POOL_WIDTH), BF16),
        scratch_shapes=[pltpu.VMEM((HALO + tm, POOL_WIDTH), F32)],
        compiler_params=_cparams(("arbitrary",)),
        name="pool",
    )(u, u, w_pool_bf16, pool_scale)


def _sample_attn_kernel(zt_ref, c_ref, cn_ref, ot_ref, lt_ref, *, dil, hps, length):
    b = pl.program_id(0)
    g = pl.program_id(1)

    @pl.when((b == 0) & (g == 0))
    def _():
        ot_ref[...] = jnp.zeros(ot_ref.shape, F32)
        lt_ref[...] = jnp.zeros(lt_ref.shape, F32)

    rows = hps * HEAD_DIM
    base = pl.multiple_of(g * rows, rows)
    lane_z = lax.broadcasted_iota(jnp.int32, (rows, zt_ref.shape[1]), 1)

    def column(role):
        blk = zt_ref[pl.ds(role * ATTN_WIDTH + base, rows), :]
        return jnp.sum(jnp.where(lane_z == b, blk, 0.0), axis=-1, keepdims=True)

    qc, knc, vnc = column(0), column(1), column(2)
    scale = HEAD_DIM ** -0.5
    lane = lax.broadcasted_iota(jnp.int32, (1, length), 1)
    key_ok = (lane & (dil - 1)) == 0
    last = lane == length - 1
    lane_o = lax.broadcasted_iota(jnp.int32, (HEAD_DIM, LANES), 1)
    lane_l = lax.broadcasted_iota(jnp.int32, (1, LANES), 1)
    for h in range(hps):
        sl = slice(h * HEAD_DIM, (h + 1) * HEAD_DIM)
        q, kn, vn = qc[sl], knc[sl], vnc[sl]
        k = c_ref[0, h]
        v = c_ref[1, h]
        s = jnp.where(key_ok, jnp.sum(k * q, axis=0, keepdims=True) * scale, NEG_INF)
        s_n = jnp.sum(kn * q, axis=0, keepdims=True) * scale
        m = jnp.maximum(jnp.max(s, axis=-1, keepdims=True), s_n)
        e = jnp.exp(s - m)
        e_n = jnp.exp(s_n - m)
        den = jnp.sum(e, axis=-1, keepdims=True) + e_n
        o = (jnp.sum(v * e, axis=-1, keepdims=True) + vn * e_n) / den
        lse = m + jnp.log(den)
        cn_ref[0, h] = jnp.where(last, kn, pltpu.roll(k, length - 1, 1))
        cn_ref[1, h] = jnp.where(last, vn, pltpu.roll(v, length - 1, 1))
        r0 = pl.multiple_of(base + h * HEAD_DIM, HEAD_DIM)
        ot_ref[pl.ds(r0, HEAD_DIM), :] = jnp.where(lane_o == b, o, ot_ref[pl.ds(r0, HEAD_DIM), :])
        head = g * hps + h
        lt_ref[pl.ds(head, 1), :] = jnp.where(lane_l == b, lse, lt_ref[pl.ds(head, 1), :])


def _sample_attn(zt, c_t, pi, dil):
    db, _, _, _, length = c_t.shape
    assert db <= LANES and length == dil * BAND
    hps = min(N_HEADS, max(1, (4 * 2048) // length))
    blk = (None, 2, hps, HEAD_DIM, length)
    return pl.pallas_call(
        functools.partial(_sample_attn_kernel, dil=dil, hps=hps, length=length),
        grid=(db, N_HEADS // hps),
        in_specs=[
            pl.BlockSpec((3 * ATTN_WIDTH, db), lambda b, g: (pi, 0)),
            pl.BlockSpec(blk, lambda b, g: (b, 0, g, 0, 0)),
        ],
        out_specs=[
            pl.BlockSpec(blk, lambda b, g: (b, 0, g, 0, 0)),
            pl.BlockSpec((ATTN_WIDTH, LANES), lambda b, g: (0, 0)),
            pl.BlockSpec((LANES, LANES), lambda b, g: (0, 0)),
        ],
        out_shape=[
            jax.ShapeDtypeStruct(c_t.shape, c_t.dtype),
            jax.ShapeDtypeStruct((ATTN_WIDTH, LANES), F32),
            jax.ShapeDtypeStruct((LANES, LANES), F32),
        ],
        compiler_params=_cparams(("arbitrary", "arbitrary")),
        name=f"sample_attn_p{pi}",
    )(zt, c_t)


def _sample_pool_kernel(u_ref, sp_ref, w_pool_ref, scale_ref, pool_ref, buf_ref):
    buf_ref[...] = jnp.zeros(buf_ref.shape, F32)
    buf_ref[HALO - POOL_STATE:HALO, :] = sp_ref[...]
    buf_ref[HALO:HALO + 1, :] = u_ref[...]

    def store(g, val):
        pool_ref[:, g * POOL_GROUP:(g + 1) * POOL_GROUP] = val[0:1]

    _pool_groups(buf_ref, HALO, PAST_LEN, w_pool_ref, scale_ref, store)


def _head_expand():
    dim_head = jnp.arange(ATTN_WIDTH, dtype=jnp.int32) // HEAD_DIM
    return (jnp.arange(LANES, dtype=jnp.int32)[:, None] == dim_head[None, :]).astype(F32)


def _sample_pool(us, state_pool, w_pool_bf16, pool_scale):
    db = us.shape[0]
    pool = pl.pallas_call(
        _sample_pool_kernel,
        grid=(db,),
        in_specs=[
            pl.BlockSpec((None, 1, POOL_WIDTH), lambda b: (b, 0, 0)),
            pl.BlockSpec((None, POOL_STATE, POOL_WIDTH), lambda b: (b, 0, 0)),
            pl.BlockSpec((len(POOL_WINDOWS), POOL_GROUP, POOL_GROUP), lambda b: (0, 0, 0)),
            pl.BlockSpec((1, POOL_WIDTH), lambda b: (0, 0)),
        ],
        out_specs=pl.BlockSpec((None, 1, POOL_WIDTH), lambda b: (b, 0, 0)),
        out_shape=jax.ShapeDtypeStruct((db, 1, POOL_WIDTH), F32),
        scratch_shapes=[pltpu.VMEM((2 * HALO, POOL_WIDTH), F32)],
        compiler_params=_cparams(("arbitrary",)),
        name="sample_pool",
    )(us.reshape(db, 1, POOL_WIDTH), state_pool, w_pool_bf16, pool_scale)
    return pool.reshape(db, POOL_WIDTH)


def _layer_norm(xf, g, b):
    mu = jnp.mean(xf, axis=-1, keepdims=True)
    xc = xf - mu
    var = jnp.mean(xc * xc, axis=-1, keepdims=True)
    return xc * lax.rsqrt(var + LN_EPS) * g + b


def _route(logits):
    lane = lax.broadcasted_iota(jnp.int32, logits.shape, 1)
    lane_f = lane.astype(F32)
    neg = jnp.float32(-jnp.inf)

    def first_argmax(v, vmax):
        return jnp.min(jnp.where(v == vmax, lane_f, jnp.float32(LANES)), axis=-1, keepdims=True)

    gl = jnp.where(lane < N_GROUPS, logits, neg)
    gmax = jnp.max(gl, axis=-1, keepdims=True)
    g_sel = first_argmax(gl, gmax)
    g_gate = 1.0 / jnp.sum(jnp.exp(gl - gmax), axis=-1, keepdims=True)
    lane_group = (((lane + (EXPERTS_PER_GROUP - N_GROUPS)) >> 3) - 1).astype(F32)
    el = jnp.where(lane_group == g_sel, logits, neg)
    v1 = jnp.max(el, axis=-1, keepdims=True)
    i1 = first_argmax(el, v1)
    el2 = jnp.where(lane_f == i1, neg, el)
    v2 = jnp.max(el2, axis=-1, keepdims=True)
    i2 = first_argmax(el2, v2)
    t = jnp.exp(v2 - v1)
    w1 = (1.0 / (1.0 + t)) * g_gate
    w2 = (t / (1.0 + t)) * g_gate
    route = jnp.where(lane == 0, i1 - N_GROUPS, 0.0)
    route = jnp.where(lane == 1, i2 - N_GROUPS, route)
    route = jnp.where(lane == 2, w1, route)
    route = jnp.where(lane == 3, w2, route)
    return route


def _post_kernel(o0_ref, o1_ref, o2_ref, l0_ref, l1_ref, l2_ref, pool_ref, x_ref, hexp_ref,
                 wo_ref, g1_ref, b1_ref, wrh_ref, wrl_ref, br_ref, *rest, n_row_tiles):
    x1_ref, route_ref = rest[-2:]
    i = pl.program_id(0)

    @pl.when(i < n_row_tiles)
    def _():
        l0, l1, l2 = l0_ref[...], l1_ref[...], l2_ref[...]
        lmax = jnp.maximum(jnp.maximum(l0, l1), l2)
        e0, e1, e2 = jnp.exp(l0 - lmax), jnp.exp(l1 - lmax), jnp.exp(l2 - lmax)
        den = e0 + e1 + e2
        hexp = hexp_ref[...]
        attn = jnp.zeros((x_ref.shape[0], ATTN_WIDTH), F32)
        for o_ref, e in ((o0_ref, e0), (o1_ref, e1), (o2_ref, e2)):
            wh, wl = _split_bf16(e / den)
            w_x = (jnp.dot(wh, hexp, preferred_element_type=F32) + jnp.dot(wl, hexp, preferred_element_type=F32))
            o = jnp.concatenate([o_ref[c] for c in range(ATTN_WIDTH // LANES)], axis=1)
            attn = attn + o * w_x
        h = _mm(attn, wo_ref[0:ATTN_WIDTH, :]) + _mm(pool_ref[...], wo_ref[ATTN_WIDTH:, :])
        x1 = _layer_norm(ALPHA * x_ref[...] + h, g1_ref[...], b1_ref[...])
        _store_row_major(x1_ref, x1)
        xh, xl = _split_bf16(x1)
        wrh = wrh_ref[...]
        logits = (jnp.dot(xh, wrh, preferred_element_type=F32) + jnp.dot(xl, wrh, preferred_element_type=F32)
                  + jnp.dot(xh, wrl_ref[...], preferred_element_type=F32)) + br_ref[...]
        route_ref[...] = _route(logits)

    @pl.when(i >= n_row_tiles)
    def _():
        x1_ref[...] = jnp.zeros(x1_ref.shape, F32)


def _row_spec(tm, width):
    return pl.BlockSpec((tm, width), lambda i: (i, 0))


def _const_spec(shape):
    return pl.BlockSpec(shape, lambda i: (0,) * len(shape), pipeline_mode=pl.Buffered(1))


def _post(os_, lses, pool, x, wo, g1, b1, wrh, wrl, br, *, tm, name, filler_tiles=0, x1_into=None, row0=0):
    m, d = x.shape
    n = m // tm
    pitch = _row_pitch(d)
    assert (tm * pitch) % SUBLANES == 0
    hexp = _head_expand().astype(BF16)
    last = n - 1

    def rows(width):
        return pl.BlockSpec((tm, width), lambda i: (jnp.minimum(i, last), 0))

    in_specs = ([pl.BlockSpec((ATTN_WIDTH // LANES, tm, LANES), lambda i: (0, jnp.minimum(i, last), 0))] * 3
                + [rows(LANES)] * 3
                + [rows(POOL_WIDTH), rows(d), _const_spec((LANES, ATTN_WIDTH)),
                   _const_spec((d, d)), _const_spec((1, d)), _const_spec((1, d)),
                   _const_spec((d, LANES)), _const_spec((d, LANES)), _const_spec((1, LANES))])
    operands = [*os_, *lses, pool, x, hexp, wo, g1, b1, wrh, wrl, br]
    aliases = {}
    if x1_into is None:
        x1_shape = jax.ShapeDtypeStruct(((m + filler_tiles * tm) * pitch, LANES), F32)
    else:
        assert filler_tiles == 0 and row0 % tm == 0
        in_specs.append(pl.BlockSpec(memory_space=pl.ANY))
        operands.append(x1_into)
        aliases = {len(operands) - 1: 0}
        x1_shape = jax.ShapeDtypeStruct(x1_into.shape, F32)
    return pl.pallas_call(
        functools.partial(_post_kernel, n_row_tiles=n),
        grid=(n + filler_tiles,),
        in_specs=in_specs,
        out_specs=[pl.BlockSpec((tm * pitch, LANES), lambda i: (row0 // tm + i, 0)), rows(LANES)],
        out_shape=[x1_shape, jax.ShapeDtypeStruct((m, LANES), F32)],
        input_output_aliases=aliases,
        compiler_params=_cparams(("arbitrary",)),
        name=name,
    )(*operands)


def _routing_plan(pair_expert, n_tiles):
    p = pair_expert.shape[0]
    experts = jnp.arange(N_EXPERTS, dtype=jnp.int32)
    onehot = (pair_expert[:, None] == experts[None, :]).astype(jnp.int32)
    csum = jnp.cumsum(onehot, axis=0)
    rank = jnp.take_along_axis(csum, pair_expert[:, None], axis=1)[:, 0] - 1
    counts = csum[-1]
    tiles_per = (counts + MOE_TILE - 1) // MOE_TILE
    tile_end = jnp.cumsum(tiles_per)
    tile_start = tile_end - tiles_per
    dest = (tile_start[pair_expert] * MOE_TILE + rank).astype(jnp.int32)
    n_used = tile_end[-1]
    tile_ids = jnp.arange(n_tiles, dtype=jnp.int32)
    tile_expert = jnp.sum((tile_ids[:, None] >= tile_end[None, :]).astype(jnp.int32), axis=1)
    last_expert = jnp.sum((n_used - 1 >= tile_end).astype(jnp.int32))
    tile_expert = jnp.where(tile_ids < n_used, tile_expert, last_expert).astype(jnp.int32)
    slot_token = jnp.zeros((n_tiles * MOE_TILE,), jnp.int32).at[dest].set(
        jnp.arange(p, dtype=jnp.int32) // 2, unique_indices=True, mode="promise_in_bounds")
    return dest, slot_token, tile_expert, n_used.reshape(1).astype(jnp.int32)


def _moe_kernel(te_ref, nu_ref, st_ref, x_ref, wg_ref, wu_ref, wd_ref, ys_ref,
                buf_ref, wgb_ref, wub_ref, wdb_ref, sem_ref, *, nc, pitch):
    c = pl.program_id(0)
    n_tiles = pl.num_programs(0)
    n_used = nu_ref[0]

    def issue(tile, slot):
        def body(g, carry):
            for k in range(GATHER_UNROLL):
                i = g * GATHER_UNROLL + k
                src = st_ref[tile * MOE_TILE + i]
                pltpu.make_async_copy(x_ref.at[pl.ds(src, nc), :], buf_ref.at[slot, pl.ds(i * pitch, nc), :],
                                      sem_ref.at[slot]).start(priority=k % 2)
            return carry

        lax.fori_loop(0, MOE_TILE // GATHER_UNROLL, body, 0)

    def wait(slot):
        pltpu.make_async_copy(x_ref.at[pl.ds(0, MOE_TILE * nc), :], buf_ref.at[slot, pl.ds(0, MOE_TILE * nc), :],
                              sem_ref.at[slot]).wait()

    slot = c % 2

    @pl.when((c == 0) & (n_used > 0))
    def _():
        issue(0, 0)

    @pl.when((c + 1 < n_tiles) & (c + 1 < n_used))
    def _():
        issue(c + 1, 1 - slot)

    prev_expert = te_ref[jnp.maximum(c - 1, 0)]

    @pl.when((c == 0) | (te_ref[c] != prev_expert))
    def _():
        wgb_ref[...] = wg_ref[...].astype(BF16)
        wub_ref[...] = wu_ref[...].astype(BF16)
        wdb_ref[...] = wd_ref[...].astype(BF16)

    @pl.when(c < n_used)
    def _():
        wait(slot)
        x = _load_row_major(buf_ref.at[slot], MOE_TILE).astype(BF16)
        gate = jnp.dot(x, wgb_ref[...], preferred_element_type=F32)
        up = jnp.dot(x, wub_ref[...], preferred_element_type=F32)
        h = jax.nn.silu(gate) * up
        _store_row_major(ys_ref, jnp.dot(h.astype(BF16), wdb_ref[...], preferred_element_type=F32))

    @pl.when(c >= n_used)
    def _():
        ys_ref[...] = jnp.zeros(ys_ref.shape, ys_ref.dtype)


def _moe(x1, slot_row, tile_expert, n_used, w_gate, w_up, w_down, n_tiles):
    d = w_gate.shape[1]
    nc, pitch = d // LANES, _row_pitch(d)
    f = w_gate.shape[-1]
    grid_spec = pltpu.PrefetchScalarGridSpec(
        num_scalar_prefetch=3,
        grid=(n_tiles,),
        in_specs=[
            pl.BlockSpec(memory_space=pl.ANY),
            pl.BlockSpec((None, d, f), lambda c, te, nu, st: (te[c], 0, 0)),
            pl.BlockSpec((None, d, f), lambda c, te, nu, st: (te[c], 0, 0)),
            pl.BlockSpec((None, f, d), lambda c, te, nu, st: (te[c], 0, 0)),
        ],
        out_specs=pl.BlockSpec((MOE_TILE * pitch, LANES), lambda c, te, nu, st: (c, 0)),
        scratch_shapes=[
            pltpu.VMEM((2, MOE_TILE * pitch, LANES), F32),
            pltpu.VMEM((d, f), BF16),
            pltpu.VMEM((d, f), BF16),
            pltpu.VMEM((f, d), BF16),
            pltpu.SemaphoreType.DMA((2,)),
        ],
    )
    return pl.pallas_call(
        functools.partial(_moe_kernel, nc=nc, pitch=pitch),
        grid_spec=grid_spec,
        out_shape=jax.ShapeDtypeStruct((n_tiles * MOE_TILE * pitch, LANES), F32),
        compiler_params=_cparams(("arbitrary",)),
        name="moe",
    )(tile_expert, n_used, slot_row, x1, w_gate, w_up, w_down)


def _final_kernel(dest_ref, ys_ref, x1_ref, route_ref, p_ref, g2_ref, b2_ref, wpg_ref, wp_ref, o_ref,
                  buf_ref, sem_ref, *, tm, pair0, nc, pitch):
    i = pl.program_id(0)
    n_steps = pl.num_programs(0)

    def issue(step, slot):
        def body(r, carry):
            for k in range(2):
                src = dest_ref[pair0 + (step * tm + r) * 2 + k]
                pltpu.make_async_copy(ys_ref.at[pl.ds(src, nc), :], buf_ref.at[slot, k, pl.ds(r * pitch, nc), :],
                                      sem_ref.at[slot]).start()
            return carry

        lax.fori_loop(0, tm, body, 0, unroll=GATHER_UNROLL)

    def wait(slot):
        for k in range(2):
            pltpu.make_async_copy(ys_ref.at[pl.ds(0, tm * nc), :], buf_ref.at[slot, k, pl.ds(0, tm * nc), :],
                                  sem_ref.at[slot]).wait()

    slot = i % 2

    @pl.when(i == 0)
    def _():
        issue(0, 0)

    @pl.when(i + 1 < n_steps)
    def _():
        issue(i + 1, 1 - slot)

    wait(slot)
    route = route_ref[...]
    y = (route[:, 2:3] * _load_row_major(buf_ref.at[slot, 0], tm)
         + route[:, 3:4] * _load_row_major(buf_ref.at[slot, 1], tm))
    x2 = _layer_norm(ALPHA * _load_row_major(x1_ref, tm) + y, g2_ref[...], b2_ref[...])
    gate = jax.nn.sigmoid(jnp.dot(x2.astype(BF16), wpg_ref[...], preferred_element_type=F32))
    ple = jnp.dot(p_ref[...].astype(BF16), wp_ref[...], preferred_element_type=F32)
    o_ref[...] = x2 + gate * ple


def _final(dest, ys, x1, route, p, g2, b2, wpg, wp, *, tm, row0):
    m, pd = p.shape
    d = wpg.shape[0]
    nc, pitch = d // LANES, _row_pitch(d)
    assert m % tm == 0 and row0 % tm == 0
    pair0 = 2 * row0
    grid_spec = pltpu.PrefetchScalarGridSpec(
        num_scalar_prefetch=1,
        grid=(m // tm,),
        in_specs=[
            pl.BlockSpec(memory_space=pl.ANY),
            pl.BlockSpec((tm * pitch, LANES), lambda i, dst: (row0 // tm + i, 0)),
            pl.BlockSpec((tm, LANES), lambda i, dst: (i, 0)),
            pl.BlockSpec((tm, pd), lambda i, dst: (i, 0)),
            pl.BlockSpec((1, d), lambda i, dst: (0, 0)),
            pl.BlockSpec((1, d), lambda i, dst: (0, 0)),
            pl.BlockSpec((d, d), lambda i, dst: (0, 0)),
            pl.BlockSpec((pd, d), lambda i, dst: (0, 0)),
        ],
        out_specs=pl.BlockSpec((tm, d), lambda i, dst: (i, 0)),
        scratch_shapes=[pltpu.VMEM((2, 2, tm * pitch, LANES), F32), pltpu.SemaphoreType.DMA((2,))],
    )
    return pl.pallas_call(
        functools.partial(_final_kernel, tm=tm, pair0=pair0, nc=nc, pitch=pitch),
        grid_spec=grid_spec,
        out_shape=jax.ShapeDtypeStruct((m, d), F32),
        compiler_params=_cparams(("arbitrary",)),
        name="final",
    )(dest, ys, x1, route, p, g2, b2, wpg, wp)


def _pick_tile(m, pref):
    t = pref
    while m % t:
        t //= 2
    return t


def _layer(xp, xs, caches, state_pool, pp, ps, w):
    s, d = xp.shape
    db = xs.shape[0]

    cos_p, sin_p = _rope_tables(jnp.arange(s, dtype=jnp.int32))
    dils = tuple(dil for _, dil in DILATED_PATTERNS)
    tables = (jnp.stack([cos_p, jnp.ones_like(cos_p)]), jnp.stack([sin_p, jnp.zeros_like(sin_p)]))
    tm_in = _pick_tile(s, 512)
    qkv_p = [_proj_pattern(xp, w["w_in"], tables, pi, dil, tm=tm_in) for pi, dil in enumerate(dils)]
    u_p = _proj_u(xp, w["w_in"], tm=tm_in)
    os_, lses = [], []
    for pi, (window, dil) in enumerate(DILATED_PATTERNS):
        o, l = _attn_pattern(qkv_p[pi], pi, dil, window)
        os_.append(o)
        lses.append(l)
    pool_p = _pool_prompt(u_p, w["w_pool"], w["pool_scale"], tm=_pick_tile(s, 512))

    cos_s, sin_s = _rope_tables(jnp.full((db,), PAST_LEN, jnp.int32))
    qkv_s, u_s = _in_proj_rows(xs, w["w_in_f32"], cos_s, sin_s)
    zt_s = qkv_s.T
    os_s, lses_s, kv_s = [], [], []
    for pi, ((window, dil), c) in enumerate(zip(DILATED_PATTERNS, caches)):
        cn, ot, lt = _sample_attn(zt_s, jnp.transpose(c, (0, 2, 3, 4, 1)), pi, dil)
        kv_s.append(jnp.transpose(cn, (0, 4, 1, 2, 3)))
        os_s.append(jnp.transpose(ot.T[:db].reshape(db, ATTN_WIDTH // LANES, LANES), (1, 0, 2)))
        lses_s.append(lt.T[:db])
    pool_s = _sample_pool(u_s, state_pool, w["w_pool_f32"], w["pool_scale"])

    wr_hi, wr_lo = _split_bf16(w["w_router"])
    post_w = (w["ln1_g"], w["ln1_b"], wr_hi, wr_lo, w["b_router"])
    tm_p = _pick_tile(s, 256)
    assert s % db == 0 and tm_p >= db
    x1, route_p = _post(os_, lses, pool_p, xp, w["w_out"], *post_w, tm=tm_p, name="post_prompt",
                        filler_tiles=1)
    x1, route_s = _post(os_s, lses_s, pool_s, xs, w["w_out_f32"], *post_w, tm=db, name="post_sample",
                        x1_into=x1, row0=s)

    pair_expert = jnp.concatenate([route_p[:, 0:2].reshape(-1), route_s[:, 0:2].reshape(-1)]).astype(jnp.int32)
    n_pairs = pair_expert.shape[0]
    n_tiles = -(-n_pairs // MOE_TILE) + N_EXPERTS
    dest, slot_token, tile_expert, n_used = _routing_plan(pair_expert, n_tiles)
    pitch = _row_pitch(d)
    ys = _moe(x1, slot_token * pitch, tile_expert, n_used, w["w_gate"], w["w_up"], w["w_down"], n_tiles)

    fin_w = (w["ln2_g"], w["ln2_b"], w["w_ple_gate"], w["w_ple"])
    y_p = _final(dest * pitch, ys, x1, route_p, pp, *fin_w, tm=tm_p, row0=0)
    y_s = _final(dest * pitch, ys, x1, route_s, ps, *fin_w, tm=db, row0=s)

    kv_p = []
    for pi, (window, dil) in enumerate(DILATED_PATTERNS):
        keep = min(window, s)
        kv = qkv_p[pi][:, (s - keep) // dil:, ATTN_WIDTH:3 * ATTN_WIDTH].astype(F32)
        kv = jnp.transpose(kv, (1, 0, 2))
        kv_p.append(kv.reshape(1, keep, 2, N_HEADS, HEAD_DIM))
    pool_state_p = u_p[s - POOL_STATE:][None]
    pool_state_s = jnp.concatenate([state_pool[:, 1:], u_s[:, None, :]], axis=1)
    return y_p, y_s, kv_p, pool_state_p, kv_s, pool_state_s


def kernel(x_prompt, x_sample, cache_kv_w128_d1, cache_kv_w512_d4, cache_kv_w2048_d16, state_pool, p_prompt, p_sample, w_in, w_out, w_pool, pool_scale, ln1_g, ln1_b, w_group_router, b_group_router, w_expert_router, b_expert_router, w_gate, w_up, w_down, ln2_g, ln2_b, w_ple, w_ple_gate):
    assert w_in.shape[0] == DEPTH == 1 and x_prompt.shape[0] == 1 and x_sample.shape[1] == 1
    d = x_prompt.shape[-1]
    pad = LANES - N_GROUPS - N_EXPERTS
    w = {
        "w_in": w_in[0].astype(BF16), "w_in_f32": w_in[0],
        "w_out": w_out[0].astype(BF16), "w_out_f32": w_out[0],
        "w_pool": w_pool[0].astype(BF16), "w_pool_f32": w_pool[0],
        "pool_scale": pool_scale[0].reshape(1, POOL_WIDTH),
        "ln1_g": ln1_g[0].reshape(1, d), "ln1_b": ln1_b[0].reshape(1, d),
        "ln2_g": ln2_g[0].reshape(1, d), "ln2_b": ln2_b[0].reshape(1, d),
        "w_router": jnp.concatenate([w_group_router[0], w_expert_router[0], jnp.zeros((d, pad), F32)], axis=1),
        "b_router": jnp.concatenate([b_group_router[0], b_expert_router[0], jnp.zeros((pad,), F32)]).reshape(1, LANES),
        "w_gate": w_gate[0], "w_up": w_up[0], "w_down": w_down[0],
        "w_ple": w_ple[0].astype(BF16),
        "w_ple_gate": w_ple_gate[0].astype(BF16),
    }
    caches = [cache_kv_w128_d1[0], cache_kv_w512_d4[0], cache_kv_w2048_d16[0]]
    y_p, y_s, kv_p, pool_p, kv_s, pool_s = _layer(
        x_prompt[0], x_sample[:, 0], caches, state_pool[0], p_prompt[0, 0], p_sample[0, :, 0], w)
    return (y_p[None], y_s[:, None], kv_p[0][None], kv_p[1][None], kv_p[2][None], pool_p[None],
            kv_s[0][None], kv_s[1][None], kv_s[2][None], pool_s[None])
```

```python
import functools

import jax
import jax.numpy as jnp
from jax import lax
from jax.experimental import pallas as pl
from jax.experimental.pallas import tpu as pltpu

F32 = jnp.float32
BF16 = jnp.bfloat16

PAST_LEN = 8192
HEAD_DIM = 64
N_HEADS = 16
ATTN_WIDTH = N_HEADS * HEAD_DIM
DILATED_PATTERNS = ((128, 1), (512, 4), (2048, 16))
N_PATTERNS = len(DILATED_PATTERNS)
BAND = 128
POOL_WINDOWS = (2, 4, 8, 16)
POOL_GROUP = 256
POOL_WIDTH = POOL_GROUP * len(POOL_WINDOWS)
POOL_STATE = max(POOL_WINDOWS) - 1
ROT_DIM = HEAD_DIM // 4
ROPE_THETA = 500000.0
QKV_WIDTH = N_PATTERNS * 3 * ATTN_WIDTH
N_GROUPS = 4
EXPERTS_PER_GROUP = 8
N_EXPERTS = N_GROUPS * EXPERTS_PER_GROUP
DEPTH = 1
ALPHA = (2.0 * DEPTH) ** 0.25
LN_EPS = 1e-5
NEG_INF = -1e30

LANES = 128
SUBLANES = 8
VMEM_LIMIT = 48 * 1024 * 1024

MOE_TILE = 256
HALO = 16
GATHER_UNROLL = 8


def _cparams(semantics):
    return pltpu.CompilerParams(dimension_semantics=semantics, vmem_limit_bytes=VMEM_LIMIT)


def _rope_tables(pos):
    inv_freq = ROPE_THETA ** (-jnp.arange(0, ROT_DIM, 2, dtype=F32) / ROT_DIM)
    ang = pos.astype(F32)[:, None] * inv_freq[None, :]
    cos, sin = jnp.cos(ang), jnp.sin(ang)
    t = pos.shape[0]
    rest = HEAD_DIM - ROT_DIM
    c64 = jnp.concatenate([cos, cos, jnp.ones((t, rest), F32)], -1)
    s64 = jnp.concatenate([-sin, sin, jnp.zeros((t, rest), F32)], -1)
    return jnp.tile(c64, (1, LANES // HEAD_DIM)), jnp.tile(s64, (1, LANES // HEAD_DIM))


def _rope_chunk(xc, cos, sin):
    half = ROT_DIM // 2
    lane = lax.broadcasted_iota(jnp.int32, xc.shape, 1) & (HEAD_DIM - 1)
    upper = pltpu.roll(xc, LANES - half, 1)
    lower = pltpu.roll(xc, half, 1)
    partner = jnp.where(lane < half, upper, lower)
    return xc * cos + partner * sin


def _split_bf16(a):
    hi = a.astype(BF16)
    return hi, (a - hi.astype(F32)).astype(BF16)


def _mm(a, w):
    if w.dtype == BF16:
        return jnp.dot(a.astype(BF16), w, preferred_element_type=F32)
    ah, al = _split_bf16(a.astype(F32))
    wh, wl = _split_bf16(w)
    return (jnp.dot(ah, wh, preferred_element_type=F32) + jnp.dot(al, wh, preferred_element_type=F32)
            + jnp.dot(ah, wl, preferred_element_type=F32))


ROW_PAD = 1


def _row_pitch(d):
    return d // LANES + ROW_PAD


def _store_row_major(ref, val):
    n, d = val.shape
    nc, pitch = d // LANES, _row_pitch(d)
    for c in range(nc):
        ref[pl.ds(c, n, stride=pitch), :] = val[:, c * LANES:(c + 1) * LANES]
    for c in range(nc, pitch):
        ref[pl.ds(c, n, stride=pitch), :] = jnp.zeros((n, LANES), F32)


def _load_row_major(ref, n):
    pitch = ref.shape[0] // n
    return jnp.concatenate([ref[pl.ds(c, n, stride=pitch), :] for c in range(pitch - ROW_PAD)], axis=1)


ROLE_TILES = 3


def _chunk(c):
    return slice(c * LANES, (c + 1) * LANES)


def _in_proj_rows_kernel(x_ref, w_ref, cos_ref, sin_ref, qkv_ref, u_ref):
    j = pl.program_id(0)
    n_qkv_tiles = N_PATTERNS * ROLE_TILES
    acc = _mm(x_ref[...], w_ref[...])
    role = j % ROLE_TILES
    is_qkv = j < n_qkv_tiles

    @pl.when(is_qkv & (role < 2))
    def _():
        cos = cos_ref[...]
        sin = sin_ref[...]
        for c in range(acc.shape[1] // LANES):
            qkv_ref[:, _chunk(c)] = _rope_chunk(acc[:, _chunk(c)], cos, sin)

    @pl.when(is_qkv & (role == 2))
    def _():
        qkv_ref[...] = acc

    @pl.when(j >= n_qkv_tiles)
    def _():
        u_ref[...] = acc


def _in_proj_rows(x, w, cos_t, sin_t):
    m, d = x.shape
    tn = ATTN_WIDTH
    assert w.shape[1] == QKV_WIDTH + POOL_WIDTH and POOL_WIDTH == tn
    n_qkv_tiles = N_PATTERNS * ROLE_TILES
    return pl.pallas_call(
        _in_proj_rows_kernel,
        grid=(n_qkv_tiles + 1,),
        in_specs=[
            pl.BlockSpec((m, d), lambda j: (0, 0)),
            pl.BlockSpec((d, tn), lambda j: (0, j)),
            pl.BlockSpec((m, LANES), lambda j: (0, 0)),
            pl.BlockSpec((m, LANES), lambda j: (0, 0)),
        ],
        out_specs=[pl.BlockSpec((m, tn), lambda j: (0, jnp.minimum(j, n_qkv_tiles - 1))),
                   pl.BlockSpec((m, tn), lambda j: (0, 0))],
        out_shape=[jax.ShapeDtypeStruct((m, QKV_WIDTH), F32), jax.ShapeDtypeStruct((m, POOL_WIDTH), F32)],
        compiler_params=_cparams(("arbitrary",)),
        name="in_proj_rows",
    )(x, w, cos_t, sin_t)


def _proj_pattern_kernel(x_ref, w_ref, cos_ref, sin_ref, out_ref, wb_ref, acc_ref, stage_ref, *, dil, n_rows):
    t = pl.program_id(0)
    n_chunks, tm, _ = stage_ref.shape

    @pl.when(t == 0)
    def _():
        acc_ref[...] = jnp.zeros(acc_ref.shape, F32)

    @pl.when(t % n_rows == 0)
    def _():
        wb_ref[...] = w_ref[...].astype(BF16)

    prev = acc_ref.at[(t + 1) % 2]
    cos = cos_ref[...]
    sin = sin_ref[...]
    for c in range(n_chunks):
        stage_ref[c] = _rope_chunk(prev[:, _chunk(c)], cos, sin)
    for r in range(dil):
        for c in range(n_chunks):
            rows = stage_ref[c, pl.ds(r, tm // dil, stride=dil), :] if dil > 1 else stage_ref[c]
            out_ref[r, :, _chunk(c)] = rows.astype(out_ref.dtype)
    acc_ref[t % 2] = jnp.dot(x_ref[...], wb_ref[...], preferred_element_type=F32)


def _proj_pattern(xb, w, tables, pi, dil, *, tm):
    m, d = xb.shape
    tn = ATTN_WIDTH
    assert m % tm == 0 and tm % (dil * 2 * SUBLANES) == 0
    n_rows = m // tm
    n = n_rows * ROLE_TILES

    def cur(t):
        return jnp.minimum(t, n - 1)

    def prv(t):
        return jnp.maximum(t - 1, 0)

    def table_map(t):
        identity = (prv(t) // n_rows == ROLE_TILES - 1).astype(jnp.int32)
        return (identity, prv(t) % n_rows, 0)

    return pl.pallas_call(
        functools.partial(_proj_pattern_kernel, dil=dil, n_rows=n_rows),
        grid=(n + 1,),
        in_specs=[
            pl.BlockSpec((tm, d), lambda t: (cur(t) % n_rows, 0)),
            pl.BlockSpec((d, tn), lambda t: (0, pi * ROLE_TILES + cur(t) // n_rows)),
            pl.BlockSpec((None, tm, LANES), table_map),
            pl.BlockSpec((None, tm, LANES), table_map),
        ],
        out_specs=pl.BlockSpec((dil, tm // dil, tn), lambda t: (0, prv(t) % n_rows, prv(t) // n_rows)),
        out_shape=jax.ShapeDtypeStruct((dil, m // dil, ROLE_TILES * tn), BF16),
        scratch_shapes=[pltpu.VMEM((d, tn), BF16), pltpu.VMEM((2, tm, tn), F32),
                        pltpu.VMEM((tn // LANES, tm, LANES), F32)],
        compiler_params=_cparams(("arbitrary",)),
        name=f"proj_p{pi}",
    )(xb, w, *tables)


def _proj_u_kernel(x_ref, w_ref, u_ref, wb_ref):
    @pl.when(pl.program_id(0) == 0)
    def _():
        wb_ref[...] = w_ref[...].astype(BF16)

    u_ref[...] = jnp.dot(x_ref[...], wb_ref[...], preferred_element_type=F32)


def _proj_u(xb, w, *, tm):
    m, d = xb.shape
    n_qkv_tiles = N_PATTERNS * ROLE_TILES
    return pl.pallas_call(
        _proj_u_kernel,
        grid=(m // tm,),
        in_specs=[pl.BlockSpec((tm, d), lambda i: (i, 0)),
                  pl.BlockSpec((d, POOL_WIDTH), lambda i: (0, n_qkv_tiles), pipeline_mode=pl.Buffered(1))],
        out_specs=pl.BlockSpec((tm, POOL_WIDTH), lambda i: (i, 0)),
        out_shape=jax.ShapeDtypeStruct((m, POOL_WIDTH), F32),
        scratch_shapes=[pltpu.VMEM((d, POOL_WIDTH), BF16)],
        compiler_params=_cparams(("arbitrary",)),
        name="proj_u",
    )(xb, w)


def _attn_kernel(q_ref, kp_ref, kc_ref, vp_ref, vc_ref, o_ref, lse_ref,
                 lim_ref, s_ref, e_ref, m_ref, den_ref, *, r_max, dil):
    n = pl.program_id(0)
    r = pl.program_id(1)
    qi = lax.broadcasted_iota(jnp.int32, (BAND, 2 * BAND), 0)
    ki = lax.broadcasted_iota(jnp.int32, (BAND, 2 * BAND), 1)
    dist = BAND + qi - ki
    valid = (dist >= 0) & (dist <= r_max) & ((n > 0) | (ki >= BAND))
    lim_ref[...] = jnp.where(valid, jnp.float32(jnp.finfo(F32).max), jnp.float32(NEG_INF))

    lane = lax.broadcasted_iota(jnp.int32, (BAND, LANES), 1)
    heads_per_chunk = LANES // HEAD_DIM
    n_chunks = ATTN_WIDTH // LANES
    lane_head = lane >> (HEAD_DIM.bit_length() - 1)
    qscale = [jnp.where(lane_head == hh, HEAD_DIM ** -0.5, 0.0).astype(BF16) for hh in range(heads_per_chunk)]
    nt = (((1,), (1,)), ((), ()))

    for c in range(n_chunks):
        sl = slice(c * LANES, (c + 1) * LANES)
        q2 = q_ref[:, sl]
        kp, kc = kp_ref[:, sl], kc_ref[:, sl]
        for hh in range(heads_per_chunk):
            h = c * heads_per_chunk + hh
            qm = q2 * qscale[hh]
            s = jnp.concatenate(
                [lax.dot_general(qm, kp, nt, preferred_element_type=F32),
                 lax.dot_general(qm, kc, nt, preferred_element_type=F32)], axis=1)
            s = jnp.minimum(s, lim_ref[...])
            s_ref[h] = s
            m_ref[h] = jnp.broadcast_to(jnp.max(s, axis=-1, keepdims=True), (BAND, LANES))

    lse_acc = jnp.zeros((BAND, LANES), F32)
    for h in range(N_HEADS):
        m = m_ref[h]
        e = jnp.exp(s_ref[h] - jnp.concatenate([m, m], axis=1))
        e_ref[h] = e.astype(BF16)
        den = jnp.broadcast_to(jnp.sum(e, axis=-1, keepdims=True), (BAND, LANES))
        den_ref[h] = den
        lse_acc = jnp.where(lane == h, m + jnp.log(den), lse_acc)

    rows = slice(None) if dil == 1 else pl.ds(r, BAND, stride=dil)
    lse_ref[rows, :] = lse_acc
    for c in range(n_chunks):
        sl = slice(c * LANES, (c + 1) * LANES)
        vp, vc = vp_ref[:, sl], vc_ref[:, sl]
        o2 = jnp.zeros((BAND, LANES), F32)
        for hh in range(heads_per_chunk):
            h = c * heads_per_chunk + hh
            pv = (jnp.dot(e_ref[h, :, :BAND], vp, preferred_element_type=F32)
                  + jnp.dot(e_ref[h, :, BAND:], vc, preferred_element_type=F32))
            o2 = jnp.where(lane_head == hh, pv / den_ref[h], o2)
        o_ref[c, rows, :] = o2


def _attn_pattern(qkv, pi, dil, window):
    s = qkv.shape[0] * qkv.shape[1]
    assert qkv.shape[0] == dil and s % (dil * BAND) == 0
    nb = s // (dil * BAND)
    r_max = window // dil

    def cur(role):
        return lambda n, r: (r, n, role)

    def prev(role):
        return lambda n, r: (r, jnp.maximum(n - 1, 0), role)

    blk = (None, BAND, ATTN_WIDTH)
    span = BAND * dil
    return pl.pallas_call(
        functools.partial(_attn_kernel, r_max=r_max, dil=dil),
        grid=(nb, dil),
        in_specs=[
            pl.BlockSpec(blk, cur(0)),
            pl.BlockSpec(blk, prev(1)),
            pl.BlockSpec(blk, cur(1)),
            pl.BlockSpec(blk, prev(2)),
            pl.BlockSpec(blk, cur(2)),
        ],
        out_specs=[
            pl.BlockSpec((ATTN_WIDTH // LANES, span, LANES), lambda n, r: (0, n, 0)),
            pl.BlockSpec((span, LANES), lambda n, r: (n, 0)),
        ],
        out_shape=[
            jax.ShapeDtypeStruct((ATTN_WIDTH // LANES, s, LANES), F32),
            jax.ShapeDtypeStruct((s, LANES), F32),
        ],
        scratch_shapes=[
            pltpu.VMEM((BAND, 2 * BAND), F32),
            pltpu.VMEM((N_HEADS, BAND, 2 * BAND), F32),
            pltpu.VMEM((N_HEADS, BAND, 2 * BAND), BF16),
            pltpu.VMEM((N_HEADS, BAND, LANES), F32),
            pltpu.VMEM((N_HEADS, BAND, LANES), F32),
        ],
        compiler_params=_cparams(("arbitrary", "arbitrary")),
        name=f"attn_p{pi}",
    )(qkv, qkv, qkv, qkv, qkv)


def _pool_groups(buf_ref, tm, pos0, w_pool_ref, scale_ref, store):
    row = lax.broadcasted_iota(jnp.int32, (tm, 1), 0) + pos0
    for g, win in enumerate(POOL_WINDOWS):
        sl = slice(g * POOL_GROUP, (g + 1) * POOL_GROUP)
        cur = buf_ref[HALO:HALO + tm, sl]
        acc = cur
        for jj in range(1, win):
            acc = acc + buf_ref[HALO - jj:HALO - jj + tm, sl]
        cnt = jnp.minimum(row + 1, win).astype(F32)
        d = acc / cnt - cur
        y = _mm(d, w_pool_ref[g])
        store(g, y * scale_ref[:, sl])


def _pool_kernel(u_ref, uprev_ref, w_pool_ref, scale_ref, o_ref, buf_ref, *, tm):
    i = pl.program_id(0)
    buf_ref[0:HALO, :] = jnp.where(i > 0, uprev_ref[...], jnp.zeros_like(uprev_ref))
    buf_ref[HALO:HALO + tm, :] = u_ref[...]

    def store(g, val):
        o_ref[:, g * POOL_GROUP:(g + 1) * POOL_GROUP] = val.astype(o_ref.dtype)

    _pool_groups(buf_ref, tm, i * tm, w_pool_ref, scale_ref, store)


def _pool_prompt(u, w_pool_bf16, pool_scale, *, tm):
    s = u.shape[0]
    assert s % tm == 0 and tm % HALO == 0
    per = tm // HALO
    return pl.pallas_call(
        functools.partial(_pool_kernel, tm=tm),
        grid=(s // tm,),
        in_specs=[
            pl.BlockSpec((tm, POOL_WIDTH), lambda i: (i, 0)),
            pl.BlockSpec((HALO, POOL_WIDTH), lambda i: (jnp.maximum(i * per - 1, 0), 0)),
            pl.BlockSpec((len(POOL_WINDOWS), POOL_GROUP, POOL_GROUP), lambda i: (0, 0, 0)),
            pl.BlockSpec((1, POOL_WIDTH), lambda i: (0, 0)),
        ],
        out_specs=pl.BlockSpec((tm, POOL_WIDTH), lambda i: (i, 0)),
        out_shape=jax.ShapeDtypeStruct((s, POOL_WIDTH), BF16),
        scratch_shapes=[pltpu.VMEM((HALO + tm, POOL_WIDTH), F32)],
        compiler_params=_cparams(("arbitrary",)),
        name="pool",
    )(u, u, w_pool_bf16, pool_scale)


def _sample_attn_kernel(zt_ref, c_ref, cn_ref, ot_ref, lt_ref, s_ref, *, dil, hps, length):
    b = pl.program_id(0)
    g = pl.program_id(1)

    @pl.when((b == 0) & (g == 0))
    def _():
        ot_ref[...] = jnp.zeros(ot_ref.shape, F32)
        lt_ref[...] = jnp.zeros(lt_ref.shape, F32)

    rows = hps * HEAD_DIM
    base = pl.multiple_of(g * rows, rows)
    lane_z = lax.broadcasted_iota(jnp.int32, (rows, zt_ref.shape[1]), 1)

    def column(role):
        blk = zt_ref[pl.ds(role * ATTN_WIDTH + base, rows), :]
        return jnp.sum(jnp.where(lane_z == b, blk, 0.0), axis=-1, keepdims=True)

    qc, knc, vnc = column(0), column(1), column(2)
    scale = HEAD_DIM ** -0.5
    lane = lax.broadcasted_iota(jnp.int32, (1, length), 1)
    key_ok = (lane & (dil - 1)) == 0
    last = lane == length - 1
    lane_o = lax.broadcasted_iota(jnp.int32, (HEAD_DIM, LANES), 1)
    lane_l = lax.broadcasted_iota(jnp.int32, (hps, LANES), 1)

    def head_rows(h):
        return slice(h * HEAD_DIM, (h + 1) * HEAD_DIM)

    for h in range(hps):
        k = c_ref[0, h]
        s_ref[h:h + 1, :] = jnp.sum(k * qc[head_rows(h)], axis=0, keepdims=True)
        cn_ref[0, h] = jnp.where(last, knc[head_rows(h)], pltpu.roll(k, length - 1, 1))

    s = jnp.where(key_ok, s_ref[...] * scale, NEG_INF)
    s_n = jnp.sum((knc * qc).reshape(hps, HEAD_DIM, 1), axis=1) * scale
    m = jnp.maximum(jnp.max(s, axis=-1, keepdims=True), s_n)
    e = jnp.exp(s - m)
    e_n = jnp.exp(s_n - m)
    den = jnp.sum(e, axis=-1, keepdims=True) + e_n
    s_ref[...] = e
    lse = m + jnp.log(den)
    head0 = pl.multiple_of(g * hps, hps)
    lt_ref[pl.ds(head0, hps), :] = jnp.where(lane_l == b, lse, lt_ref[pl.ds(head0, hps), :])

    for h in range(hps):
        v = c_ref[1, h]
        vn = vnc[head_rows(h)]
        o = (jnp.sum(v * s_ref[h:h + 1, :], axis=-1, keepdims=True) + vn * e_n[h:h + 1, :]) / den[h:h + 1, :]
        cn_ref[1, h] = jnp.where(last, vn, pltpu.roll(v, length - 1, 1))
        r0 = pl.multiple_of(base + h * HEAD_DIM, HEAD_DIM)
        ot_ref[pl.ds(r0, HEAD_DIM), :] = jnp.where(lane_o == b, o, ot_ref[pl.ds(r0, HEAD_DIM), :])


def _sample_attn(zt, c_t, pi, dil):
    db, _, _, _, length = c_t.shape
    assert db <= LANES and length == dil * BAND
    hps = min(N_HEADS, max(1, (4 * 2048) // length))
    blk = (None, 2, hps, HEAD_DIM, length)
    return pl.pallas_call(
        functools.partial(_sample_attn_kernel, dil=dil, hps=hps, length=length),
        grid=(db, N_HEADS // hps),
        in_specs=[
            pl.BlockSpec((3 * ATTN_WIDTH, db), lambda b, g: (pi, 0)),
            pl.BlockSpec(blk, lambda b, g: (b, 0, g, 0, 0)),
        ],
        out_specs=[
            pl.BlockSpec(blk, lambda b, g: (b, 0, g, 0, 0)),
            pl.BlockSpec((ATTN_WIDTH, LANES), lambda b, g: (0, 0)),
            pl.BlockSpec((LANES, LANES), lambda b, g: (0, 0)),
        ],
        out_shape=[
            jax.ShapeDtypeStruct(c_t.shape, c_t.dtype),
            jax.ShapeDtypeStruct((ATTN_WIDTH, LANES), F32),
            jax.ShapeDtypeStruct((LANES, LANES), F32),
        ],
        scratch_shapes=[pltpu.VMEM((hps, length), F32)],
        compiler_params=_cparams(("arbitrary", "arbitrary")),
        name=f"sample_attn_p{pi}",
    )(zt, c_t)


def _sample_pool_kernel(u_ref, sp_ref, w_pool_ref, scale_ref, pool_ref, buf_ref):
    buf_ref[...] = jnp.zeros(buf_ref.shape, F32)
    buf_ref[HALO - POOL_STATE:HALO, :] = sp_ref[...]
    buf_ref[HALO:HALO + 1, :] = u_ref[...]

    def store(g, val):
        pool_ref[:, g * POOL_GROUP:(g + 1) * POOL_GROUP] = val[0:1]

    _pool_groups(buf_ref, HALO, PAST_LEN, w_pool_ref, scale_ref, store)


def _head_expand():
    dim_head = jnp.arange(ATTN_WIDTH, dtype=jnp.int32) // HEAD_DIM
    return (jnp.arange(LANES, dtype=jnp.int32)[:, None] == dim_head[None, :]).astype(F32)


def _sample_pool(us, state_pool, w_pool_bf16, pool_scale):
    db = us.shape[0]
    pool = pl.pallas_call(
        _sample_pool_kernel,
        grid=(db,),
        in_specs=[
            pl.BlockSpec((None, 1, POOL_WIDTH), lambda b: (b, 0, 0)),
            pl.BlockSpec((None, POOL_STATE, POOL_WIDTH), lambda b: (b, 0, 0)),
            pl.BlockSpec((len(POOL_WINDOWS), POOL_GROUP, POOL_GROUP), lambda b: (0, 0, 0)),
            pl.BlockSpec((1, POOL_WIDTH), lambda b: (0, 0)),
        ],
        out_specs=pl.BlockSpec((None, 1, POOL_WIDTH), lambda b: (b, 0, 0)),
        out_shape=jax.ShapeDtypeStruct((db, 1, POOL_WIDTH), F32),
        scratch_shapes=[pltpu.VMEM((2 * HALO, POOL_WIDTH), F32)],
        compiler_params=_cparams(("arbitrary",)),
        name="sample_pool",
    )(us.reshape(db, 1, POOL_WIDTH), state_pool, w_pool_bf16, pool_scale)
    return pool.reshape(db, POOL_WIDTH)


def _layer_norm(xf, g, b):
    mu = jnp.mean(xf, axis=-1, keepdims=True)
    xc = xf - mu
    var = jnp.mean(xc * xc, axis=-1, keepdims=True)
    return xc * lax.rsqrt(var + LN_EPS) * g + b


def _route(logits):
    lane = lax.broadcasted_iota(jnp.int32, logits.shape, 1)
    lane_f = lane.astype(F32)
    neg = jnp.float32(-jnp.inf)

    def first_argmax(v, vmax):
        return jnp.min(jnp.where(v == vmax, lane_f, jnp.float32(LANES)), axis=-1, keepdims=True)

    gl = jnp.where(lane < N_GROUPS, logits, neg)
    gmax = jnp.max(gl, axis=-1, keepdims=True)
    g_sel = first_argmax(gl, gmax)
    g_gate = 1.0 / jnp.sum(jnp.exp(gl - gmax), axis=-1, keepdims=True)
    lane_group = (((lane + (EXPERTS_PER_GROUP - N_GROUPS)) >> 3) - 1).astype(F32)
    el = jnp.where(lane_group == g_sel, logits, neg)
    v1 = jnp.max(el, axis=-1, keepdims=True)
    i1 = first_argmax(el, v1)
    el2 = jnp.where(lane_f == i1, neg, el)
    v2 = jnp.max(el2, axis=-1, keepdims=True)
    i2 = first_argmax(el2, v2)
    t = jnp.exp(v2 - v1)
    w1 = (1.0 / (1.0 + t)) * g_gate
    w2 = (t / (1.0 + t)) * g_gate
    route = jnp.where(lane == 0, i1 - N_GROUPS, 0.0)
    route = jnp.where(lane == 1, i2 - N_GROUPS, route)
    route = jnp.where(lane == 2, w1, route)
    route = jnp.where(lane == 3, w2, route)
    return route


def _post_kernel(o0_ref, o1_ref, o2_ref, l0_ref, l1_ref, l2_ref, pool_ref, x_ref, hexp_ref,
                 wo_ref, g1_ref, b1_ref, wrh_ref, wrl_ref, br_ref, *rest, n_row_tiles):
    x1_ref, route_ref = rest[-2:]
    i = pl.program_id(0)

    @pl.when(i < n_row_tiles)
    def _():
        l0, l1, l2 = l0_ref[...], l1_ref[...], l2_ref[...]
        lmax = jnp.maximum(jnp.maximum(l0, l1), l2)
        e0, e1, e2 = jnp.exp(l0 - lmax), jnp.exp(l1 - lmax), jnp.exp(l2 - lmax)
        den = e0 + e1 + e2
        hexp = hexp_ref[...]
        attn = jnp.zeros((x_ref.shape[0], ATTN_WIDTH), F32)
        for o_ref, e in ((o0_ref, e0), (o1_ref, e1), (o2_ref, e2)):
            wh, wl = _split_bf16(e / den)
            w_x = (jnp.dot(wh, hexp, preferred_element_type=F32) + jnp.dot(wl, hexp, preferred_element_type=F32))
            o = jnp.concatenate([o_ref[c] for c in range(ATTN_WIDTH // LANES)], axis=1)
            attn = attn + o * w_x
        h = _mm(attn, wo_ref[0:ATTN_WIDTH, :]) + _mm(pool_ref[...], wo_ref[ATTN_WIDTH:, :])
        x1 = _layer_norm(ALPHA * x_ref[...] + h, g1_ref[...], b1_ref[...])
        _store_row_major(x1_ref, x1)
        xh, xl = _split_bf16(x1)
        wrh = wrh_ref[...]
        logits = (jnp.dot(xh, wrh, preferred_element_type=F32) + jnp.dot(xl, wrh, preferred_element_type=F32)
                  + jnp.dot(xh, wrl_ref[...], preferred_element_type=F32)) + br_ref[...]
        route_ref[...] = _route(logits)

    @pl.when(i >= n_row_tiles)
    def _():
        x1_ref[...] = jnp.zeros(x1_ref.shape, F32)


def _row_spec(tm, width):
    return pl.BlockSpec((tm, width), lambda i: (i, 0))


def _const_spec(shape):
    return pl.BlockSpec(shape, lambda i: (0,) * len(shape), pipeline_mode=pl.Buffered(1))


def _post(os_, lses, pool, x, wo, g1, b1, wrh, wrl, br, *, tm, name, filler_tiles=0, x1_into=None, row0=0):
    m, d = x.shape
    n = m // tm
    pitch = _row_pitch(d)
    assert (tm * pitch) % SUBLANES == 0
    hexp = _head_expand().astype(BF16)
    last = n - 1

    def rows(width):
        return pl.BlockSpec((tm, width), lambda i: (jnp.minimum(i, last), 0))

    in_specs = ([pl.BlockSpec((ATTN_WIDTH // LANES, tm, LANES), lambda i: (0, jnp.minimum(i, last), 0))] * 3
                + [rows(LANES)] * 3
                + [rows(POOL_WIDTH), rows(d), _const_spec((LANES, ATTN_WIDTH)),
                   _const_spec((d, d)), _const_spec((1, d)), _const_spec((1, d)),
                   _const_spec((d, LANES)), _const_spec((d, LANES)), _const_spec((1, LANES))])
    operands = [*os_, *lses, pool, x, hexp, wo, g1, b1, wrh, wrl, br]
    aliases = {}
    if x1_into is None:
        x1_shape = jax.ShapeDtypeStruct(((m + filler_tiles * tm) * pitch, LANES), F32)
    else:
        assert filler_tiles == 0 and row0 % tm == 0
        in_specs.append(pl.BlockSpec(memory_space=pl.ANY))
        operands.append(x1_into)
        aliases = {len(operands) - 1: 0}
        x1_shape = jax.ShapeDtypeStruct(x1_into.shape, F32)
    return pl.pallas_call(
        functools.partial(_post_kernel, n_row_tiles=n),
        grid=(n + filler_tiles,),
        in_specs=in_specs,
        out_specs=[pl.BlockSpec((tm * pitch, LANES), lambda i: (row0 // tm + i, 0)), rows(LANES)],
        out_shape=[x1_shape, jax.ShapeDtypeStruct((m, LANES), F32)],
        input_output_aliases=aliases,
        compiler_params=_cparams(("arbitrary",)),
        name=name,
    )(*operands)


def _routing_plan(pair_expert, n_tiles):
    p = pair_expert.shape[0]
    experts = jnp.arange(N_EXPERTS, dtype=jnp.int32)
    onehot = (pair_expert[:, None] == experts[None, :]).astype(jnp.int32)
    csum = jnp.cumsum(onehot, axis=0)
    rank = jnp.take_along_axis(csum, pair_expert[:, None], axis=1)[:, 0] - 1
    counts = csum[-1]
    tiles_per = (counts + MOE_TILE - 1) // MOE_TILE
    tile_end = jnp.cumsum(tiles_per)
    tile_start = tile_end - tiles_per
    dest = (tile_start[pair_expert] * MOE_TILE + rank).astype(jnp.int32)
    n_used = tile_end[-1]
    tile_ids = jnp.arange(n_tiles, dtype=jnp.int32)
    tile_expert = jnp.sum((tile_ids[:, None] >= tile_end[None, :]).astype(jnp.int32), axis=1)
    last_expert = jnp.sum((n_used - 1 >= tile_end).astype(jnp.int32))
    tile_expert = jnp.where(tile_ids < n_used, tile_expert, last_expert).astype(jnp.int32)
    slot_token = jnp.zeros((n_tiles * MOE_TILE,), jnp.int32).at[dest].set(
        jnp.arange(p, dtype=jnp.int32) // 2, unique_indices=True, mode="promise_in_bounds")
    return dest, slot_token, tile_expert, n_used.reshape(1).astype(jnp.int32)


def _moe_kernel(te_ref, nu_ref, st_ref, x_ref, wg_ref, wu_ref, wd_ref, ys_ref,
                buf_ref, wgb_ref, wub_ref, wdb_ref, sem_ref, *, nc, pitch):
    c = pl.program_id(0)
    n_tiles = pl.num_programs(0)
    n_used = nu_ref[0]

    def issue(tile, slot):
        def body(g, carry):
            for k in range(GATHER_UNROLL):
                i = g * GATHER_UNROLL + k
                src = st_ref[tile * MOE_TILE + i]
                pltpu.make_async_copy(x_ref.at[pl.ds(src, nc), :], buf_ref.at[slot, pl.ds(i * pitch, nc), :],
                                      sem_ref.at[slot]).start(priority=k % 2)
            return carry

        lax.fori_loop(0, MOE_TILE // GATHER_UNROLL, body, 0)

    def wait(slot):
        pltpu.make_async_copy(x_ref.at[pl.ds(0, MOE_TILE * nc), :], buf_ref.at[slot, pl.ds(0, MOE_TILE * nc), :],
                              sem_ref.at[slot]).wait()

    slot = c % 2

    @pl.when((c == 0) & (n_used > 0))
    def _():
        issue(0, 0)

    @pl.when((c + 1 < n_tiles) & (c + 1 < n_used))
    def _():
        issue(c + 1, 1 - slot)

    prev_expert = te_ref[jnp.maximum(c - 1, 0)]

    @pl.when((c == 0) | (te_ref[c] != prev_expert))
    def _():
        wgb_ref[...] = wg_ref[...].astype(BF16)
        wub_ref[...] = wu_ref[...].astype(BF16)
        wdb_ref[...] = wd_ref[...].astype(BF16)

    @pl.when(c < n_used)
    def _():
        wait(slot)
        x = _load_row_major(buf_ref.at[slot], MOE_TILE).astype(BF16)
        gate = jnp.dot(x, wgb_ref[...], preferred_element_type=F32)
        up = jnp.dot(x, wub_ref[...], preferred_element_type=F32)
        h = jax.nn.silu(gate) * up
        _store_row_major(ys_ref, jnp.dot(h.astype(BF16), wdb_ref[...], preferred_element_type=F32))

    @pl.when(c >= n_used)
    def _():
        ys_ref[...] = jnp.zeros(ys_ref.shape, ys_ref.dtype)


def _moe(x1, slot_row, tile_expert, n_used, w_gate, w_up, w_down, n_tiles):
    d = w_gate.shape[1]
    nc, pitch = d // LANES, _row_pitch(d)
    f = w_gate.shape[-1]
    grid_spec = pltpu.PrefetchScalarGridSpec(
        num_scalar_prefetch=3,
        grid=(n_tiles,),
        in_specs=[
            pl.BlockSpec(memory_space=pl.ANY),
            pl.BlockSpec((None, d, f), lambda c, te, nu, st: (te[c], 0, 0)),
            pl.BlockSpec((None, d, f), lambda c, te, nu, st: (te[c], 0, 0)),
            pl.BlockSpec((None, f, d), lambda c, te, nu, st: (te[c], 0, 0)),
        ],
        out_specs=pl.BlockSpec((MOE_TILE * pitch, LANES), lambda c, te, nu, st: (c, 0)),
        scratch_shapes=[
            pltpu.VMEM((2, MOE_TILE * pitch, LANES), F32),
            pltpu.VMEM((d, f), BF16),
            pltpu.VMEM((d, f), BF16),
            pltpu.VMEM((f, d), BF16),
            pltpu.SemaphoreType.DMA((2,)),
        ],
    )
    return pl.pallas_call(
        functools.partial(_moe_kernel, nc=nc, pitch=pitch),
        grid_spec=grid_spec,
        out_shape=jax.ShapeDtypeStruct((n_tiles * MOE_TILE * pitch, LANES), F32),
        compiler_params=_cparams(("arbitrary",)),
        name="moe",
    )(tile_expert, n_used, slot_row, x1, w_gate, w_up, w_down)


def _final_kernel(dest_ref, ys_ref, x1_ref, route_ref, p_ref, g2_ref, b2_ref, wpg_ref, wp_ref, o_ref,
                  buf_ref, sem_ref, *, tm, pair0, nc, pitch):
    i = pl.program_id(0)
    n_steps = pl.num_programs(0)

    def issue(step, slot):
        def body(r, carry):
            for k in range(2):
                src = dest_ref[pair0 + (step * tm + r) * 2 + k]
                pltpu.make_async_copy(ys_ref.at[pl.ds(src, nc), :], buf_ref.at[slot, k, pl.ds(r * pitch, nc), :],
                                      sem_ref.at[slot]).start()
            return carry

        lax.fori_loop(0, tm, body, 0, unroll=GATHER_UNROLL)

    def wait(slot):
        for k in range(2):
            pltpu.make_async_copy(ys_ref.at[pl.ds(0, tm * nc), :], buf_ref.at[slot, k, pl.ds(0, tm * nc), :],
                                  sem_ref.at[slot]).wait()

    slot = i % 2

    @pl.when(i == 0)
    def _():
        issue(0, 0)

    @pl.when(i + 1 < n_steps)
    def _():
        issue(i + 1, 1 - slot)

    wait(slot)
    route = route_ref[...]
    y = (route[:, 2:3] * _load_row_major(buf_ref.at[slot, 0], tm)
         + route[:, 3:4] * _load_row_major(buf_ref.at[slot, 1], tm))
    x2 = _layer_norm(ALPHA * _load_row_major(x1_ref, tm) + y, g2_ref[...], b2_ref[...])
    gate = jax.nn.sigmoid(jnp.dot(x2.astype(BF16), wpg_ref[...], preferred_element_type=F32))
    ple = jnp.dot(p_ref[...].astype(BF16), wp_ref[...], preferred_element_type=F32)
    o_ref[...] = x2 + gate * ple


def _final(dest, ys, x1, route, p, g2, b2, wpg, wp, *, tm, row0):
    m, pd = p.shape
    d = wpg.shape[0]
    nc, pitch = d // LANES, _row_pitch(d)
    assert m % tm == 0 and row0 % tm == 0
    pair0 = 2 * row0
    grid_spec = pltpu.PrefetchScalarGridSpec(
        num_scalar_prefetch=1,
        grid=(m // tm,),
        in_specs=[
            pl.BlockSpec(memory_space=pl.ANY),
            pl.BlockSpec((tm * pitch, LANES), lambda i, dst: (row0 // tm + i, 0)),
            pl.BlockSpec((tm, LANES), lambda i, dst: (i, 0)),
            pl.BlockSpec((tm, pd), lambda i, dst: (i, 0)),
            pl.BlockSpec((1, d), lambda i, dst: (0, 0)),
            pl.BlockSpec((1, d), lambda i, dst: (0, 0)),
            pl.BlockSpec((d, d), lambda i, dst: (0, 0)),
            pl.BlockSpec((pd, d), lambda i, dst: (0, 0)),
        ],
        out_specs=pl.BlockSpec((tm, d), lambda i, dst: (i, 0)),
        scratch_shapes=[pltpu.VMEM((2, 2, tm * pitch, LANES), F32), pltpu.SemaphoreType.DMA((2,))],
    )
    return pl.pallas_call(
        functools.partial(_final_kernel, tm=tm, pair0=pair0, nc=nc, pitch=pitch),
        grid_spec=grid_spec,
        out_shape=jax.ShapeDtypeStruct((m, d), F32),
        compiler_params=_cparams(("arbitrary",)),
        name="final",
    )(dest, ys, x1, route, p, g2, b2, wpg, wp)


def _pick_tile(m, pref):
    t = pref
    while m % t:
        t //= 2
    return t


def _layer(xp, xs, caches, state_pool, pp, ps, w):
    s, d = xp.shape
    db = xs.shape[0]

    cos_p, sin_p = _rope_tables(jnp.arange(s, dtype=jnp.int32))
    dils = tuple(dil for _, dil in DILATED_PATTERNS)
    tables = (jnp.stack([cos_p, jnp.ones_like(cos_p)]), jnp.stack([sin_p, jnp.zeros_like(sin_p)]))
    tm_in = _pick_tile(s, 512)
    xp_b = xp.astype(BF16)
    qkv_p = [_proj_pattern(xp_b, w["w_in_f32"], tables, pi, dil, tm=tm_in) for pi, dil in enumerate(dils)]
    u_p = _proj_u(xp_b, w["w_in_f32"], tm=tm_in)
    os_, lses = [], []
    for pi, (window, dil) in enumerate(DILATED_PATTERNS):
        o, l = _attn_pattern(qkv_p[pi], pi, dil, window)
        os_.append(o)
        lses.append(l)
    pool_p = _pool_prompt(u_p, w["w_pool"], w["pool_scale"], tm=_pick_tile(s, 512))

    cos_s, sin_s = _rope_tables(jnp.full((db,), PAST_LEN, jnp.int32))
    qkv_s, u_s = _in_proj_rows(xs, w["w_in_f32"], cos_s, sin_s)
    zt_s = qkv_s.T
    os_s, lses_s, kv_s = [], [], []
    for pi, ((window, dil), c) in enumerate(zip(DILATED_PATTERNS, caches)):
        cn, ot, lt = _sample_attn(zt_s, jnp.transpose(c, (0, 2, 3, 4, 1)), pi, dil)
        kv_s.append(jnp.transpose(cn, (0, 4, 1, 2, 3)))
        os_s.append(jnp.transpose(ot.T[:db].reshape(db, ATTN_WIDTH // LANES, LANES), (1, 0, 2)))
        lses_s.append(lt.T[:db])
    pool_s = _sample_pool(u_s, state_pool, w["w_pool_f32"], w["pool_scale"])

    wr_hi_f32 = lax.reduce_precision(w["w_router"], exponent_bits=8, mantissa_bits=7)
    wr_hi, wr_lo = wr_hi_f32.astype(BF16), (w["w_router"] - wr_hi_f32).astype(BF16)
    post_w = (w["ln1_g"], w["ln1_b"], wr_hi, wr_lo, w["b_router"])
    tm_p = _pick_tile(s, 256)
    assert s % db == 0 and tm_p >= db
    x1, route_p = _post(os_, lses, pool_p, xp, w["w_out"], *post_w, tm=tm_p, name="post_prompt",
                        filler_tiles=1)
    x1, route_s = _post(os_s, lses_s, pool_s, xs, w["w_out_f32"], *post_w, tm=db, name="post_sample",
                        x1_into=x1, row0=s)

    pair_expert = jnp.concatenate([route_p[:, 0:2].reshape(-1), route_s[:, 0:2].reshape(-1)]).astype(jnp.int32)
    n_pairs = pair_expert.shape[0]
    n_tiles = -(-n_pairs // MOE_TILE) + N_EXPERTS
    dest, slot_token, tile_expert, n_used = _routing_plan(pair_expert, n_tiles)
    pitch = _row_pitch(d)
    ys = _moe(x1, slot_token * pitch, tile_expert, n_used, w["w_gate"], w["w_up"], w["w_down"], n_tiles)

    fin_w = (w["ln2_g"], w["ln2_b"], w["w_ple_gate"], w["w_ple"])
    y_p = _final(dest * pitch, ys, x1, route_p, pp, *fin_w, tm=tm_p, row0=0)
    y_s = _final(dest * pitch, ys, x1, route_s, ps, *fin_w, tm=db, row0=s)

    kv_p = []
    for pi, (window, dil) in enumerate(DILATED_PATTERNS):
        keep = min(window, s)
        kv = qkv_p[pi][:, (s - keep) // dil:, ATTN_WIDTH:3 * ATTN_WIDTH].astype(F32)
        kv = jnp.transpose(kv, (1, 0, 2))
        kv_p.append(kv.reshape(1, keep, 2, N_HEADS, HEAD_DIM))
    pool_state_p = u_p[s - POOL_STATE:][None]
    pool_state_s = jnp.concatenate([state_pool[:, 1:], u_s[:, None, :]], axis=1)
    return y_p, y_s, kv_p, pool_state_p, kv_s, pool_state_s


def kernel(x_prompt, x_sample, cache_kv_w128_d1, cache_kv_w512_d4, cache_kv_w2048_d16, state_pool, p_prompt, p_sample, w_in, w_out, w_pool, pool_scale, ln1_g, ln1_b, w_group_router, b_group_router, w_expert_router, b_expert_router, w_gate, w_up, w_down, ln2_g, ln2_b, w_ple, w_ple_gate):
    assert w_in.shape[0] == DEPTH == 1 and x_prompt.shape[0] == 1 and x_sample.shape[1] == 1
    d = x_prompt.shape[-1]
    pad = LANES - N_GROUPS - N_EXPERTS
    w = {
        "w_in_f32": w_in[0],
        "w_out": w_out[0].astype(BF16), "w_out_f32": w_out[0],
        "w_pool": w_pool[0].astype(BF16), "w_pool_f32": w_pool[0],
        "pool_scale": pool_scale[0].reshape(1, POOL_WIDTH),
        "ln1_g": ln1_g[0].reshape(1, d), "ln1_b": ln1_b[0].reshape(1, d),
        "ln2_g": ln2_g[0].reshape(1, d), "ln2_b": ln2_b[0].reshape(1, d),
        "w_router": jnp.concatenate([w_group_router[0], w_expert_router[0], jnp.zeros((d, pad), F32)], axis=1),
        "b_router": jnp.concatenate([b_group_router[0], b_expert_router[0], jnp.zeros((pad,), F32)]).reshape(1, LANES),
        "w_gate": w_gate[0], "w_up": w_up[0], "w_down": w_down[0],
        "w_ple": w_ple[0].astype(BF16),
        "w_ple_gate": w_ple_gate[0].astype(BF16),
    }
    caches = [cache_kv_w128_d1[0], cache_kv_w512_d4[0], cache_kv_w2048_d16[0]]
    y_p, y_s, kv_p, pool_p, kv_s, pool_s = _layer(
        x_prompt[0], x_sample[:, 0], caches, state_pool[0], p_prompt[0, 0], p_sample[0, :, 0], w)
    return (y_p[None], y_s[:, None], kv_p[0][None], kv_p[1][None], kv_p[2][None], pool_p[None],
            kv_s[0][None], kv_s[1][None], kv_s[2][None], pool_s[None])
```

```python
import functools

import jax
import jax.numpy as jnp
from jax import lax
from jax.experimental import pallas as pl
from jax.experimental.pallas import tpu as pltpu

F32 = jnp.float32
BF16 = jnp.bfloat16

PAST_LEN = 8192
HEAD_DIM = 64
N_HEADS = 16
ATTN_WIDTH = N_HEADS * HEAD_DIM
DILATED_PATTERNS = ((128, 1), (512, 4), (2048, 16))
N_PATTERNS = len(DILATED_PATTERNS)
BAND = 128
POOL_WINDOWS = (2, 4, 8, 16)
POOL_GROUP = 256
POOL_WIDTH = POOL_GROUP * len(POOL_WINDOWS)
POOL_STATE = max(POOL_WINDOWS) - 1
ROT_DIM = HEAD_DIM // 4
ROPE_THETA = 500000.0
QKV_WIDTH = N_PATTERNS * 3 * ATTN_WIDTH
N_GROUPS = 4
EXPERTS_PER_GROUP = 8
N_EXPERTS = N_GROUPS * EXPERTS_PER_GROUP
DEPTH = 1
ALPHA = (2.0 * DEPTH) ** 0.25
LN_EPS = 1e-5
NEG_INF = -1e30

LANES = 128
SUBLANES = 8
VMEM_LIMIT = 48 * 1024 * 1024

MOE_TILE = 256
HALO = 16
GATHER_UNROLL = 8


def _cparams(semantics):
    return pltpu.CompilerParams(dimension_semantics=semantics, vmem_limit_bytes=VMEM_LIMIT)


def _rope_tables(pos):
    inv_freq = ROPE_THETA ** (-jnp.arange(0, ROT_DIM, 2, dtype=F32) / ROT_DIM)
    ang = pos.astype(F32)[:, None] * inv_freq[None, :]
    cos, sin = jnp.cos(ang), jnp.sin(ang)
    t = pos.shape[0]
    rest = HEAD_DIM - ROT_DIM
    c64 = jnp.concatenate([cos, cos, jnp.ones((t, rest), F32)], -1)
    s64 = jnp.concatenate([-sin, sin, jnp.zeros((t, rest), F32)], -1)
    return jnp.tile(c64, (1, LANES // HEAD_DIM)), jnp.tile(s64, (1, LANES // HEAD_DIM))


def _rope_chunk(xc, cos, sin):
    half = ROT_DIM // 2
    lane = lax.broadcasted_iota(jnp.int32, xc.shape, 1) & (HEAD_DIM - 1)
    upper = pltpu.roll(xc, LANES - half, 1)
    lower = pltpu.roll(xc, half, 1)
    partner = jnp.where(lane < half, upper, lower)
    return xc * cos + partner * sin


def _split_bf16(a):
    hi = a.astype(BF16)
    return hi, (a - hi.astype(F32)).astype(BF16)


def _mm(a, w):
    if w.dtype == BF16:
        return jnp.dot(a.astype(BF16), w, preferred_element_type=F32)
    ah, al = _split_bf16(a.astype(F32))
    wh, wl = _split_bf16(w)
    return (jnp.dot(ah, wh, preferred_element_type=F32) + jnp.dot(al, wh, preferred_element_type=F32)
            + jnp.dot(ah, wl, preferred_element_type=F32))


ROW_PAD = 1


def _row_pitch(d):
    return d // LANES + ROW_PAD


def _store_row_major(ref, val):
    n, d = val.shape
    nc, pitch = d // LANES, _row_pitch(d)
    for c in range(nc):
        ref[pl.ds(c, n, stride=pitch), :] = val[:, c * LANES:(c + 1) * LANES]
    for c in range(nc, pitch):
        ref[pl.ds(c, n, stride=pitch), :] = jnp.zeros((n, LANES), F32)


def _load_row_major(ref, n):
    pitch = ref.shape[0] // n
    return jnp.concatenate([ref[pl.ds(c, n, stride=pitch), :] for c in range(pitch - ROW_PAD)], axis=1)


ROLE_TILES = 3


def _chunk(c):
    return slice(c * LANES, (c + 1) * LANES)


def _in_proj_rows_kernel(x_ref, w_ref, cos_ref, sin_ref, qkv_ref, u_ref):
    j = pl.program_id(0)
    n_qkv_tiles = N_PATTERNS * ROLE_TILES
    acc = _mm(x_ref[...], w_ref[...])
    role = j % ROLE_TILES
    is_qkv = j < n_qkv_tiles

    @pl.when(is_qkv & (role < 2))
    def _():
        cos = cos_ref[...]
        sin = sin_ref[...]
        for c in range(acc.shape[1] // LANES):
            qkv_ref[:, _chunk(c)] = _rope_chunk(acc[:, _chunk(c)], cos, sin)

    @pl.when(is_qkv & (role == 2))
    def _():
        qkv_ref[...] = acc

    @pl.when(j >= n_qkv_tiles)
    def _():
        u_ref[...] = acc


def _in_proj_rows(x, w, cos_t, sin_t):
    m, d = x.shape
    tn = ATTN_WIDTH
    assert w.shape[1] == QKV_WIDTH + POOL_WIDTH and POOL_WIDTH == tn
    n_qkv_tiles = N_PATTERNS * ROLE_TILES
    return pl.pallas_call(
        _in_proj_rows_kernel,
        grid=(n_qkv_tiles + 1,),
        in_specs=[
            pl.BlockSpec((m, d), lambda j: (0, 0)),
            pl.BlockSpec((d, tn), lambda j: (0, j)),
            pl.BlockSpec((m, LANES), lambda j: (0, 0)),
            pl.BlockSpec((m, LANES), lambda j: (0, 0)),
        ],
        out_specs=[pl.BlockSpec((m, tn), lambda j: (0, jnp.minimum(j, n_qkv_tiles - 1))),
                   pl.BlockSpec((m, tn), lambda j: (0, 0))],
        out_shape=[jax.ShapeDtypeStruct((m, QKV_WIDTH), F32), jax.ShapeDtypeStruct((m, POOL_WIDTH), F32)],
        compiler_params=_cparams(("arbitrary",)),
        name="in_proj_rows",
    )(x, w, cos_t, sin_t)


def _proj_pattern_kernel(x_ref, w_ref, cos_ref, sin_ref, out_ref, wb_ref, acc_ref, stage_ref, *, dil, n_rows):
    t = pl.program_id(0)
    n_chunks, tm, _ = stage_ref.shape

    @pl.when(t == 0)
    def _():
        acc_ref[...] = jnp.zeros(acc_ref.shape, F32)

    @pl.when(t % n_rows == 0)
    def _():
        wb_ref[...] = w_ref[...].astype(BF16)

    prev = acc_ref.at[(t + 1) % 2]
    cos = cos_ref[...]
    sin = sin_ref[...]
    for c in range(n_chunks):
        stage_ref[c] = _rope_chunk(prev[:, _chunk(c)], cos, sin)
    for r in range(dil):
        for c in range(n_chunks):
            rows = stage_ref[c, pl.ds(r, tm // dil, stride=dil), :] if dil > 1 else stage_ref[c]
            out_ref[r, :, _chunk(c)] = rows.astype(out_ref.dtype)
    acc_ref[t % 2] = jnp.dot(x_ref[...], wb_ref[...], preferred_element_type=F32)


def _proj_pattern(xb, w, tables, pi, dil, *, tm):
    m, d = xb.shape
    tn = ATTN_WIDTH
    assert m % tm == 0 and tm % (dil * 2 * SUBLANES) == 0
    n_rows = m // tm
    n = n_rows * ROLE_TILES

    def cur(t):
        return jnp.minimum(t, n - 1)

    def prv(t):
        return jnp.maximum(t - 1, 0)

    def table_map(t):
        identity = (prv(t) // n_rows == ROLE_TILES - 1).astype(jnp.int32)
        return (identity, prv(t) % n_rows, 0)

    return pl.pallas_call(
        functools.partial(_proj_pattern_kernel, dil=dil, n_rows=n_rows),
        grid=(n + 1,),
        in_specs=[
            pl.BlockSpec((tm, d), lambda t: (cur(t) % n_rows, 0)),
            pl.BlockSpec((d, tn), lambda t: (0, pi * ROLE_TILES + cur(t) // n_rows)),
            pl.BlockSpec((None, tm, LANES), table_map),
            pl.BlockSpec((None, tm, LANES), table_map),
        ],
        out_specs=pl.BlockSpec((dil, tm // dil, tn), lambda t: (0, prv(t) % n_rows, prv(t) // n_rows)),
        out_shape=jax.ShapeDtypeStruct((dil, m // dil, ROLE_TILES * tn), BF16),
        scratch_shapes=[pltpu.VMEM((d, tn), BF16), pltpu.VMEM((2, tm, tn), F32),
                        pltpu.VMEM((tn // LANES, tm, LANES), F32)],
        compiler_params=_cparams(("arbitrary",)),
        name=f"proj_p{pi}",
    )(xb, w, *tables)


def _proj_u_pool_kernel(x_ref, w_ref, w_pool_ref, scale_ref, pool_ref, xb_ref, utail_ref, wb_ref, buf_ref):
    t = pl.program_id(0)
    tm = x_ref.shape[0]

    @pl.when(t == 0)
    def _():
        wb_ref[...] = w_ref[...].astype(BF16)
        buf_ref[...] = jnp.zeros(buf_ref.shape, F32)

    prev = buf_ref.at[(t + 1) % 2]

    def store(g, val):
        pool_ref[:, g * POOL_GROUP:(g + 1) * POOL_GROUP] = val.astype(pool_ref.dtype)

    _pool_groups(prev, tm, jnp.maximum(t - 1, 0) * tm, w_pool_ref, scale_ref, store)

    xb = x_ref[...].astype(BF16)
    xb_ref[...] = xb
    u = jnp.dot(xb, wb_ref[...], preferred_element_type=F32)
    buf_ref[t % 2, HALO:HALO + tm, :] = u
    tail = u[tm - HALO:, :]
    utail_ref[...] = tail
    prev[0:HALO, :] = tail


def _proj_u_pool(x, w, w_pool_bf16, pool_scale, *, tm):
    m, d = x.shape
    assert m % tm == 0 and tm % HALO == 0
    n = m // tm
    n_qkv_tiles = N_PATTERNS * ROLE_TILES
    cur = lambda t: (jnp.minimum(t, n - 1), 0)
    prv = lambda t: (jnp.maximum(t - 1, 0), 0)
    return pl.pallas_call(
        _proj_u_pool_kernel,
        grid=(n + 1,),
        in_specs=[
            pl.BlockSpec((tm, d), cur),
            pl.BlockSpec((d, POOL_WIDTH), lambda t: (0, n_qkv_tiles), pipeline_mode=pl.Buffered(1)),
            pl.BlockSpec((len(POOL_WINDOWS), POOL_GROUP, POOL_GROUP), lambda t: (0, 0, 0)),
            pl.BlockSpec((1, POOL_WIDTH), lambda t: (0, 0)),
        ],
        out_specs=[
            pl.BlockSpec((tm, POOL_WIDTH), prv),
            pl.BlockSpec((tm, d), cur),
            pl.BlockSpec((HALO, POOL_WIDTH), lambda t: (0, 0)),
        ],
        out_shape=[
            jax.ShapeDtypeStruct((m, POOL_WIDTH), BF16),
            jax.ShapeDtypeStruct((m, d), BF16),
            jax.ShapeDtypeStruct((HALO, POOL_WIDTH), F32),
        ],
        scratch_shapes=[pltpu.VMEM((d, POOL_WIDTH), BF16), pltpu.VMEM((2, HALO + tm, POOL_WIDTH), F32)],
        compiler_params=_cparams(("arbitrary",)),
        name="proj_u_pool",
    )(x, w, w_pool_bf16, pool_scale)


def _attn_kernel(q_ref, kp_ref, kc_ref, vp_ref, vc_ref, o_ref, lse_ref,
                 lim_ref, s_ref, e_ref, m_ref, den_ref, *, r_max, dil):
    n = pl.program_id(0)
    r = pl.program_id(1)
    qi = lax.broadcasted_iota(jnp.int32, (BAND, 2 * BAND), 0)
    ki = lax.broadcasted_iota(jnp.int32, (BAND, 2 * BAND), 1)
    dist = BAND + qi - ki
    valid = (dist >= 0) & (dist <= r_max) & ((n > 0) | (ki >= BAND))
    lim_ref[...] = jnp.where(valid, jnp.float32(jnp.finfo(F32).max), jnp.float32(NEG_INF))

    lane = lax.broadcasted_iota(jnp.int32, (BAND, LANES), 1)
    heads_per_chunk = LANES // HEAD_DIM
    n_chunks = ATTN_WIDTH // LANES
    lane_head = lane >> (HEAD_DIM.bit_length() - 1)
    qscale = [jnp.where(lane_head == hh, HEAD_DIM ** -0.5, 0.0).astype(BF16) for hh in range(heads_per_chunk)]
    nt = (((1,), (1,)), ((), ()))

    for c in range(n_chunks):
        sl = slice(c * LANES, (c + 1) * LANES)
        q2 = q_ref[:, sl]
        kp, kc = kp_ref[:, sl], kc_ref[:, sl]
        for hh in range(heads_per_chunk):
            h = c * heads_per_chunk + hh
            qm = q2 * qscale[hh]
            s = jnp.concatenate(
                [lax.dot_general(qm, kp, nt, preferred_element_type=F32),
                 lax.dot_general(qm, kc, nt, preferred_element_type=F32)], axis=1)
            s = jnp.minimum(s, lim_ref[...])
            s_ref[h] = s
            m_ref[h] = jnp.broadcast_to(jnp.max(s, axis=-1, keepdims=True), (BAND, LANES))

    lse_acc = jnp.zeros((BAND, LANES), F32)
    for h in range(N_HEADS):
        m = m_ref[h]
        e = jnp.exp(s_ref[h] - jnp.concatenate([m, m], axis=1))
        e_ref[h] = e.astype(BF16)
        den = jnp.broadcast_to(jnp.sum(e, axis=-1, keepdims=True), (BAND, LANES))
        den_ref[h] = den
        lse_acc = jnp.where(lane == h, m + jnp.log(den), lse_acc)

    rows = slice(None) if dil == 1 else pl.ds(r, BAND, stride=dil)
    lse_ref[rows, :] = lse_acc
    for c in range(n_chunks):
        sl = slice(c * LANES, (c + 1) * LANES)
        vp, vc = vp_ref[:, sl], vc_ref[:, sl]
        o2 = jnp.zeros((BAND, LANES), F32)
        for hh in range(heads_per_chunk):
            h = c * heads_per_chunk + hh
            pv = (jnp.dot(e_ref[h, :, :BAND], vp, preferred_element_type=F32)
                  + jnp.dot(e_ref[h, :, BAND:], vc, preferred_element_type=F32))
            o2 = jnp.where(lane_head == hh, pv / den_ref[h], o2)
        o_ref[c, rows, :] = o2


def _attn_pattern(qkv, pi, dil, window):
    s = qkv.shape[0] * qkv.shape[1]
    assert qkv.shape[0] == dil and s % (dil * BAND) == 0
    nb = s // (dil * BAND)
    r_max = window // dil

    def cur(role):
        return lambda n, r: (r, n, role)

    def prev(role):
        return lambda n, r: (r, jnp.maximum(n - 1, 0), role)

    blk = (None, BAND, ATTN_WIDTH)
    span = BAND * dil
    return pl.pallas_call(
        functools.partial(_attn_kernel, r_max=r_max, dil=dil),
        grid=(nb, dil),
        in_specs=[
            pl.BlockSpec(blk, cur(0)),
            pl.BlockSpec(blk, prev(1)),
            pl.BlockSpec(blk, cur(1)),
            pl.BlockSpec(blk, prev(2)),
            pl.BlockSpec(blk, cur(2)),
        ],
        out_specs=[
            pl.BlockSpec((ATTN_WIDTH // LANES, span, LANES), lambda n, r: (0, n, 0)),
            pl.BlockSpec((span, LANES), lambda n, r: (n, 0)),
        ],
        out_shape=[
            jax.ShapeDtypeStruct((ATTN_WIDTH // LANES, s, LANES), F32),
            jax.ShapeDtypeStruct((s, LANES), F32),
        ],
        scratch_shapes=[
            pltpu.VMEM((BAND, 2 * BAND), F32),
            pltpu.VMEM((N_HEADS, BAND, 2 * BAND), F32),
            pltpu.VMEM((N_HEADS, BAND, 2 * BAND), BF16),
            pltpu.VMEM((N_HEADS, BAND, LANES), F32),
            pltpu.VMEM((N_HEADS, BAND, LANES), F32),
        ],
        compiler_params=_cparams(("arbitrary", "arbitrary")),
        name=f"attn_p{pi}",
    )(qkv, qkv, qkv, qkv, qkv)


def _pool_groups(buf_ref, tm, pos0, w_pool_ref, scale_ref, store):
    row = lax.broadcasted_iota(jnp.int32, (tm, 1), 0) + pos0
    for g, win in enumerate(POOL_WINDOWS):
        sl = slice(g * POOL_GROUP, (g + 1) * POOL_GROUP)
        cur = buf_ref[HALO:HALO + tm, sl]
        acc = cur
        for jj in range(1, win):
            acc = acc + buf_ref[HALO - jj:HALO - jj + tm, sl]
        cnt = jnp.minimum(row + 1, win).astype(F32)
        d = acc / cnt - cur
        y = _mm(d, w_pool_ref[g])
        store(g, y * scale_ref[:, sl])


def _sample_attn_kernel(zt_ref, c_ref, cn_ref, ot_ref, lt_ref, s_ref, *, dil, hps, length):
    b = pl.program_id(0)
    g = pl.program_id(1)

    @pl.when((b == 0) & (g == 0))
    def _():
        ot_ref[...] = jnp.zeros(ot_ref.shape, F32)
        lt_ref[...] = jnp.zeros(lt_ref.shape, F32)

    rows = hps * HEAD_DIM
    base = pl.multiple_of(g * rows, rows)
    lane_z = lax.broadcasted_iota(jnp.int32, (rows, zt_ref.shape[1]), 1)

    def column(role):
        blk = zt_ref[pl.ds(role * ATTN_WIDTH + base, rows), :]
        return jnp.sum(jnp.where(lane_z == b, blk, 0.0), axis=-1, keepdims=True)

    qc, knc, vnc = column(0), column(1), column(2)
    scale = HEAD_DIM ** -0.5
    lane = lax.broadcasted_iota(jnp.int32, (1, length), 1)
    key_ok = (lane & (dil - 1)) == 0
    last = lane == length - 1
    lane_o = lax.broadcasted_iota(jnp.int32, (HEAD_DIM, LANES), 1)
    lane_l = lax.broadcasted_iota(jnp.int32, (hps, LANES), 1)

    def head_rows(h):
        return slice(h * HEAD_DIM, (h + 1) * HEAD_DIM)

    for h in range(hps):
        k = c_ref[0, h]
        s_ref[h:h + 1, :] = jnp.sum(k * qc[head_rows(h)], axis=0, keepdims=True)
        cn_ref[0, h] = jnp.where(last, knc[head_rows(h)], pltpu.roll(k, length - 1, 1))

    s = jnp.where(key_ok, s_ref[...] * scale, NEG_INF)
    s_n = jnp.sum((knc * qc).reshape(hps, HEAD_DIM, 1), axis=1) * scale
    m = jnp.maximum(jnp.max(s, axis=-1, keepdims=True), s_n)
    e = jnp.exp(s - m)
    e_n = jnp.exp(s_n - m)
    den = jnp.sum(e, axis=-1, keepdims=True) + e_n
    s_ref[...] = e
    lse = m + jnp.log(den)
    head0 = pl.multiple_of(g * hps, hps)
    lt_ref[pl.ds(head0, hps), :] = jnp.where(lane_l == b, lse, lt_ref[pl.ds(head0, hps), :])

    for h in range(hps):
        v = c_ref[1, h]
        vn = vnc[head_rows(h)]
        o = (jnp.sum(v * s_ref[h:h + 1, :], axis=-1, keepdims=True) + vn * e_n[h:h + 1, :]) / den[h:h + 1, :]
        cn_ref[1, h] = jnp.where(last, vn, pltpu.roll(v, length - 1, 1))
        r0 = pl.multiple_of(base + h * HEAD_DIM, HEAD_DIM)
        ot_ref[pl.ds(r0, HEAD_DIM), :] = jnp.where(lane_o == b, o, ot_ref[pl.ds(r0, HEAD_DIM), :])


def _sample_attn(zt, c_t, pi, dil):
    db, _, _, _, length = c_t.shape
    assert db <= LANES and length == dil * BAND
    hps = min(N_HEADS, max(1, (4 * 2048) // length))
    blk = (None, 2, hps, HEAD_DIM, length)
    return pl.pallas_call(
        functools.partial(_sample_attn_kernel, dil=dil, hps=hps, length=length),
        grid=(db, N_HEADS // hps),
        in_specs=[
            pl.BlockSpec((3 * ATTN_WIDTH, db), lambda b, g: (pi, 0)),
            pl.BlockSpec(blk, lambda b, g: (b, 0, g, 0, 0)),
        ],
        out_specs=[
            pl.BlockSpec(blk, lambda b, g: (b, 0, g, 0, 0)),
            pl.BlockSpec((ATTN_WIDTH, LANES), lambda b, g: (0, 0)),
            pl.BlockSpec((LANES, LANES), lambda b, g: (0, 0)),
        ],
        out_shape=[
            jax.ShapeDtypeStruct(c_t.shape, c_t.dtype),
            jax.ShapeDtypeStruct((ATTN_WIDTH, LANES), F32),
            jax.ShapeDtypeStruct((LANES, LANES), F32),
        ],
        scratch_shapes=[pltpu.VMEM((hps, length), F32)],
        compiler_params=_cparams(("arbitrary", "arbitrary")),
        name=f"sample_attn_p{pi}",
    )(zt, c_t)


def _sample_pool_kernel(u_ref, sp_ref, w_pool_ref, scale_ref, pool_ref, buf_ref):
    buf_ref[...] = jnp.zeros(buf_ref.shape, F32)
    buf_ref[HALO - POOL_STATE:HALO, :] = sp_ref[...]
    buf_ref[HALO:HALO + 1, :] = u_ref[...]

    def store(g, val):
        pool_ref[:, g * POOL_GROUP:(g + 1) * POOL_GROUP] = val[0:1]

    _pool_groups(buf_ref, HALO, PAST_LEN, w_pool_ref, scale_ref, store)


def _head_expand():
    dim_head = jnp.arange(ATTN_WIDTH, dtype=jnp.int32) // HEAD_DIM
    return (jnp.arange(LANES, dtype=jnp.int32)[:, None] == dim_head[None, :]).astype(F32)


def _sample_pool(us, state_pool, w_pool_bf16, pool_scale):
    db = us.shape[0]
    pool = pl.pallas_call(
        _sample_pool_kernel,
        grid=(db,),
        in_specs=[
            pl.BlockSpec((None, 1, POOL_WIDTH), lambda b: (b, 0, 0)),
            pl.BlockSpec((None, POOL_STATE, POOL_WIDTH), lambda b: (b, 0, 0)),
            pl.BlockSpec((len(POOL_WINDOWS), POOL_GROUP, POOL_GROUP), lambda b: (0, 0, 0)),
            pl.BlockSpec((1, POOL_WIDTH), lambda b: (0, 0)),
        ],
        out_specs=pl.BlockSpec((None, 1, POOL_WIDTH), lambda b: (b, 0, 0)),
        out_shape=jax.ShapeDtypeStruct((db, 1, POOL_WIDTH), F32),
        scratch_shapes=[pltpu.VMEM((2 * HALO, POOL_WIDTH), F32)],
        compiler_params=_cparams(("arbitrary",)),
        name="sample_pool",
    )(us.reshape(db, 1, POOL_WIDTH), state_pool, w_pool_bf16, pool_scale)
    return pool.reshape(db, POOL_WIDTH)


def _layer_norm(xf, g, b):
    mu = jnp.mean(xf, axis=-1, keepdims=True)
    xc = xf - mu
    var = jnp.mean(xc * xc, axis=-1, keepdims=True)
    return xc * lax.rsqrt(var + LN_EPS) * g + b


def _route(logits):
    lane = lax.broadcasted_iota(jnp.int32, logits.shape, 1)
    lane_f = lane.astype(F32)
    neg = jnp.float32(-jnp.inf)

    def first_argmax(v, vmax):
        return jnp.min(jnp.where(v == vmax, lane_f, jnp.float32(LANES)), axis=-1, keepdims=True)

    gl = jnp.where(lane < N_GROUPS, logits, neg)
    gmax = jnp.max(gl, axis=-1, keepdims=True)
    g_sel = first_argmax(gl, gmax)
    g_gate = 1.0 / jnp.sum(jnp.exp(gl - gmax), axis=-1, keepdims=True)
    lane_group = (((lane + (EXPERTS_PER_GROUP - N_GROUPS)) >> 3) - 1).astype(F32)
    el = jnp.where(lane_group == g_sel, logits, neg)
    v1 = jnp.max(el, axis=-1, keepdims=True)
    i1 = first_argmax(el, v1)
    el2 = jnp.where(lane_f == i1, neg, el)
    v2 = jnp.max(el2, axis=-1, keepdims=True)
    i2 = first_argmax(el2, v2)
    t = jnp.exp(v2 - v1)
    w1 = (1.0 / (1.0 + t)) * g_gate
    w2 = (t / (1.0 + t)) * g_gate
    route = jnp.where(lane == 0, i1 - N_GROUPS, 0.0)
    route = jnp.where(lane == 1, i2 - N_GROUPS, route)
    route = jnp.where(lane == 2, w1, route)
    route = jnp.where(lane == 3, w2, route)
    return route


def _post_kernel(o0_ref, o1_ref, o2_ref, l0_ref, l1_ref, l2_ref, pool_ref, x_ref, hexp_ref,
                 wo_ref, g1_ref, b1_ref, wr_ref, br_ref, *rest, n_row_tiles):
    x1_ref, route_ref, wrh_ref, wrl_ref = rest[-4:]
    i = pl.program_id(0)

    @pl.when(i == 0)
    def _():
        wrh_ref[...], wrl_ref[...] = _split_bf16(wr_ref[...])

    @pl.when(i < n_row_tiles)
    def _():
        l0, l1, l2 = l0_ref[...], l1_ref[...], l2_ref[...]
        lmax = jnp.maximum(jnp.maximum(l0, l1), l2)
        e0, e1, e2 = jnp.exp(l0 - lmax), jnp.exp(l1 - lmax), jnp.exp(l2 - lmax)
        den = e0 + e1 + e2
        hexp = hexp_ref[...]
        attn = jnp.zeros((x_ref.shape[0], ATTN_WIDTH), F32)
        for o_ref, e in ((o0_ref, e0), (o1_ref, e1), (o2_ref, e2)):
            wh, wl = _split_bf16(e / den)
            w_x = (jnp.dot(wh, hexp, preferred_element_type=F32) + jnp.dot(wl, hexp, preferred_element_type=F32))
            o = jnp.concatenate([o_ref[c] for c in range(ATTN_WIDTH // LANES)], axis=1)
            attn = attn + o * w_x
        h = _mm(attn, wo_ref[0:ATTN_WIDTH, :]) + _mm(pool_ref[...], wo_ref[ATTN_WIDTH:, :])
        x1 = _layer_norm(ALPHA * x_ref[...] + h, g1_ref[...], b1_ref[...])
        _store_row_major(x1_ref, x1)
        xh, xl = _split_bf16(x1)
        wrh = wrh_ref[...]
        logits = (jnp.dot(xh, wrh, preferred_element_type=F32) + jnp.dot(xl, wrh, preferred_element_type=F32)
                  + jnp.dot(xh, wrl_ref[...], preferred_element_type=F32)) + br_ref[...]
        route_ref[...] = _route(logits)

    @pl.when(i >= n_row_tiles)
    def _():
        x1_ref[...] = jnp.zeros(x1_ref.shape, F32)


def _row_spec(tm, width):
    return pl.BlockSpec((tm, width), lambda i: (i, 0))


def _const_spec(shape):
    return pl.BlockSpec(shape, lambda i: (0,) * len(shape), pipeline_mode=pl.Buffered(1))


def _post(os_, lses, pool, x, wo, g1, b1, wr, br, *, tm, name, filler_tiles=0, x1_into=None, row0=0):
    m, d = x.shape
    n = m // tm
    pitch = _row_pitch(d)
    assert (tm * pitch) % SUBLANES == 0
    hexp = _head_expand().astype(BF16)
    last = n - 1

    def rows(width):
        return pl.BlockSpec((tm, width), lambda i: (jnp.minimum(i, last), 0))

    in_specs = ([pl.BlockSpec((ATTN_WIDTH // LANES, tm, LANES), lambda i: (0, jnp.minimum(i, last), 0))] * 3
                + [rows(LANES)] * 3
                + [rows(POOL_WIDTH), rows(d), _const_spec((LANES, ATTN_WIDTH)),
                   _const_spec((d, d)), _const_spec((1, d)), _const_spec((1, d)),
                   _const_spec((d, LANES)), _const_spec((1, LANES))])
    operands = [*os_, *lses, pool, x, hexp, wo, g1, b1, wr, br]
    aliases = {}
    if x1_into is None:
        x1_shape = jax.ShapeDtypeStruct(((m + filler_tiles * tm) * pitch, LANES), F32)
    else:
        assert filler_tiles == 0 and row0 % tm == 0
        in_specs.append(pl.BlockSpec(memory_space=pl.ANY))
        operands.append(x1_into)
        aliases = {len(operands) - 1: 0}
        x1_shape = jax.ShapeDtypeStruct(x1_into.shape, F32)
    return pl.pallas_call(
        functools.partial(_post_kernel, n_row_tiles=n),
        grid=(n + filler_tiles,),
        in_specs=in_specs,
        out_specs=[pl.BlockSpec((tm * pitch, LANES), lambda i: (row0 // tm + i, 0)), rows(LANES)],
        out_shape=[x1_shape, jax.ShapeDtypeStruct((m, LANES), F32)],
        input_output_aliases=aliases,
        scratch_shapes=[pltpu.VMEM((d, LANES), BF16), pltpu.VMEM((d, LANES), BF16)],
        compiler_params=_cparams(("arbitrary",)),
        name=name,
    )(*operands)


def _routing_plan(pair_expert, n_tiles):
    p = pair_expert.shape[0]
    experts = jnp.arange(N_EXPERTS, dtype=jnp.int32)
    onehot = (pair_expert[:, None] == experts[None, :]).astype(jnp.int32)
    csum = jnp.cumsum(onehot, axis=0)
    rank = jnp.take_along_axis(csum, pair_expert[:, None], axis=1)[:, 0] - 1
    counts = csum[-1]
    tiles_per = (counts + MOE_TILE - 1) // MOE_TILE
    tile_end = jnp.cumsum(tiles_per)
    tile_start = tile_end - tiles_per
    dest = (tile_start[pair_expert] * MOE_TILE + rank).astype(jnp.int32)
    n_used = tile_end[-1]
    tile_ids = jnp.arange(n_tiles, dtype=jnp.int32)
    tile_expert = jnp.sum((tile_ids[:, None] >= tile_end[None, :]).astype(jnp.int32), axis=1)
    last_expert = jnp.sum((n_used - 1 >= tile_end).astype(jnp.int32))
    tile_expert = jnp.where(tile_ids < n_used, tile_expert, last_expert).astype(jnp.int32)
    slot_token = jnp.zeros((n_tiles * MOE_TILE,), jnp.int32).at[dest].set(
        jnp.arange(p, dtype=jnp.int32) // 2, unique_indices=True, mode="promise_in_bounds")
    return dest, slot_token, tile_expert, n_used.reshape(1).astype(jnp.int32)


def _moe_kernel(te_ref, nu_ref, st_ref, x_ref, wg_ref, wu_ref, wd_ref, ys_ref,
                buf_ref, wgb_ref, wub_ref, wdb_ref, sem_ref, *, nc, pitch):
    c = pl.program_id(0)
    n_tiles = pl.num_programs(0)
    n_used = nu_ref[0]

    def issue(tile, slot):
        def body(g, carry):
            for k in range(GATHER_UNROLL):
                i = g * GATHER_UNROLL + k
                src = st_ref[tile * MOE_TILE + i]
                pltpu.make_async_copy(x_ref.at[pl.ds(src, nc), :], buf_ref.at[slot, pl.ds(i * pitch, nc), :],
                                      sem_ref.at[slot]).start(priority=k % 2)
            return carry

        lax.fori_loop(0, MOE_TILE // GATHER_UNROLL, body, 0)

    def wait(slot):
        pltpu.make_async_copy(x_ref.at[pl.ds(0, MOE_TILE * nc), :], buf_ref.at[slot, pl.ds(0, MOE_TILE * nc), :],
                              sem_ref.at[slot]).wait()

    slot = c % 2

    @pl.when((c == 0) & (n_used > 0))
    def _():
        issue(0, 0)

    @pl.when((c + 1 < n_tiles) & (c + 1 < n_used))
    def _():
        issue(c + 1, 1 - slot)

    prev_expert = te_ref[jnp.maximum(c - 1, 0)]

    @pl.when((c == 0) | (te_ref[c] != prev_expert))
    def _():
        wgb_ref[...] = wg_ref[...].astype(BF16)
        wub_ref[...] = wu_ref[...].astype(BF16)
        wdb_ref[...] = wd_ref[...].astype(BF16)

    @pl.when(c < n_used)
    def _():
        wait(slot)
        x = _load_row_major(buf_ref.at[slot], MOE_TILE).astype(BF16)
        gate = jnp.dot(x, wgb_ref[...], preferred_element_type=F32)
        up = jnp.dot(x, wub_ref[...], preferred_element_type=F32)
        h = jax.nn.silu(gate) * up
        _store_row_major(ys_ref, jnp.dot(h.astype(BF16), wdb_ref[...], preferred_element_type=F32))

    @pl.when(c >= n_used)
    def _():
        ys_ref[...] = jnp.zeros(ys_ref.shape, ys_ref.dtype)


def _moe(x1, slot_row, tile_expert, n_used, w_gate, w_up, w_down, n_tiles):
    d = w_gate.shape[1]
    nc, pitch = d // LANES, _row_pitch(d)
    f = w_gate.shape[-1]
    grid_spec = pltpu.PrefetchScalarGridSpec(
        num_scalar_prefetch=3,
        grid=(n_tiles,),
        in_specs=[
            pl.BlockSpec(memory_space=pl.ANY),
            pl.BlockSpec((None, d, f), lambda c, te, nu, st: (te[c], 0, 0)),
            pl.BlockSpec((None, d, f), lambda c, te, nu, st: (te[c], 0, 0)),
            pl.BlockSpec((None, f, d), lambda c, te, nu, st: (te[c], 0, 0)),
        ],
        out_specs=pl.BlockSpec((MOE_TILE * pitch, LANES), lambda c, te, nu, st: (c, 0)),
        scratch_shapes=[
            pltpu.VMEM((2, MOE_TILE * pitch, LANES), F32),
            pltpu.VMEM((d, f), BF16),
            pltpu.VMEM((d, f), BF16),
            pltpu.VMEM((f, d), BF16),
            pltpu.SemaphoreType.DMA((2,)),
        ],
    )
    return pl.pallas_call(
        functools.partial(_moe_kernel, nc=nc, pitch=pitch),
        grid_spec=grid_spec,
        out_shape=jax.ShapeDtypeStruct((n_tiles * MOE_TILE * pitch, LANES), F32),
        compiler_params=_cparams(("arbitrary",)),
        name="moe",
    )(tile_expert, n_used, slot_row, x1, w_gate, w_up, w_down)


def _final_kernel(dest_ref, ys_ref, x1_ref, route_ref, p_ref, g2_ref, b2_ref, wpg_ref, wp_ref, o_ref,
                  buf_ref, sem_ref, *, tm, pair0, nc, pitch):
    i = pl.program_id(0)
    n_steps = pl.num_programs(0)

    def issue(step, slot):
        def body(r, carry):
            for k in range(2):
                src = dest_ref[pair0 + (step * tm + r) * 2 + k]
                pltpu.make_async_copy(ys_ref.at[pl.ds(src, nc), :], buf_ref.at[slot, k, pl.ds(r * pitch, nc), :],
                                      sem_ref.at[slot]).start()
            return carry

        lax.fori_loop(0, tm, body, 0, unroll=GATHER_UNROLL)

    def wait(slot):
        for k in range(2):
            pltpu.make_async_copy(ys_ref.at[pl.ds(0, tm * nc), :], buf_ref.at[slot, k, pl.ds(0, tm * nc), :],
                                  sem_ref.at[slot]).wait()

    slot = i % 2

    @pl.when(i == 0)
    def _():
        issue(0, 0)

    @pl.when(i + 1 < n_steps)
    def _():
        issue(i + 1, 1 - slot)

    wait(slot)
    route = route_ref[...]
    y = (route[:, 2:3] * _load_row_major(buf_ref.at[slot, 0], tm)
         + route[:, 3:4] * _load_row_major(buf_ref.at[slot, 1], tm))
    x2 = _layer_norm(ALPHA * _load_row_major(x1_ref, tm) + y, g2_ref[...], b2_ref[...])
    gate = jax.nn.sigmoid(jnp.dot(x2.astype(BF16), wpg_ref[...], preferred_element_type=F32))
    ple = jnp.dot(p_ref[...].astype(BF16), wp_ref[...], preferred_element_type=F32)
    o_ref[...] = x2 + gate * ple


def _final(dest, ys, x1, route, p, g2, b2, wpg, wp, *, tm, row0):
    m, pd = p.shape
    d = wpg.shape[0]
    nc, pitch = d // LANES, _row_pitch(d)
    assert m % tm == 0 and row0 % tm == 0
    pair0 = 2 * row0
    grid_spec = pltpu.PrefetchScalarGridSpec(
        num_scalar_prefetch=1,
        grid=(m // tm,),
        in_specs=[
            pl.BlockSpec(memory_space=pl.ANY),
            pl.BlockSpec((tm * pitch, LANES), lambda i, dst: (row0 // tm + i, 0)),
            pl.BlockSpec((tm, LANES), lambda i, dst: (i, 0)),
            pl.BlockSpec((tm, pd), lambda i, dst: (i, 0)),
            pl.BlockSpec((1, d), lambda i, dst: (0, 0)),
            pl.BlockSpec((1, d), lambda i, dst: (0, 0)),
            pl.BlockSpec((d, d), lambda i, dst: (0, 0)),
            pl.BlockSpec((pd, d), lambda i, dst: (0, 0)),
        ],
        out_specs=pl.BlockSpec((tm, d), lambda i, dst: (i, 0)),
        scratch_shapes=[pltpu.VMEM((2, 2, tm * pitch, LANES), F32), pltpu.SemaphoreType.DMA((2,))],
    )
    return pl.pallas_call(
        functools.partial(_final_kernel, tm=tm, pair0=pair0, nc=nc, pitch=pitch),
        grid_spec=grid_spec,
        out_shape=jax.ShapeDtypeStruct((m, d), F32),
        compiler_params=_cparams(("arbitrary",)),
        name="final",
    )(dest, ys, x1, route, p, g2, b2, wpg, wp)


def _pick_tile(m, pref):
    t = pref
    while m % t:
        t //= 2
    return t


def _layer(xp, xs, caches, state_pool, pp, ps, w):
    s, d = xp.shape
    db = xs.shape[0]

    cos_p, sin_p = _rope_tables(jnp.arange(s, dtype=jnp.int32))
    dils = tuple(dil for _, dil in DILATED_PATTERNS)
    tables = (jnp.stack([cos_p, jnp.ones_like(cos_p)]), jnp.stack([sin_p, jnp.zeros_like(sin_p)]))
    tm_in = _pick_tile(s, 512)
    pool_p, xp_b, u_tail = _proj_u_pool(xp, w["w_in_f32"], w["w_pool"], w["pool_scale"], tm=tm_in)
    qkv_p = [_proj_pattern(xp_b, w["w_in_f32"], tables, pi, dil, tm=tm_in) for pi, dil in enumerate(dils)]
    os_, lses = [], []
    for pi, (window, dil) in enumerate(DILATED_PATTERNS):
        o, l = _attn_pattern(qkv_p[pi], pi, dil, window)
        os_.append(o)
        lses.append(l)

    cos_s, sin_s = _rope_tables(jnp.full((db,), PAST_LEN, jnp.int32))
    qkv_s, u_s = _in_proj_rows(xs, w["w_in_f32"], cos_s, sin_s)
    zt_s = qkv_s.T
    os_s, lses_s, kv_s = [], [], []
    for pi, ((window, dil), c) in enumerate(zip(DILATED_PATTERNS, caches)):
        cn, ot, lt = _sample_attn(zt_s, jnp.transpose(c, (0, 2, 3, 4, 1)), pi, dil)
        kv_s.append(jnp.transpose(cn, (0, 4, 1, 2, 3)))
        os_s.append(jnp.transpose(ot.T[:db].reshape(db, ATTN_WIDTH // LANES, LANES), (1, 0, 2)))
        lses_s.append(lt.T[:db])
    pool_s = _sample_pool(u_s, state_pool, w["w_pool_f32"], w["pool_scale"])

    post_w = (w["ln1_g"], w["ln1_b"], w["w_router"], w["b_router"])
    tm_p = _pick_tile(s, 256)
    assert s % db == 0 and tm_p >= db
    x1, route_p = _post(os_, lses, pool_p, xp, w["w_out"], *post_w, tm=tm_p, name="post_prompt",
                        filler_tiles=1)
    x1, route_s = _post(os_s, lses_s, pool_s, xs, w["w_out_f32"], *post_w, tm=db, name="post_sample",
                        x1_into=x1, row0=s)

    pair_expert = jnp.concatenate([route_p[:, 0:2].reshape(-1), route_s[:, 0:2].reshape(-1)]).astype(jnp.int32)
    n_pairs = pair_expert.shape[0]
    n_tiles = -(-n_pairs // MOE_TILE) + N_EXPERTS
    dest, slot_token, tile_expert, n_used = _routing_plan(pair_expert, n_tiles)
    pitch = _row_pitch(d)
    ys = _moe(x1, slot_token * pitch, tile_expert, n_used, w["w_gate"], w["w_up"], w["w_down"], n_tiles)

    fin_w = (w["ln2_g"], w["ln2_b"], w["w_ple_gate"], w["w_ple"])
    y_p = _final(dest * pitch, ys, x1, route_p, pp, *fin_w, tm=tm_p, row0=0)
    y_s = _final(dest * pitch, ys, x1, route_s, ps, *fin_w, tm=db, row0=s)

    kv_p = []
    for pi, (window, dil) in enumerate(DILATED_PATTERNS):
        keep = min(window, s)
        kv = qkv_p[pi][:, (s - keep) // dil:, ATTN_WIDTH:3 * ATTN_WIDTH].astype(F32)
        kv = jnp.transpose(kv, (1, 0, 2))
        kv_p.append(kv.reshape(1, keep, 2, N_HEADS, HEAD_DIM))
    pool_state_p = u_tail[HALO - POOL_STATE:][None]
    pool_state_s = jnp.concatenate([state_pool[:, 1:], u_s[:, None, :]], axis=1)
    return y_p, y_s, kv_p, pool_state_p, kv_s, pool_state_s


def kernel(x_prompt, x_sample, cache_kv_w128_d1, cache_kv_w512_d4, cache_kv_w2048_d16, state_pool, p_prompt, p_sample, w_in, w_out, w_pool, pool_scale, ln1_g, ln1_b, w_group_router, b_group_router, w_expert_router, b_expert_router, w_gate, w_up, w_down, ln2_g, ln2_b, w_ple, w_ple_gate):
    assert w_in.shape[0] == DEPTH == 1 and x_prompt.shape[0] == 1 and x_sample.shape[1] == 1
    d = x_prompt.shape[-1]
    pad = LANES - N_GROUPS - N_EXPERTS
    w = {
        "w_in_f32": w_in[0],
        "w_out": w_out[0].astype(BF16), "w_out_f32": w_out[0],
        "w_pool": w_pool[0].astype(BF16), "w_pool_f32": w_pool[0],
        "pool_scale": pool_scale[0].reshape(1, POOL_WIDTH),
        "ln1_g": ln1_g[0].reshape(1, d), "ln1_b": ln1_b[0].reshape(1, d),
        "ln2_g": ln2_g[0].reshape(1, d), "ln2_b": ln2_b[0].reshape(1, d),
        "w_router": jnp.concatenate([w_group_router[0], w_expert_router[0], jnp.zeros((d, pad), F32)], axis=1),
        "b_router": jnp.concatenate([b_group_router[0], b_expert_router[0], jnp.zeros((pad,), F32)]).reshape(1, LANES),
        "w_gate": w_gate[0], "w_up": w_up[0], "w_down": w_down[0],
        "w_ple": w_ple[0].astype(BF16),
        "w_ple_gate": w_ple_gate[0].astype(BF16),
    }
    caches = [cache_kv_w128_d1[0], cache_kv_w512_d4[0], cache_kv_w2048_d16[0]]
    y_p, y_s, kv_p, pool_p, kv_s, pool_s = _layer(
        x_prompt[0], x_sample[:, 0], caches, state_pool[0], p_prompt[0, 0], p_sample[0, :, 0], w)
    return (y_p[None], y_s[:, None], kv_p[0][None], kv_p[1][None], kv_p[2][None], pool_p[None],
            kv_s[0][None], kv_s[1][None], kv_s[2][None], pool_s[None])
```

```python
import functools

import jax
import jax.numpy as jnp
from jax import lax
from jax.experimental import pallas as pl
from jax.experimental.pallas import tpu as pltpu

F32 = jnp.float32
BF16 = jnp.bfloat16

PAST_LEN = 8192
HEAD_DIM = 64
N_HEADS = 16
ATTN_WIDTH = N_HEADS * HEAD_DIM
DILATED_PATTERNS = ((128, 1), (512, 4), (2048, 16))
N_PATTERNS = len(DILATED_PATTERNS)
BAND = 128
POOL_WINDOWS = (2, 4, 8, 16)
POOL_GROUP = 256
POOL_WIDTH = POOL_GROUP * len(POOL_WINDOWS)
POOL_STATE = max(POOL_WINDOWS) - 1
ROT_DIM = HEAD_DIM // 4
ROPE_THETA = 500000.0
QKV_WIDTH = N_PATTERNS * 3 * ATTN_WIDTH
N_GROUPS = 4
EXPERTS_PER_GROUP = 8
N_EXPERTS = N_GROUPS * EXPERTS_PER_GROUP
DEPTH = 1
ALPHA = (2.0 * DEPTH) ** 0.25
LN_EPS = 1e-5
NEG_INF = -1e30

LANES = 128
SUBLANES = 8
VMEM_LIMIT = 48 * 1024 * 1024

TM_POOL_PROJ = 512
TM_PATTERN_PROJ = 1024
TM_ROWWISE = 256
MOE_TILE = 256
HALO = 16
GATHER_UNROLL = 8


def _cparams(semantics):
    return pltpu.CompilerParams(dimension_semantics=semantics, vmem_limit_bytes=VMEM_LIMIT)


def _rope_tables(pos):
    inv_freq = ROPE_THETA ** (-jnp.arange(0, ROT_DIM, 2, dtype=F32) / ROT_DIM)
    ang = pos.astype(F32)[:, None] * inv_freq[None, :]
    cos, sin = jnp.cos(ang), jnp.sin(ang)
    t = pos.shape[0]
    rest = HEAD_DIM - ROT_DIM
    c64 = jnp.concatenate([cos, cos, jnp.ones((t, rest), F32)], -1)
    s64 = jnp.concatenate([-sin, sin, jnp.zeros((t, rest), F32)], -1)
    return jnp.tile(c64, (1, LANES // HEAD_DIM)), jnp.tile(s64, (1, LANES // HEAD_DIM))


def _rope_chunk(xc, cos, sin):
    half = ROT_DIM // 2
    lane = lax.broadcasted_iota(jnp.int32, xc.shape, 1) & (HEAD_DIM - 1)
    upper = pltpu.roll(xc, LANES - half, 1)
    lower = pltpu.roll(xc, half, 1)
    partner = jnp.where(lane < half, upper, lower)
    return xc * cos + partner * sin


def _split_bf16(a):
    hi = a.astype(BF16)
    return hi, (a - hi.astype(F32)).astype(BF16)


def _mm(a, w):
    if w.dtype == BF16:
        return jnp.dot(a.astype(BF16), w, preferred_element_type=F32)
    ah, al = _split_bf16(a.astype(F32))
    wh, wl = _split_bf16(w)
    return (jnp.dot(ah, wh, preferred_element_type=F32) + jnp.dot(al, wh, preferred_element_type=F32)
            + jnp.dot(ah, wl, preferred_element_type=F32))


ROW_PAD = 1


def _row_pitch(d):
    return d // LANES + ROW_PAD


def _store_row_major(ref, val):
    n, d = val.shape
    nc, pitch = d // LANES, _row_pitch(d)
    for c in range(nc):
        ref[pl.ds(c, n, stride=pitch), :] = val[:, c * LANES:(c + 1) * LANES]
    for c in range(nc, pitch):
        ref[pl.ds(c, n, stride=pitch), :] = jnp.zeros((n, LANES), F32)


def _load_row_major(ref, n):
    pitch = ref.shape[0] // n
    return jnp.concatenate([ref[pl.ds(c, n, stride=pitch), :] for c in range(pitch - ROW_PAD)], axis=1)


ROLE_TILES = 3


def _chunk(c):
    return slice(c * LANES, (c + 1) * LANES)


def _in_proj_rows_kernel(x_ref, w_ref, cos_ref, sin_ref, qkv_ref, u_ref):
    j = pl.program_id(0)
    n_qkv_tiles = N_PATTERNS * ROLE_TILES
    acc = _mm(x_ref[...], w_ref[...])
    role = j % ROLE_TILES
    is_qkv = j < n_qkv_tiles

    @pl.when(is_qkv & (role < 2))
    def _():
        cos = cos_ref[...]
        sin = sin_ref[...]
        for c in range(acc.shape[1] // LANES):
            qkv_ref[:, _chunk(c)] = _rope_chunk(acc[:, _chunk(c)], cos, sin)

    @pl.when(is_qkv & (role == 2))
    def _():
        qkv_ref[...] = acc

    @pl.when(j >= n_qkv_tiles)
    def _():
        u_ref[...] = acc


def _in_proj_rows(x, w, cos_t, sin_t):
    m, d = x.shape
    tn = ATTN_WIDTH
    assert w.shape[1] == QKV_WIDTH + POOL_WIDTH and POOL_WIDTH == tn
    n_qkv_tiles = N_PATTERNS * ROLE_TILES
    return pl.pallas_call(
        _in_proj_rows_kernel,
        grid=(n_qkv_tiles + 1,),
        in_specs=[
            pl.BlockSpec((m, d), lambda j: (0, 0)),
            pl.BlockSpec((d, tn), lambda j: (0, j)),
            pl.BlockSpec((m, LANES), lambda j: (0, 0)),
            pl.BlockSpec((m, LANES), lambda j: (0, 0)),
        ],
        out_specs=[pl.BlockSpec((m, tn), lambda j: (0, jnp.minimum(j, n_qkv_tiles - 1))),
                   pl.BlockSpec((m, tn), lambda j: (0, 0))],
        out_shape=[jax.ShapeDtypeStruct((m, QKV_WIDTH), F32), jax.ShapeDtypeStruct((m, POOL_WIDTH), F32)],
        compiler_params=_cparams(("arbitrary",)),
        name="in_proj_rows",
    )(x, w, cos_t, sin_t)


def _proj_pattern_kernel(x_ref, w_ref, cos_ref, sin_ref, out_ref, wb_ref, acc_ref, stage_ref, *, dil, n_rows):
    t = pl.program_id(0)
    n_chunks, tm, _ = stage_ref.shape

    @pl.when(t == 0)
    def _():
        acc_ref[...] = jnp.zeros(acc_ref.shape, F32)

    @pl.when(t % n_rows == 0)
    def _():
        wb_ref[...] = w_ref[...].astype(BF16)

    prev = acc_ref.at[(t + 1) % 2]
    cos = cos_ref[...]
    sin = sin_ref[...]
    for c in range(n_chunks):
        stage_ref[c] = _rope_chunk(prev[:, _chunk(c)], cos, sin)
    for r in range(dil):
        for c in range(n_chunks):
            rows = stage_ref[c, pl.ds(r, tm // dil, stride=dil), :] if dil > 1 else stage_ref[c]
            out_ref[r, :, _chunk(c)] = rows.astype(out_ref.dtype)
    acc_ref[t % 2] = jnp.dot(x_ref[...], wb_ref[...], preferred_element_type=F32)


def _proj_pattern(xb, w, tables, pi, dil, *, tm):
    m, d = xb.shape
    tn = ATTN_WIDTH
    assert m % tm == 0 and tm % (dil * 2 * SUBLANES) == 0
    n_rows = m // tm
    n = n_rows * ROLE_TILES

    def cur(t):
        return jnp.minimum(t, n - 1)

    def prv(t):
        return jnp.maximum(t - 1, 0)

    def table_map(t):
        identity = (prv(t) // n_rows == ROLE_TILES - 1).astype(jnp.int32)
        return (identity, prv(t) % n_rows, 0)

    return pl.pallas_call(
        functools.partial(_proj_pattern_kernel, dil=dil, n_rows=n_rows),
        grid=(n + 1,),
        in_specs=[
            pl.BlockSpec((tm, d), lambda t: (cur(t) % n_rows, 0)),
            pl.BlockSpec((d, tn), lambda t: (0, pi * ROLE_TILES + cur(t) // n_rows), pipeline_mode=pl.Buffered(1)),
            pl.BlockSpec((None, tm, LANES), table_map),
            pl.BlockSpec((None, tm, LANES), table_map),
        ],
        out_specs=pl.BlockSpec((dil, tm // dil, tn), lambda t: (0, prv(t) % n_rows, prv(t) // n_rows)),
        out_shape=jax.ShapeDtypeStruct((dil, m // dil, ROLE_TILES * tn), BF16),
        scratch_shapes=[pltpu.VMEM((d, tn), BF16), pltpu.VMEM((2, tm, tn), F32),
                        pltpu.VMEM((tn // LANES, tm, LANES), F32)],
        compiler_params=_cparams(("arbitrary",)),
        name=f"proj_p{pi}",
    )(xb, w, *tables)


def _proj_u_pool_kernel(x_ref, w_ref, w_pool_ref, scale_ref, pool_ref, xb_ref, utail_ref, wb_ref, buf_ref):
    t = pl.program_id(0)
    tm = x_ref.shape[0]

    @pl.when(t == 0)
    def _():
        wb_ref[...] = w_ref[...].astype(BF16)
        buf_ref[...] = jnp.zeros(buf_ref.shape, F32)

    prev = buf_ref.at[(t + 1) % 2]

    def store(g, val):
        pool_ref[:, g * POOL_GROUP:(g + 1) * POOL_GROUP] = val.astype(pool_ref.dtype)

    _pool_groups(prev, tm, jnp.maximum(t - 1, 0) * tm, w_pool_ref, scale_ref, store)

    xb = x_ref[...].astype(BF16)
    xb_ref[...] = xb
    u = jnp.dot(xb, wb_ref[...], preferred_element_type=F32)
    buf_ref[t % 2, HALO:HALO + tm, :] = u
    tail = u[tm - HALO:, :]
    utail_ref[...] = tail
    prev[0:HALO, :] = tail


def _proj_u_pool(x, w, w_pool_bf16, pool_scale, *, tm):
    m, d = x.shape
    assert m % tm == 0 and tm % HALO == 0
    n = m // tm
    n_qkv_tiles = N_PATTERNS * ROLE_TILES
    cur = lambda t: (jnp.minimum(t, n - 1), 0)
    prv = lambda t: (jnp.maximum(t - 1, 0), 0)
    return pl.pallas_call(
        _proj_u_pool_kernel,
        grid=(n + 1,),
        in_specs=[
            pl.BlockSpec((tm, d), cur),
            pl.BlockSpec((d, POOL_WIDTH), lambda t: (0, n_qkv_tiles), pipeline_mode=pl.Buffered(1)),
            pl.BlockSpec((len(POOL_WINDOWS), POOL_GROUP, POOL_GROUP), lambda t: (0, 0, 0)),
            pl.BlockSpec((1, POOL_WIDTH), lambda t: (0, 0)),
        ],
        out_specs=[
            pl.BlockSpec((tm, POOL_WIDTH), prv),
            pl.BlockSpec((tm, d), cur),
            pl.BlockSpec((HALO, POOL_WIDTH), lambda t: (0, 0)),
        ],
        out_shape=[
            jax.ShapeDtypeStruct((m, POOL_WIDTH), BF16),
            jax.ShapeDtypeStruct((m, d), BF16),
            jax.ShapeDtypeStruct((HALO, POOL_WIDTH), F32),
        ],
        scratch_shapes=[pltpu.VMEM((d, POOL_WIDTH), BF16), pltpu.VMEM((2, HALO + tm, POOL_WIDTH), F32)],
        compiler_params=_cparams(("arbitrary",)),
        name="proj_u_pool",
    )(x, w, w_pool_bf16, pool_scale)


def _attn_kernel(q_ref, kp_ref, kc_ref, vp_ref, vc_ref, o_ref, lse_ref,
                 lim_ref, s_ref, e_ref, m_ref, den_ref, *, r_max, dil):
    n = pl.program_id(0)
    r = pl.program_id(1)
    qi = lax.broadcasted_iota(jnp.int32, (BAND, 2 * BAND), 0)
    ki = lax.broadcasted_iota(jnp.int32, (BAND, 2 * BAND), 1)
    dist = BAND + qi - ki
    valid = (dist >= 0) & (dist <= r_max) & ((n > 0) | (ki >= BAND))
    lim_ref[...] = jnp.where(valid, jnp.float32(jnp.finfo(F32).max), jnp.float32(NEG_INF))

    lane = lax.broadcasted_iota(jnp.int32, (BAND, LANES), 1)
    heads_per_chunk = LANES // HEAD_DIM
    n_chunks = ATTN_WIDTH // LANES
    lane_head = lane >> (HEAD_DIM.bit_length() - 1)
    qscale = [jnp.where(lane_head == hh, HEAD_DIM ** -0.5, 0.0).astype(BF16) for hh in range(heads_per_chunk)]
    nt = (((1,), (1,)), ((), ()))

    for c in range(n_chunks):
        sl = slice(c * LANES, (c + 1) * LANES)
        q2 = q_ref[:, sl]
        kp, kc = kp_ref[:, sl], kc_ref[:, sl]
        for hh in range(heads_per_chunk):
            h = c * heads_per_chunk + hh
            qm = q2 * qscale[hh]
            s = jnp.concatenate(
                [lax.dot_general(qm, kp, nt, preferred_element_type=F32),
                 lax.dot_general(qm, kc, nt, preferred_element_type=F32)], axis=1)
            s = jnp.minimum(s, lim_ref[...])
            s_ref[h] = s
            m_ref[h] = jnp.broadcast_to(jnp.max(s, axis=-1, keepdims=True), (BAND, LANES))

    lse_acc = jnp.zeros((BAND, LANES), F32)
    for h in range(N_HEADS):
        m = m_ref[h]
        e = jnp.exp(s_ref[h] - jnp.concatenate([m, m], axis=1))
        e_ref[h] = e.astype(BF16)
        den = jnp.broadcast_to(jnp.sum(e, axis=-1, keepdims=True), (BAND, LANES))
        den_ref[h] = den
        lse_acc = jnp.where(lane == h, m + jnp.log(den), lse_acc)

    rows = slice(None) if dil == 1 else pl.ds(r, BAND, stride=dil)
    lse_ref[rows, :] = lse_acc
    for c in range(n_chunks):
        sl = slice(c * LANES, (c + 1) * LANES)
        vp, vc = vp_ref[:, sl], vc_ref[:, sl]
        o2 = jnp.zeros((BAND, LANES), F32)
        for hh in range(heads_per_chunk):
            h = c * heads_per_chunk + hh
            pv = (jnp.dot(e_ref[h, :, :BAND], vp, preferred_element_type=F32)
                  + jnp.dot(e_ref[h, :, BAND:], vc, preferred_element_type=F32))
            o2 = jnp.where(lane_head == hh, pv / den_ref[h], o2)
        o_ref[c, rows, :] = o2


def _attn_pattern(qkv, pi, dil, window):
    s = qkv.shape[0] * qkv.shape[1]
    assert qkv.shape[0] == dil and s % (dil * BAND) == 0
    nb = s // (dil * BAND)
    r_max = window // dil

    def cur(role):
        return lambda n, r: (r, n, role)

    def prev(role):
        return lambda n, r: (r, jnp.maximum(n - 1, 0), role)

    blk = (None, BAND, ATTN_WIDTH)
    span = BAND * dil
    return pl.pallas_call(
        functools.partial(_attn_kernel, r_max=r_max, dil=dil),
        grid=(nb, dil),
        in_specs=[
            pl.BlockSpec(blk, cur(0)),
            pl.BlockSpec(blk, prev(1)),
            pl.BlockSpec(blk, cur(1)),
            pl.BlockSpec(blk, prev(2)),
            pl.BlockSpec(blk, cur(2)),
        ],
        out_specs=[
            pl.BlockSpec((ATTN_WIDTH // LANES, span, LANES), lambda n, r: (0, n, 0)),
            pl.BlockSpec((span, LANES), lambda n, r: (n, 0)),
        ],
        out_shape=[
            jax.ShapeDtypeStruct((ATTN_WIDTH // LANES, s, LANES), F32),
            jax.ShapeDtypeStruct((s, LANES), F32),
        ],
        scratch_shapes=[
            pltpu.VMEM((BAND, 2 * BAND), F32),
            pltpu.VMEM((N_HEADS, BAND, 2 * BAND), F32),
            pltpu.VMEM((N_HEADS, BAND, 2 * BAND), BF16),
            pltpu.VMEM((N_HEADS, BAND, LANES), F32),
            pltpu.VMEM((N_HEADS, BAND, LANES), F32),
        ],
        compiler_params=_cparams(("arbitrary", "arbitrary")),
        name=f"attn_p{pi}",
    )(qkv, qkv, qkv, qkv, qkv)


def _pool_groups(buf_ref, tm, pos0, w_pool_ref, scale_ref, store):
    row = lax.broadcasted_iota(jnp.int32, (tm, 1), 0) + pos0
    for g, win in enumerate(POOL_WINDOWS):
        sl = slice(g * POOL_GROUP, (g + 1) * POOL_GROUP)
        cur = buf_ref[HALO:HALO + tm, sl]
        acc = cur
        for jj in range(1, win):
            acc = acc + buf_ref[HALO - jj:HALO - jj + tm, sl]
        cnt = jnp.minimum(row + 1, win).astype(F32)
        d = acc / cnt - cur
        y = _mm(d, w_pool_ref[g])
        store(g, y * scale_ref[:, sl])


def _sample_attn_kernel(zt_ref, c_ref, cn_ref, ot_ref, lt_ref, s_ref, *, dil, hps, length):
    b = pl.program_id(0)
    g = pl.program_id(1)

    @pl.when((b == 0) & (g == 0))
    def _():
        ot_ref[...] = jnp.zeros(ot_ref.shape, F32)
        lt_ref[...] = jnp.zeros(lt_ref.shape, F32)

    rows = hps * HEAD_DIM
    base = pl.multiple_of(g * rows, rows)
    lane_z = lax.broadcasted_iota(jnp.int32, (rows, zt_ref.shape[1]), 1)

    def column(role):
        blk = zt_ref[pl.ds(role * ATTN_WIDTH + base, rows), :]
        return jnp.sum(jnp.where(lane_z == b, blk, 0.0), axis=-1, keepdims=True)

    qc, knc, vnc = column(0), column(1), column(2)
    scale = HEAD_DIM ** -0.5
    lane = lax.broadcasted_iota(jnp.int32, (1, length), 1)
    key_ok = (lane & (dil - 1)) == 0
    last = lane == length - 1
    lane_o = lax.broadcasted_iota(jnp.int32, (HEAD_DIM, LANES), 1)
    lane_l = lax.broadcasted_iota(jnp.int32, (hps, LANES), 1)

    def head_rows(h):
        return slice(h * HEAD_DIM, (h + 1) * HEAD_DIM)

    for h in range(hps):
        k = c_ref[0, h]
        s_ref[h:h + 1, :] = jnp.sum(k * qc[head_rows(h)], axis=0, keepdims=True)
        cn_ref[0, h] = jnp.where(last, knc[head_rows(h)], pltpu.roll(k, length - 1, 1))

    s = jnp.where(key_ok, s_ref[...] * scale, NEG_INF)
    s_n = jnp.sum((knc * qc).reshape(hps, HEAD_DIM, 1), axis=1) * scale
    m = jnp.maximum(jnp.max(s, axis=-1, keepdims=True), s_n)
    e = jnp.exp(s - m)
    e_n = jnp.exp(s_n - m)
    den = jnp.sum(e, axis=-1, keepdims=True) + e_n
    s_ref[...] = e
    lse = m + jnp.log(den)
    head0 = pl.multiple_of(g * hps, hps)
    lt_ref[pl.ds(head0, hps), :] = jnp.where(lane_l == b, lse, lt_ref[pl.ds(head0, hps), :])

    for h in range(hps):
        v = c_ref[1, h]
        vn = vnc[head_rows(h)]
        o = (jnp.sum(v * s_ref[h:h + 1, :], axis=-1, keepdims=True) + vn * e_n[h:h + 1, :]) / den[h:h + 1, :]
        cn_ref[1, h] = jnp.where(last, vn, pltpu.roll(v, length - 1, 1))
        r0 = pl.multiple_of(base + h * HEAD_DIM, HEAD_DIM)
        ot_ref[pl.ds(r0, HEAD_DIM), :] = jnp.where(lane_o == b, o, ot_ref[pl.ds(r0, HEAD_DIM), :])


def _sample_attn(zt, c_t, pi, dil):
    db, _, _, _, length = c_t.shape
    assert db <= LANES and length == dil * BAND
    hps = min(N_HEADS, max(1, (4 * 2048) // length))
    blk = (None, 2, hps, HEAD_DIM, length)
    return pl.pallas_call(
        functools.partial(_sample_attn_kernel, dil=dil, hps=hps, length=length),
        grid=(db, N_HEADS // hps),
        in_specs=[
            pl.BlockSpec((3 * ATTN_WIDTH, db), lambda b, g: (pi, 0)),
            pl.BlockSpec(blk, lambda b, g: (b, 0, g, 0, 0)),
        ],
        out_specs=[
            pl.BlockSpec(blk, lambda b, g: (b, 0, g, 0, 0)),
            pl.BlockSpec((ATTN_WIDTH, LANES), lambda b, g: (0, 0)),
            pl.BlockSpec((LANES, LANES), lambda b, g: (0, 0)),
        ],
        out_shape=[
            jax.ShapeDtypeStruct(c_t.shape, c_t.dtype),
            jax.ShapeDtypeStruct((ATTN_WIDTH, LANES), F32),
            jax.ShapeDtypeStruct((LANES, LANES), F32),
        ],
        scratch_shapes=[pltpu.VMEM((hps, length), F32)],
        compiler_params=_cparams(("arbitrary", "arbitrary")),
        name=f"sample_attn_p{pi}",
    )(zt, c_t)


def _sample_pool_kernel(u_ref, sp_ref, w_pool_ref, scale_ref, pool_ref, buf_ref):
    buf_ref[...] = jnp.zeros(buf_ref.shape, F32)
    buf_ref[HALO - POOL_STATE:HALO, :] = sp_ref[...]
    buf_ref[HALO:HALO + 1, :] = u_ref[...]

    def store(g, val):
        pool_ref[:, g * POOL_GROUP:(g + 1) * POOL_GROUP] = val[0:1]

    _pool_groups(buf_ref, HALO, PAST_LEN, w_pool_ref, scale_ref, store)


def _head_expand():
    dim_head = jnp.arange(ATTN_WIDTH, dtype=jnp.int32) // HEAD_DIM
    return (jnp.arange(LANES, dtype=jnp.int32)[:, None] == dim_head[None, :]).astype(F32)


def _sample_pool(us, state_pool, w_pool_bf16, pool_scale):
    db = us.shape[0]
    pool = pl.pallas_call(
        _sample_pool_kernel,
        grid=(db,),
        in_specs=[
            pl.BlockSpec((None, 1, POOL_WIDTH), lambda b: (b, 0, 0)),
            pl.BlockSpec((None, POOL_STATE, POOL_WIDTH), lambda b: (b, 0, 0)),
            pl.BlockSpec((len(POOL_WINDOWS), POOL_GROUP, POOL_GROUP), lambda b: (0, 0, 0)),
            pl.BlockSpec((1, POOL_WIDTH), lambda b: (0, 0)),
        ],
        out_specs=pl.BlockSpec((None, 1, POOL_WIDTH), lambda b: (b, 0, 0)),
        out_shape=jax.ShapeDtypeStruct((db, 1, POOL_WIDTH), F32),
        scratch_shapes=[pltpu.VMEM((2 * HALO, POOL_WIDTH), F32)],
        compiler_params=_cparams(("arbitrary",)),
        name="sample_pool",
    )(us.reshape(db, 1, POOL_WIDTH), state_pool, w_pool_bf16, pool_scale)
    return pool.reshape(db, POOL_WIDTH)


def _layer_norm(xf, g, b):
    mu = jnp.mean(xf, axis=-1, keepdims=True)
    xc = xf - mu
    var = jnp.mean(xc * xc, axis=-1, keepdims=True)
    return xc * lax.rsqrt(var + LN_EPS) * g + b


def _route(logits):
    lane = lax.broadcasted_iota(jnp.int32, logits.shape, 1)
    lane_f = lane.astype(F32)
    neg = jnp.float32(-jnp.inf)

    def first_argmax(v, vmax):
        return jnp.min(jnp.where(v == vmax, lane_f, jnp.float32(LANES)), axis=-1, keepdims=True)

    gl = jnp.where(lane < N_GROUPS, logits, neg)
    gmax = jnp.max(gl, axis=-1, keepdims=True)
    g_sel = first_argmax(gl, gmax)
    g_gate = 1.0 / jnp.sum(jnp.exp(gl - gmax), axis=-1, keepdims=True)
    lane_group = (((lane + (EXPERTS_PER_GROUP - N_GROUPS)) >> 3) - 1).astype(F32)
    el = jnp.where(lane_group == g_sel, logits, neg)
    v1 = jnp.max(el, axis=-1, keepdims=True)
    i1 = first_argmax(el, v1)
    el2 = jnp.where(lane_f == i1, neg, el)
    v2 = jnp.max(el2, axis=-1, keepdims=True)
    i2 = first_argmax(el2, v2)
    t = jnp.exp(v2 - v1)
    w1 = (1.0 / (1.0 + t)) * g_gate
    w2 = (t / (1.0 + t)) * g_gate
    route = jnp.where(lane == 0, i1 - N_GROUPS, 0.0)
    route = jnp.where(lane == 1, i2 - N_GROUPS, route)
    route = jnp.where(lane == 2, w1, route)
    route = jnp.where(lane == 3, w2, route)
    return route


def _post_kernel(o0_ref, o1_ref, o2_ref, l0_ref, l1_ref, l2_ref, pool_ref, x_ref, hexp_ref,
                 wo_ref, g1_ref, b1_ref, wr_ref, br_ref, *rest, n_row_tiles):
    x1_ref, route_ref, wrh_ref, wrl_ref = rest[-4:]
    i = pl.program_id(0)

    @pl.when(i == 0)
    def _():
        wrh_ref[...], wrl_ref[...] = _split_bf16(wr_ref[...])

    @pl.when(i < n_row_tiles)
    def _():
        l0, l1, l2 = l0_ref[...], l1_ref[...], l2_ref[...]
        lmax = jnp.maximum(jnp.maximum(l0, l1), l2)
        e0, e1, e2 = jnp.exp(l0 - lmax), jnp.exp(l1 - lmax), jnp.exp(l2 - lmax)
        den = e0 + e1 + e2
        hexp = hexp_ref[...]
        attn = jnp.zeros((x_ref.shape[0], ATTN_WIDTH), F32)
        for o_ref, e in ((o0_ref, e0), (o1_ref, e1), (o2_ref, e2)):
            wh, wl = _split_bf16(e / den)
            w_x = (jnp.dot(wh, hexp, preferred_element_type=F32) + jnp.dot(wl, hexp, preferred_element_type=F32))
            o = jnp.concatenate([o_ref[c] for c in range(ATTN_WIDTH // LANES)], axis=1)
            attn = attn + o * w_x
        h = _mm(attn, wo_ref[0:ATTN_WIDTH, :]) + _mm(pool_ref[...], wo_ref[ATTN_WIDTH:, :])
        x1 = _layer_norm(ALPHA * x_ref[...] + h, g1_ref[...], b1_ref[...])
        _store_row_major(x1_ref, x1)
        xh, xl = _split_bf16(x1)
        wrh = wrh_ref[...]
        logits = (jnp.dot(xh, wrh, preferred_element_type=F32) + jnp.dot(xl, wrh, preferred_element_type=F32)
                  + jnp.dot(xh, wrl_ref[...], preferred_element_type=F32)) + br_ref[...]
        route_ref[...] = _route(logits)

    @pl.when(i >= n_row_tiles)
    def _():
        x1_ref[...] = jnp.zeros(x1_ref.shape, F32)


def _const_spec(shape):
    return pl.BlockSpec(shape, lambda i: (0,) * len(shape), pipeline_mode=pl.Buffered(1))


def _post(os_, lses, pool, x, wo, g1, b1, wr, br, *, tm, name, filler_tiles=0, x1_into=None, row0=0):
    m, d = x.shape
    n = m // tm
    pitch = _row_pitch(d)
    assert (tm * pitch) % SUBLANES == 0
    hexp = _head_expand().astype(BF16)
    last = n - 1

    def rows(width):
        return pl.BlockSpec((tm, width), lambda i: (jnp.minimum(i, last), 0))

    in_specs = ([pl.BlockSpec((ATTN_WIDTH // LANES, tm, LANES), lambda i: (0, jnp.minimum(i, last), 0))] * 3
                + [rows(LANES)] * 3
                + [rows(POOL_WIDTH), rows(d), _const_spec((LANES, ATTN_WIDTH)),
                   _const_spec((d, d)), _const_spec((1, d)), _const_spec((1, d)),
                   _const_spec((d, LANES)), _const_spec((1, LANES))])
    operands = [*os_, *lses, pool, x, hexp, wo, g1, b1, wr, br]
    aliases = {}
    if x1_into is None:
        x1_shape = jax.ShapeDtypeStruct(((m + filler_tiles * tm) * pitch, LANES), F32)
    else:
        assert filler_tiles == 0 and row0 % tm == 0
        in_specs.append(pl.BlockSpec(memory_space=pl.ANY))
        operands.append(x1_into)
        aliases = {len(operands) - 1: 0}
        x1_shape = jax.ShapeDtypeStruct(x1_into.shape, F32)
    return pl.pallas_call(
        functools.partial(_post_kernel, n_row_tiles=n),
        grid=(n + filler_tiles,),
        in_specs=in_specs,
        out_specs=[pl.BlockSpec((tm * pitch, LANES), lambda i: (row0 // tm + i, 0)), rows(LANES)],
        out_shape=[x1_shape, jax.ShapeDtypeStruct((m, LANES), F32)],
        input_output_aliases=aliases,
        scratch_shapes=[pltpu.VMEM((d, LANES), BF16), pltpu.VMEM((d, LANES), BF16)],
        compiler_params=_cparams(("arbitrary",)),
        name=name,
    )(*operands)


def _routing_plan(pair_expert, n_tiles):
    p = pair_expert.shape[0]
    experts = jnp.arange(N_EXPERTS, dtype=jnp.int32)
    onehot = (pair_expert[:, None] == experts[None, :]).astype(jnp.int32)
    csum = jnp.cumsum(onehot, axis=0)
    rank = jnp.take_along_axis(csum, pair_expert[:, None], axis=1)[:, 0] - 1
    counts = csum[-1]
    tiles_per = (counts + MOE_TILE - 1) // MOE_TILE
    tile_end = jnp.cumsum(tiles_per)
    tile_start = tile_end - tiles_per
    dest = (tile_start[pair_expert] * MOE_TILE + rank).astype(jnp.int32)
    n_used = tile_end[-1]
    tile_ids = jnp.arange(n_tiles, dtype=jnp.int32)
    tile_expert = jnp.sum((tile_ids[:, None] >= tile_end[None, :]).astype(jnp.int32), axis=1)
    last_expert = jnp.sum((n_used - 1 >= tile_end).astype(jnp.int32))
    tile_expert = jnp.where(tile_ids < n_used, tile_expert, last_expert).astype(jnp.int32)
    slot_token = jnp.zeros((n_tiles * MOE_TILE,), jnp.int32).at[dest].set(
        jnp.arange(p, dtype=jnp.int32) // 2, unique_indices=True, mode="promise_in_bounds")
    return dest, slot_token, tile_expert, n_used.reshape(1).astype(jnp.int32)


def _moe_kernel(te_ref, nu_ref, st_ref, x_ref, wg_ref, wu_ref, wd_ref, ys_ref,
                buf_ref, wgb_ref, wub_ref, wdb_ref, sem_ref, *, nc, pitch):
    c = pl.program_id(0)
    n_tiles = pl.num_programs(0)
    n_used = nu_ref[0]

    def issue(tile, slot):
        def body(g, carry):
            for k in range(GATHER_UNROLL):
                i = g * GATHER_UNROLL + k
                src = st_ref[tile * MOE_TILE + i]
                pltpu.make_async_copy(x_ref.at[pl.ds(src, nc), :], buf_ref.at[slot, pl.ds(i * pitch, nc), :],
                                      sem_ref.at[slot]).start(priority=k % 2)
            return carry

        lax.fori_loop(0, MOE_TILE // GATHER_UNROLL, body, 0)

    def wait(slot):
        pltpu.make_async_copy(x_ref.at[pl.ds(0, MOE_TILE * nc), :], buf_ref.at[slot, pl.ds(0, MOE_TILE * nc), :],
                              sem_ref.at[slot]).wait()

    slot = c % 2

    @pl.when((c == 0) & (n_used > 0))
    def _():
        issue(0, 0)

    @pl.when((c + 1 < n_tiles) & (c + 1 < n_used))
    def _():
        issue(c + 1, 1 - slot)

    prev_expert = te_ref[jnp.maximum(c - 1, 0)]

    @pl.when((c == 0) | (te_ref[c] != prev_expert))
    def _():
        wgb_ref[...] = wg_ref[...].astype(BF16)
        wub_ref[...] = wu_ref[...].astype(BF16)
        wdb_ref[...] = wd_ref[...].astype(BF16)

    @pl.when(c < n_used)
    def _():
        wait(slot)
        x = _load_row_major(buf_ref.at[slot], MOE_TILE).astype(BF16)
        gate = jnp.dot(x, wgb_ref[...], preferred_element_type=F32)
        up = jnp.dot(x, wub_ref[...], preferred_element_type=F32)
        h = jax.nn.silu(gate) * up
        _store_row_major(ys_ref, jnp.dot(h.astype(BF16), wdb_ref[...], preferred_element_type=F32))

    @pl.when(c >= n_used)
    def _():
        ys_ref[...] = jnp.zeros(ys_ref.shape, ys_ref.dtype)


def _moe(x1, slot_row, tile_expert, n_used, w_gate, w_up, w_down, n_tiles):
    d = w_gate.shape[1]
    nc, pitch = d // LANES, _row_pitch(d)
    f = w_gate.shape[-1]
    grid_spec = pltpu.PrefetchScalarGridSpec(
        num_scalar_prefetch=3,
        grid=(n_tiles,),
        in_specs=[
            pl.BlockSpec(memory_space=pl.ANY),
            pl.BlockSpec((None, d, f), lambda c, te, nu, st: (te[c], 0, 0)),
            pl.BlockSpec((None, d, f), lambda c, te, nu, st: (te[c], 0, 0)),
            pl.BlockSpec((None, f, d), lambda c, te, nu, st: (te[c], 0, 0)),
        ],
        out_specs=pl.BlockSpec((MOE_TILE * pitch, LANES), lambda c, te, nu, st: (c, 0)),
        scratch_shapes=[
            pltpu.VMEM((2, MOE_TILE * pitch, LANES), F32),
            pltpu.VMEM((d, f), BF16),
            pltpu.VMEM((d, f), BF16),
            pltpu.VMEM((f, d), BF16),
            pltpu.SemaphoreType.DMA((2,)),
        ],
    )
    return pl.pallas_call(
        functools.partial(_moe_kernel, nc=nc, pitch=pitch),
        grid_spec=grid_spec,
        out_shape=jax.ShapeDtypeStruct((n_tiles * MOE_TILE * pitch, LANES), F32),
        compiler_params=_cparams(("arbitrary",)),
        name="moe",
    )(tile_expert, n_used, slot_row, x1, w_gate, w_up, w_down)


def _final_kernel(dest_ref, ys_ref, x1_ref, route_ref, p_ref, g2_ref, b2_ref, wpg_ref, wp_ref, o_ref,
                  buf_ref, sem_ref, *, tm, pair0, nc, pitch):
    i = pl.program_id(0)
    n_steps = pl.num_programs(0)

    def issue(step, slot):
        def body(r, carry):
            for k in range(2):
                src = dest_ref[pair0 + (step * tm + r) * 2 + k]
                pltpu.make_async_copy(ys_ref.at[pl.ds(src, nc), :], buf_ref.at[slot, k, pl.ds(r * pitch, nc), :],
                                      sem_ref.at[slot]).start()
            return carry

        lax.fori_loop(0, tm, body, 0, unroll=GATHER_UNROLL)

    def wait(slot):
        for k in range(2):
            pltpu.make_async_copy(ys_ref.at[pl.ds(0, tm * nc), :], buf_ref.at[slot, k, pl.ds(0, tm * nc), :],
                                  sem_ref.at[slot]).wait()

    slot = i % 2

    @pl.when(i == 0)
    def _():
        issue(0, 0)

    @pl.when(i + 1 < n_steps)
    def _():
        issue(i + 1, 1 - slot)

    wait(slot)
    route = route_ref[...]
    y = (route[:, 2:3] * _load_row_major(buf_ref.at[slot, 0], tm)
         + route[:, 3:4] * _load_row_major(buf_ref.at[slot, 1], tm))
    x2 = _layer_norm(ALPHA * _load_row_major(x1_ref, tm) + y, g2_ref[...], b2_ref[...])
    gate = jax.nn.sigmoid(jnp.dot(x2.astype(BF16), wpg_ref[...], preferred_element_type=F32))
    ple = jnp.dot(p_ref[...].astype(BF16), wp_ref[...], preferred_element_type=F32)
    o_ref[...] = x2 + gate * ple


def _final(dest, ys, x1, route, p, g2, b2, wpg, wp, *, tm, row0):
    m, pd = p.shape
    d = wpg.shape[0]
    nc, pitch = d // LANES, _row_pitch(d)
    assert m % tm == 0 and row0 % tm == 0
    pair0 = 2 * row0
    grid_spec = pltpu.PrefetchScalarGridSpec(
        num_scalar_prefetch=1,
        grid=(m // tm,),
        in_specs=[
            pl.BlockSpec(memory_space=pl.ANY),
            pl.BlockSpec((tm * pitch, LANES), lambda i, dst: (row0 // tm + i, 0)),
            pl.BlockSpec((tm, LANES), lambda i, dst: (i, 0)),
            pl.BlockSpec((tm, pd), lambda i, dst: (i, 0)),
            pl.BlockSpec((1, d), lambda i, dst: (0, 0)),
            pl.BlockSpec((1, d), lambda i, dst: (0, 0)),
            pl.BlockSpec((d, d), lambda i, dst: (0, 0)),
            pl.BlockSpec((pd, d), lambda i, dst: (0, 0)),
        ],
        out_specs=pl.BlockSpec((tm, d), lambda i, dst: (i, 0)),
        scratch_shapes=[pltpu.VMEM((2, 2, tm * pitch, LANES), F32), pltpu.SemaphoreType.DMA((2,))],
    )
    return pl.pallas_call(
        functools.partial(_final_kernel, tm=tm, pair0=pair0, nc=nc, pitch=pitch),
        grid_spec=grid_spec,
        out_shape=jax.ShapeDtypeStruct((m, d), F32),
        compiler_params=_cparams(("arbitrary",)),
        name="final",
    )(dest, ys, x1, route, p, g2, b2, wpg, wp)


def _pick_tile(m, pref):
    t = pref
    while m % t:
        t //= 2
    return t


def _layer(xp, xs, caches, state_pool, pp, ps, w):
    s, d = xp.shape
    db = xs.shape[0]

    cos_p, sin_p = _rope_tables(jnp.arange(s, dtype=jnp.int32))
    dils = tuple(dil for _, dil in DILATED_PATTERNS)
    tables = (jnp.stack([cos_p, jnp.ones_like(cos_p)]), jnp.stack([sin_p, jnp.zeros_like(sin_p)]))
    tm_in = _pick_tile(s, TM_POOL_PROJ)
    pool_p, xp_b, u_tail = _proj_u_pool(xp, w["w_in_f32"], w["w_pool"], w["pool_scale"], tm=tm_in)
    tm_pat = _pick_tile(s, TM_PATTERN_PROJ)
    qkv_p = [_proj_pattern(xp_b, w["w_in_f32"], tables, pi, dil, tm=tm_pat) for pi, dil in enumerate(dils)]
    os_, lses = [], []
    for pi, (window, dil) in enumerate(DILATED_PATTERNS):
        o, l = _attn_pattern(qkv_p[pi], pi, dil, window)
        os_.append(o)
        lses.append(l)

    cos_s, sin_s = _rope_tables(jnp.full((db,), PAST_LEN, jnp.int32))
    qkv_s, u_s = _in_proj_rows(xs, w["w_in_f32"], cos_s, sin_s)
    zt_s = qkv_s.T
    os_s, lses_s, kv_s = [], [], []
    for pi, ((window, dil), c) in enumerate(zip(DILATED_PATTERNS, caches)):
        cn, ot, lt = _sample_attn(zt_s, jnp.transpose(c, (0, 2, 3, 4, 1)), pi, dil)
        kv_s.append(jnp.transpose(cn, (0, 4, 1, 2, 3)))
        os_s.append(jnp.transpose(ot.T[:db].reshape(db, ATTN_WIDTH // LANES, LANES), (1, 0, 2)))
        lses_s.append(lt.T[:db])
    pool_s = _sample_pool(u_s, state_pool, w["w_pool_f32"], w["pool_scale"])

    post_w = (w["ln1_g"], w["ln1_b"], w["w_router"], w["b_router"])
    tm_p = _pick_tile(s, TM_ROWWISE)
    assert s % db == 0 and tm_p >= db
    x1, route_p = _post(os_, lses, pool_p, xp, w["w_out"], *post_w, tm=tm_p, name="post_prompt",
                        filler_tiles=1)
    x1, route_s = _post(os_s, lses_s, pool_s, xs, w["w_out_f32"], *post_w, tm=db, name="post_sample",
                        x1_into=x1, row0=s)

    pair_expert = jnp.concatenate([route_p[:, 0:2].reshape(-1), route_s[:, 0:2].reshape(-1)]).astype(jnp.int32)
    n_pairs = pair_expert.shape[0]
    n_tiles = -(-n_pairs // MOE_TILE) + N_EXPERTS
    dest, slot_token, tile_expert, n_used = _routing_plan(pair_expert, n_tiles)
    pitch = _row_pitch(d)
    ys = _moe(x1, slot_token * pitch, tile_expert, n_used, w["w_gate"], w["w_up"], w["w_down"], n_tiles)

    fin_w = (w["ln2_g"], w["ln2_b"], w["w_ple_gate"], w["w_ple"])
    y_p = _final(dest * pitch, ys, x1, route_p, pp, *fin_w, tm=tm_p, row0=0)
    y_s = _final(dest * pitch, ys, x1, route_s, ps, *fin_w, tm=db, row0=s)

    kv_p = []
    for pi, (window, dil) in enumerate(DILATED_PATTERNS):
        keep = min(window, s)
        kv = qkv_p[pi][:, (s - keep) // dil:, ATTN_WIDTH:3 * ATTN_WIDTH].astype(F32)
        kv = jnp.transpose(kv, (1, 0, 2))
        kv_p.append(kv.reshape(1, keep, 2, N_HEADS, HEAD_DIM))
    pool_state_p = u_tail[HALO - POOL_STATE:][None]
    pool_state_s = jnp.concatenate([state_pool[:, 1:], u_s[:, None, :]], axis=1)
    return y_p, y_s, kv_p, pool_state_p, kv_s, pool_state_s


def kernel(x_prompt, x_sample, cache_kv_w128_d1, cache_kv_w512_d4, cache_kv_w2048_d16, state_pool, p_prompt, p_sample, w_in, w_out, w_pool, pool_scale, ln1_g, ln1_b, w_group_router, b_group_router, w_expert_router, b_expert_router, w_gate, w_up, w_down, ln2_g, ln2_b, w_ple, w_ple_gate):
    assert w_in.shape[0] == DEPTH == 1 and x_prompt.shape[0] == 1 and x_sample.shape[1] == 1
    d = x_prompt.shape[-1]
    pad = LANES - N_GROUPS - N_EXPERTS
    w = {
        "w_in_f32": w_in[0],
        "w_out": w_out[0].astype(BF16), "w_out_f32": w_out[0],
        "w_pool": w_pool[0].astype(BF16), "w_pool_f32": w_pool[0],
        "pool_scale": pool_scale[0].reshape(1, POOL_WIDTH),
        "ln1_g": ln1_g[0].reshape(1, d), "ln1_b": ln1_b[0].reshape(1, d),
        "ln2_g": ln2_g[0].reshape(1, d), "ln2_b": ln2_b[0].reshape(1, d),
        "w_router": jnp.concatenate([w_group_router[0], w_expert_router[0], jnp.zeros((d, pad), F32)], axis=1),
        "b_router": jnp.concatenate([b_group_router[0], b_expert_router[0], jnp.zeros((pad,), F32)]).reshape(1, LANES),
        "w_gate": w_gate[0], "w_up": w_up[0], "w_down": w_down[0],
        "w_ple": w_ple[0].astype(BF16),
        "w_ple_gate": w_ple_gate[0].astype(BF16),
    }
    caches = [cache_kv_w128_d1[0], cache_kv_w512_d4[0], cache_kv_w2048_d16[0]]
    y_p, y_s, kv_p, pool_p, kv_s, pool_s = _layer(
        x_prompt[0], x_sample[:, 0], caches, state_pool[0], p_prompt[0, 0], p_sample[0, :, 0], w)
    return (y_p[None], y_s[:, None], kv_p[0][None], kv_p[1][None], kv_p[2][None], pool_p[None],
            kv_s[0][None], kv_s[1][None], kv_s[2][None], pool_s[None])
```

```python
import functools

import jax
import jax.numpy as jnp
from jax import lax
from jax.experimental import pallas as pl
from jax.experimental.pallas import tpu as pltpu

F32 = jnp.float32
BF16 = jnp.bfloat16

PAST_LEN = 8192
HEAD_DIM = 64
N_HEADS = 16
ATTN_WIDTH = N_HEADS * HEAD_DIM
DILATED_PATTERNS = ((128, 1), (512, 4), (2048, 16))
N_PATTERNS = len(DILATED_PATTERNS)
BAND = 128
POOL_WINDOWS = (2, 4, 8, 16)
POOL_GROUP = 256
POOL_WIDTH = POOL_GROUP * len(POOL_WINDOWS)
POOL_STATE = max(POOL_WINDOWS) - 1
ROT_DIM = HEAD_DIM // 4
ROPE_THETA = 500000.0
QKV_WIDTH = N_PATTERNS * 3 * ATTN_WIDTH
N_GROUPS = 4
EXPERTS_PER_GROUP = 8
N_EXPERTS = N_GROUPS * EXPERTS_PER_GROUP
DEPTH = 1
ALPHA = (2.0 * DEPTH) ** 0.25
LN_EPS = 1e-5
NEG_INF = -1e30

LANES = 128
SUBLANES = 8
VMEM_LIMIT = 48 * 1024 * 1024

TM_POOL_PROJ = 512
TM_PATTERN_PROJ = 512
TM_ROWWISE = 256
MOE_TILE = 256
HALO = 16
GATHER_UNROLL = 8


def _cparams(semantics):
    return pltpu.CompilerParams(dimension_semantics=semantics, vmem_limit_bytes=VMEM_LIMIT)


def _rope_tables(pos):
    inv_freq = ROPE_THETA ** (-jnp.arange(0, ROT_DIM, 2, dtype=F32) / ROT_DIM)
    ang = pos.astype(F32)[:, None] * inv_freq[None, :]
    cos, sin = jnp.cos(ang), jnp.sin(ang)
    t = pos.shape[0]
    rest = HEAD_DIM - ROT_DIM
    c64 = jnp.concatenate([cos, cos, jnp.ones((t, rest), F32)], -1)
    s64 = jnp.concatenate([-sin, sin, jnp.zeros((t, rest), F32)], -1)
    return jnp.tile(c64, (1, LANES // HEAD_DIM)), jnp.tile(s64, (1, LANES // HEAD_DIM))


def _rope_chunk(xc, cos, sin):
    half = ROT_DIM // 2
    lane = lax.broadcasted_iota(jnp.int32, xc.shape, 1) & (HEAD_DIM - 1)
    upper = pltpu.roll(xc, LANES - half, 1)
    lower = pltpu.roll(xc, half, 1)
    partner = jnp.where(lane < half, upper, lower)
    return xc * cos + partner * sin


def _split_bf16(a):
    hi = a.astype(BF16)
    return hi, (a - hi.astype(F32)).astype(BF16)


def _mm(a, w):
    if w.dtype == BF16:
        return jnp.dot(a.astype(BF16), w, preferred_element_type=F32)
    ah, al = _split_bf16(a.astype(F32))
    wh, wl = _split_bf16(w)
    return (jnp.dot(ah, wh, preferred_element_type=F32) + jnp.dot(al, wh, preferred_element_type=F32)
            + jnp.dot(ah, wl, preferred_element_type=F32))


ROW_PAD = 1


def _row_pitch(d):
    return d // LANES + ROW_PAD


def _store_row_major(ref, val):
    n, d = val.shape
    nc, pitch = d // LANES, _row_pitch(d)
    for c in range(nc):
        ref[pl.ds(c, n, stride=pitch), :] = val[:, c * LANES:(c + 1) * LANES]
    for c in range(nc, pitch):
        ref[pl.ds(c, n, stride=pitch), :] = jnp.zeros((n, LANES), F32)


def _load_row_major(ref, n):
    pitch = ref.shape[0] // n
    return jnp.concatenate([ref[pl.ds(c, n, stride=pitch), :] for c in range(pitch - ROW_PAD)], axis=1)


ROLE_TILES = 3


def _chunk(c):
    return slice(c * LANES, (c + 1) * LANES)


def _in_proj_rows_kernel(x_ref, w_ref, cos_ref, sin_ref, qkv_ref, u_ref):
    j = pl.program_id(0)
    n_qkv_tiles = N_PATTERNS * ROLE_TILES
    acc = _mm(x_ref[...], w_ref[...])
    role = j % ROLE_TILES
    is_qkv = j < n_qkv_tiles

    @pl.when(is_qkv & (role < 2))
    def _():
        cos = cos_ref[...]
        sin = sin_ref[...]
        for c in range(acc.shape[1] // LANES):
            qkv_ref[:, _chunk(c)] = _rope_chunk(acc[:, _chunk(c)], cos, sin)

    @pl.when(is_qkv & (role == 2))
    def _():
        qkv_ref[...] = acc

    @pl.when(j >= n_qkv_tiles)
    def _():
        u_ref[...] = acc


def _in_proj_rows(x, w, cos_t, sin_t):
    m, d = x.shape
    tn = ATTN_WIDTH
    assert w.shape[1] == QKV_WIDTH + POOL_WIDTH and POOL_WIDTH == tn
    n_qkv_tiles = N_PATTERNS * ROLE_TILES
    return pl.pallas_call(
        _in_proj_rows_kernel,
        grid=(n_qkv_tiles + 1,),
        in_specs=[
            pl.BlockSpec((m, d), lambda j: (0, 0)),
            pl.BlockSpec((d, tn), lambda j: (0, j)),
            pl.BlockSpec((m, LANES), lambda j: (0, 0)),
            pl.BlockSpec((m, LANES), lambda j: (0, 0)),
        ],
        out_specs=[pl.BlockSpec((m, tn), lambda j: (0, jnp.minimum(j, n_qkv_tiles - 1))),
                   pl.BlockSpec((m, tn), lambda j: (0, 0))],
        out_shape=[jax.ShapeDtypeStruct((m, QKV_WIDTH), F32), jax.ShapeDtypeStruct((m, POOL_WIDTH), F32)],
        compiler_params=_cparams(("arbitrary",)),
        name="in_proj_rows",
    )(x, w, cos_t, sin_t)


def _proj_pattern_kernel(x_ref, w_ref, cos_ref, sin_ref, out_ref, wb_ref, acc_ref, stage_ref, *, dil, n_rows):
    t = pl.program_id(0)
    n_chunks, tm, _ = stage_ref.shape

    @pl.when(t == 0)
    def _():
        acc_ref[...] = jnp.zeros(acc_ref.shape, F32)

    @pl.when(t % n_rows == 0)
    def _():
        wb_ref[...] = w_ref[...].astype(BF16)

    prev = acc_ref.at[(t + 1) % 2]
    cos = cos_ref[...]
    sin = sin_ref[...]
    for c in range(n_chunks):
        stage_ref[c] = _rope_chunk(prev[:, _chunk(c)], cos, sin)
    for r in range(dil):
        for c in range(n_chunks):
            rows = stage_ref[c, pl.ds(r, tm // dil, stride=dil), :] if dil > 1 else stage_ref[c]
            out_ref[r, :, _chunk(c)] = rows.astype(out_ref.dtype)
    acc_ref[t % 2] = jnp.dot(x_ref[...], wb_ref[...], preferred_element_type=F32)


def _proj_pattern(xb, w, tables, pi, dil, *, tm):
    m, d = xb.shape
    tn = ATTN_WIDTH
    assert m % tm == 0 and tm % (dil * 2 * SUBLANES) == 0
    n_rows = m // tm
    n = n_rows * ROLE_TILES

    def cur(t):
        return jnp.minimum(t, n - 1)

    def prv(t):
        return jnp.maximum(t - 1, 0)

    def table_map(t):
        identity = (prv(t) // n_rows == ROLE_TILES - 1).astype(jnp.int32)
        return (identity, prv(t) % n_rows, 0)

    return pl.pallas_call(
        functools.partial(_proj_pattern_kernel, dil=dil, n_rows=n_rows),
        grid=(n + 1,),
        in_specs=[
            pl.BlockSpec((tm, d), lambda t: (cur(t) % n_rows, 0)),
            pl.BlockSpec((d, tn), lambda t: (0, pi * ROLE_TILES + cur(t) // n_rows)),
            pl.BlockSpec((None, tm, LANES), table_map),
            pl.BlockSpec((None, tm, LANES), table_map),
        ],
        out_specs=pl.BlockSpec((dil, tm // dil, tn), lambda t: (0, prv(t) % n_rows, prv(t) // n_rows)),
        out_shape=jax.ShapeDtypeStruct((dil, m // dil, ROLE_TILES * tn), BF16),
        scratch_shapes=[pltpu.VMEM((d, tn), BF16), pltpu.VMEM((2, tm, tn), F32),
                        pltpu.VMEM((tn // LANES, tm, LANES), F32)],
        compiler_params=_cparams(("arbitrary",)),
        name=f"proj_p{pi}",
    )(xb, w, *tables)


def _proj_u_pool_kernel(x_ref, w_ref, w_pool_ref, scale_ref, pool_ref, xb_ref, utail_ref, wb_ref, buf_ref):
    t = pl.program_id(0)
    tm = x_ref.shape[0]

    @pl.when(t == 0)
    def _():
        wb_ref[...] = w_ref[...].astype(BF16)
        buf_ref[...] = jnp.zeros(buf_ref.shape, F32)

    prev = buf_ref.at[(t + 1) % 2]

    def store(g, val):
        pool_ref[:, g * POOL_GROUP:(g + 1) * POOL_GROUP] = val.astype(pool_ref.dtype)

    _pool_groups(prev, tm, jnp.maximum(t - 1, 0) * tm, w_pool_ref, scale_ref, store)

    xb = x_ref[...].astype(BF16)
    xb_ref[...] = xb
    u = jnp.dot(xb, wb_ref[...], preferred_element_type=F32)
    buf_ref[t % 2, HALO:HALO + tm, :] = u
    tail = u[tm - HALO:, :]
    utail_ref[...] = tail
    prev[0:HALO, :] = tail


def _proj_u_pool(x, w, w_pool_bf16, pool_scale, *, tm):
    m, d = x.shape
    assert m % tm == 0 and tm % HALO == 0
    n = m // tm
    n_qkv_tiles = N_PATTERNS * ROLE_TILES
    cur = lambda t: (jnp.minimum(t, n - 1), 0)
    prv = lambda t: (jnp.maximum(t - 1, 0), 0)
    return pl.pallas_call(
        _proj_u_pool_kernel,
        grid=(n + 1,),
        in_specs=[
            pl.BlockSpec((tm, d), cur),
            pl.BlockSpec((d, POOL_WIDTH), lambda t: (0, n_qkv_tiles), pipeline_mode=pl.Buffered(1)),
            pl.BlockSpec((len(POOL_WINDOWS), POOL_GROUP, POOL_GROUP), lambda t: (0, 0, 0)),
            pl.BlockSpec((1, POOL_WIDTH), lambda t: (0, 0)),
        ],
        out_specs=[
            pl.BlockSpec((tm, POOL_WIDTH), prv),
            pl.BlockSpec((tm, d), cur),
            pl.BlockSpec((HALO, POOL_WIDTH), lambda t: (0, 0)),
        ],
        out_shape=[
            jax.ShapeDtypeStruct((m, POOL_WIDTH), BF16),
            jax.ShapeDtypeStruct((m, d), BF16),
            jax.ShapeDtypeStruct((HALO, POOL_WIDTH), F32),
        ],
        scratch_shapes=[pltpu.VMEM((d, POOL_WIDTH), BF16), pltpu.VMEM((2, HALO + tm, POOL_WIDTH), F32)],
        compiler_params=_cparams(("arbitrary",)),
        name="proj_u_pool",
    )(x, w, w_pool_bf16, pool_scale)


def _attn_kernel(q_ref, kp_ref, kc_ref, vp_ref, vc_ref, o_ref, lse_ref,
                 lim_ref, s_ref, e_ref, m_ref, den_ref, *, r_max, dil):
    n = pl.program_id(0)
    r = pl.program_id(1)
    qi = lax.broadcasted_iota(jnp.int32, (BAND, 2 * BAND), 0)
    ki = lax.broadcasted_iota(jnp.int32, (BAND, 2 * BAND), 1)
    dist = BAND + qi - ki
    valid = (dist >= 0) & (dist <= r_max) & ((n > 0) | (ki >= BAND))
    lim_ref[...] = jnp.where(valid, jnp.float32(jnp.finfo(F32).max), jnp.float32(NEG_INF))

    lane = lax.broadcasted_iota(jnp.int32, (BAND, LANES), 1)
    heads_per_chunk = LANES // HEAD_DIM
    n_chunks = ATTN_WIDTH // LANES
    lane_head = lane >> (HEAD_DIM.bit_length() - 1)
    qscale = [jnp.where(lane_head == hh, HEAD_DIM ** -0.5, 0.0).astype(BF16) for hh in range(heads_per_chunk)]
    nt = (((1,), (1,)), ((), ()))

    for c in range(n_chunks):
        sl = slice(c * LANES, (c + 1) * LANES)
        q2 = q_ref[:, sl]
        kp, kc = kp_ref[:, sl], kc_ref[:, sl]
        for hh in range(heads_per_chunk):
            h = c * heads_per_chunk + hh
            qm = q2 * qscale[hh]
            s = jnp.concatenate(
                [lax.dot_general(qm, kp, nt, preferred_element_type=F32),
                 lax.dot_general(qm, kc, nt, preferred_element_type=F32)], axis=1)
            s = jnp.minimum(s, lim_ref[...])
            s_ref[h] = s
            m_ref[h] = jnp.broadcast_to(jnp.max(s, axis=-1, keepdims=True), (BAND, LANES))

    lse_acc = jnp.zeros((BAND, LANES), F32)
    for h in range(N_HEADS):
        m = m_ref[h]
        e = jnp.exp(s_ref[h] - jnp.concatenate([m, m], axis=1))
        e_ref[h] = e.astype(BF16)
        den = jnp.broadcast_to(jnp.sum(e, axis=-1, keepdims=True), (BAND, LANES))
        den_ref[h] = den
        lse_acc = jnp.where(lane == h, m + jnp.log(den), lse_acc)

    rows = slice(None) if dil == 1 else pl.ds(r, BAND, stride=dil)
    lse_ref[rows, :] = lse_acc
    for c in range(n_chunks):
        sl = slice(c * LANES, (c + 1) * LANES)
        vp, vc = vp_ref[:, sl], vc_ref[:, sl]
        o2 = jnp.zeros((BAND, LANES), F32)
        for hh in range(heads_per_chunk):
            h = c * heads_per_chunk + hh
            pv = (jnp.dot(e_ref[h, :, :BAND], vp, preferred_element_type=F32)
                  + jnp.dot(e_ref[h, :, BAND:], vc, preferred_element_type=F32))
            o2 = jnp.where(lane_head == hh, pv / den_ref[h], o2)
        o_ref[c, rows, :] = o2


def _attn_pattern(qkv, pi, dil, window):
    s = qkv.shape[0] * qkv.shape[1]
    assert qkv.shape[0] == dil and s % (dil * BAND) == 0
    nb = s // (dil * BAND)
    r_max = window // dil

    def cur(role):
        return lambda n, r: (r, n, role)

    def prev(role):
        return lambda n, r: (r, jnp.maximum(n - 1, 0), role)

    blk = (None, BAND, ATTN_WIDTH)
    span = BAND * dil
    return pl.pallas_call(
        functools.partial(_attn_kernel, r_max=r_max, dil=dil),
        grid=(nb, dil),
        in_specs=[
            pl.BlockSpec(blk, cur(0)),
            pl.BlockSpec(blk, prev(1)),
            pl.BlockSpec(blk, cur(1)),
            pl.BlockSpec(blk, prev(2)),
            pl.BlockSpec(blk, cur(2)),
        ],
        out_specs=[
            pl.BlockSpec((ATTN_WIDTH // LANES, span, LANES), lambda n, r: (0, n, 0)),
            pl.BlockSpec((span, LANES), lambda n, r: (n, 0)),
        ],
        out_shape=[
            jax.ShapeDtypeStruct((ATTN_WIDTH // LANES, s, LANES), F32),
            jax.ShapeDtypeStruct((s, LANES), F32),
        ],
        scratch_shapes=[
            pltpu.VMEM((BAND, 2 * BAND), F32),
            pltpu.VMEM((N_HEADS, BAND, 2 * BAND), F32),
            pltpu.VMEM((N_HEADS, BAND, 2 * BAND), BF16),
            pltpu.VMEM((N_HEADS, BAND, LANES), F32),
            pltpu.VMEM((N_HEADS, BAND, LANES), F32),
        ],
        compiler_params=_cparams(("arbitrary", "arbitrary")),
        name=f"attn_p{pi}",
    )(qkv, qkv, qkv, qkv, qkv)


def _pool_groups(buf_ref, tm, pos0, w_pool_ref, scale_ref, store):
    row = lax.broadcasted_iota(jnp.int32, (tm, 1), 0) + pos0
    for g, win in enumerate(POOL_WINDOWS):
        sl = slice(g * POOL_GROUP, (g + 1) * POOL_GROUP)
        cur = buf_ref[HALO:HALO + tm, sl]
        acc = cur
        for jj in range(1, win):
            acc = acc + buf_ref[HALO - jj:HALO - jj + tm, sl]
        cnt = jnp.minimum(row + 1, win).astype(F32)
        d = acc / cnt - cur
        y = _mm(d, w_pool_ref[g])
        store(g, y * scale_ref[:, sl])


def _sample_attn_kernel(zt_ref, c_ref, cn_ref, ot_ref, lt_ref, s_ref, *, dil, hps, length):
    b = pl.program_id(0)
    g = pl.program_id(1)

    @pl.when((b == 0) & (g == 0))
    def _():
        ot_ref[...] = jnp.zeros(ot_ref.shape, F32)
        lt_ref[...] = jnp.zeros(lt_ref.shape, F32)

    rows = hps * HEAD_DIM
    base = pl.multiple_of(g * rows, rows)
    lane_z = lax.broadcasted_iota(jnp.int32, (rows, zt_ref.shape[1]), 1)

    def column(role):
        blk = zt_ref[pl.ds(role * ATTN_WIDTH + base, rows), :]
        return jnp.sum(jnp.where(lane_z == b, blk, 0.0), axis=-1, keepdims=True)

    qc, knc, vnc = column(0), column(1), column(2)
    scale = HEAD_DIM ** -0.5
    lane = lax.broadcasted_iota(jnp.int32, (1, length), 1)
    key_ok = (lane & (dil - 1)) == 0
    last = lane == length - 1
    lane_o = lax.broadcasted_iota(jnp.int32, (HEAD_DIM, LANES), 1)
    lane_l = lax.broadcasted_iota(jnp.int32, (hps, LANES), 1)

    def head_rows(h):
        return slice(h * HEAD_DIM, (h + 1) * HEAD_DIM)

    for h in range(hps):
        k = c_ref[0, h]
        s_ref[h:h + 1, :] = jnp.sum(k * qc[head_rows(h)], axis=0, keepdims=True)
        cn_ref[0, h] = jnp.where(last, knc[head_rows(h)], pltpu.roll(k, length - 1, 1))

    s = jnp.where(key_ok, s_ref[...] * scale, NEG_INF)
    s_n = jnp.sum((knc * qc).reshape(hps, HEAD_DIM, 1), axis=1) * scale
    m = jnp.maximum(jnp.max(s, axis=-1, keepdims=True), s_n)
    e = jnp.exp(s - m)
    e_n = jnp.exp(s_n - m)
    den = jnp.sum(e, axis=-1, keepdims=True) + e_n
    s_ref[...] = e
    lse = m + jnp.log(den)
    head0 = pl.multiple_of(g * hps, hps)
    lt_ref[pl.ds(head0, hps), :] = jnp.where(lane_l == b, lse, lt_ref[pl.ds(head0, hps), :])

    for h in range(hps):
        v = c_ref[1, h]
        vn = vnc[head_rows(h)]
        o = (jnp.sum(v * s_ref[h:h + 1, :], axis=-1, keepdims=True) + vn * e_n[h:h + 1, :]) / den[h:h + 1, :]
        cn_ref[1, h] = jnp.where(last, vn, pltpu.roll(v, length - 1, 1))
        r0 = pl.multiple_of(base + h * HEAD_DIM, HEAD_DIM)
        ot_ref[pl.ds(r0, HEAD_DIM), :] = jnp.where(lane_o == b, o, ot_ref[pl.ds(r0, HEAD_DIM), :])


def _sample_attn(zt, c_t, pi, dil):
    db, _, _, _, length = c_t.shape
    assert db <= LANES and length == dil * BAND
    hps = min(N_HEADS, max(1, (4 * 2048) // length))
    blk = (None, 2, hps, HEAD_DIM, length)
    return pl.pallas_call(
        functools.partial(_sample_attn_kernel, dil=dil, hps=hps, length=length),
        grid=(db, N_HEADS // hps),
        in_specs=[
            pl.BlockSpec((3 * ATTN_WIDTH, db), lambda b, g: (pi, 0)),
            pl.BlockSpec(blk, lambda b, g: (b, 0, g, 0, 0)),
        ],
        out_specs=[
            pl.BlockSpec(blk, lambda b, g: (b, 0, g, 0, 0)),
            pl.BlockSpec((ATTN_WIDTH, LANES), lambda b, g: (0, 0)),
            pl.BlockSpec((LANES, LANES), lambda b, g: (0, 0)),
        ],
        out_shape=[
            jax.ShapeDtypeStruct(c_t.shape, c_t.dtype),
            jax.ShapeDtypeStruct((ATTN_WIDTH, LANES), F32),
            jax.ShapeDtypeStruct((LANES, LANES), F32),
        ],
        scratch_shapes=[pltpu.VMEM((hps, length), F32)],
        compiler_params=_cparams(("arbitrary", "arbitrary")),
        name=f"sample_attn_p{pi}",
    )(zt, c_t)


def _sample_pool_kernel(u_ref, sp_ref, w_pool_ref, scale_ref, pool_ref, buf_ref):
    buf_ref[...] = jnp.zeros(buf_ref.shape, F32)
    buf_ref[HALO - POOL_STATE:HALO, :] = sp_ref[...]
    buf_ref[HALO:HALO + 1, :] = u_ref[...]

    def store(g, val):
        pool_ref[:, g * POOL_GROUP:(g + 1) * POOL_GROUP] = val[0:1]

    _pool_groups(buf_ref, HALO, PAST_LEN, w_pool_ref, scale_ref, store)


def _head_expand():
    dim_head = jnp.arange(ATTN_WIDTH, dtype=jnp.int32) // HEAD_DIM
    return (jnp.arange(LANES, dtype=jnp.int32)[:, None] == dim_head[None, :]).astype(F32)


def _sample_pool(us, state_pool, w_pool_bf16, pool_scale):
    db = us.shape[0]
    pool = pl.pallas_call(
        _sample_pool_kernel,
        grid=(db,),
        in_specs=[
            pl.BlockSpec((None, 1, POOL_WIDTH), lambda b: (b, 0, 0)),
            pl.BlockSpec((None, POOL_STATE, POOL_WIDTH), lambda b: (b, 0, 0)),
            pl.BlockSpec((len(POOL_WINDOWS), POOL_GROUP, POOL_GROUP), lambda b: (0, 0, 0)),
            pl.BlockSpec((1, POOL_WIDTH), lambda b: (0, 0)),
        ],
        out_specs=pl.BlockSpec((None, 1, POOL_WIDTH), lambda b: (b, 0, 0)),
        out_shape=jax.ShapeDtypeStruct((db, 1, POOL_WIDTH), F32),
        scratch_shapes=[pltpu.VMEM((2 * HALO, POOL_WIDTH), F32)],
        compiler_params=_cparams(("arbitrary",)),
        name="sample_pool",
    )(us.reshape(db, 1, POOL_WIDTH), state_pool, w_pool_bf16, pool_scale)
    return pool.reshape(db, POOL_WIDTH)


def _layer_norm(xf, g, b):
    mu = jnp.mean(xf, axis=-1, keepdims=True)
    xc = xf - mu
    var = jnp.mean(xc * xc, axis=-1, keepdims=True)
    return xc * lax.rsqrt(var + LN_EPS) * g + b


def _route(logits):
    lane = lax.broadcasted_iota(jnp.int32, logits.shape, 1)
    lane_f = lane.astype(F32)
    neg = jnp.float32(-jnp.inf)

    def first_argmax(v, vmax):
        return jnp.min(jnp.where(v == vmax, lane_f, jnp.float32(LANES)), axis=-1, keepdims=True)

    gl = jnp.where(lane < N_GROUPS, logits, neg)
    gmax = jnp.max(gl, axis=-1, keepdims=True)
    g_sel = first_argmax(gl, gmax)
    g_gate = 1.0 / jnp.sum(jnp.exp(gl - gmax), axis=-1, keepdims=True)
    lane_group = (((lane + (EXPERTS_PER_GROUP - N_GROUPS)) >> 3) - 1).astype(F32)
    el = jnp.where(lane_group == g_sel, logits, neg)
    v1 = jnp.max(el, axis=-1, keepdims=True)
    i1 = first_argmax(el, v1)
    el2 = jnp.where(lane_f == i1, neg, el)
    v2 = jnp.max(el2, axis=-1, keepdims=True)
    i2 = first_argmax(el2, v2)
    t = jnp.exp(v2 - v1)
    w1 = (1.0 / (1.0 + t)) * g_gate
    w2 = (t / (1.0 + t)) * g_gate
    route = jnp.where(lane == 0, i1 - N_GROUPS, 0.0)
    route = jnp.where(lane == 1, i2 - N_GROUPS, route)
    route = jnp.where(lane == 2, w1, route)
    route = jnp.where(lane == 3, w2, route)
    return route


def _post_kernel(o0_ref, o1_ref, o2_ref, l0_ref, l1_ref, l2_ref, pool_ref, x_ref, hexp_ref,
                 wo_ref, g1_ref, b1_ref, wr_ref, br_ref, *rest, n_row_tiles):
    x1_ref, route_ref, wrh_ref, wrl_ref = rest[-4:]
    i = pl.program_id(0)

    @pl.when(i == 0)
    def _():
        wrh_ref[...], wrl_ref[...] = _split_bf16(wr_ref[...])

    @pl.when(i < n_row_tiles)
    def _():
        l0, l1, l2 = l0_ref[...], l1_ref[...], l2_ref[...]
        lmax = jnp.maximum(jnp.maximum(l0, l1), l2)
        e0, e1, e2 = jnp.exp(l0 - lmax), jnp.exp(l1 - lmax), jnp.exp(l2 - lmax)
        den = e0 + e1 + e2
        hexp = hexp_ref[...]
        attn = jnp.zeros((x_ref.shape[0], ATTN_WIDTH), F32)
        for o_ref, e in ((o0_ref, e0), (o1_ref, e1), (o2_ref, e2)):
            wh, wl = _split_bf16(e / den)
            w_x = (jnp.dot(wh, hexp, preferred_element_type=F32) + jnp.dot(wl, hexp, preferred_element_type=F32))
            o = jnp.concatenate([o_ref[c] for c in range(ATTN_WIDTH // LANES)], axis=1)
            attn = attn + o * w_x
        h = _mm(attn, wo_ref[0:ATTN_WIDTH, :]) + _mm(pool_ref[...], wo_ref[ATTN_WIDTH:, :])
        x1 = _layer_norm(ALPHA * x_ref[...] + h, g1_ref[...], b1_ref[...])
        _store_row_major(x1_ref, x1)
        xh, xl = _split_bf16(x1)
        wrh = wrh_ref[...]
        logits = (jnp.dot(xh, wrh, preferred_element_type=F32) + jnp.dot(xl, wrh, preferred_element_type=F32)
                  + jnp.dot(xh, wrl_ref[...], preferred_element_type=F32)) + br_ref[...]
        route_ref[...] = _route(logits)

    @pl.when(i >= n_row_tiles)
    def _():
        x1_ref[...] = jnp.zeros(x1_ref.shape, F32)


def _const_spec(shape):
    return pl.BlockSpec(shape, lambda i: (0,) * len(shape), pipeline_mode=pl.Buffered(1))


def _post(os_, lses, pool, x, wo, g1, b1, wr, br, *, tm, name, filler_tiles=0, x1_into=None, row0=0):
    m, d = x.shape
    n = m // tm
    pitch = _row_pitch(d)
    assert (tm * pitch) % SUBLANES == 0
    hexp = _head_expand().astype(BF16)
    last = n - 1

    def rows(width):
        return pl.BlockSpec((tm, width), lambda i: (jnp.minimum(i, last), 0))

    in_specs = ([pl.BlockSpec((ATTN_WIDTH // LANES, tm, LANES), lambda i: (0, jnp.minimum(i, last), 0))] * 3
                + [rows(LANES)] * 3
                + [rows(POOL_WIDTH), rows(d), _const_spec((LANES, ATTN_WIDTH)),
                   _const_spec((d, d)), _const_spec((1, d)), _const_spec((1, d)),
                   _const_spec((d, LANES)), _const_spec((1, LANES))])
    operands = [*os_, *lses, pool, x, hexp, wo, g1, b1, wr, br]
    aliases = {}
    if x1_into is None:
        x1_shape = jax.ShapeDtypeStruct(((m + filler_tiles * tm) * pitch, LANES), F32)
    else:
        assert filler_tiles == 0 and row0 % tm == 0
        in_specs.append(pl.BlockSpec(memory_space=pl.ANY))
        operands.append(x1_into)
        aliases = {len(operands) - 1: 0}
        x1_shape = jax.ShapeDtypeStruct(x1_into.shape, F32)
    return pl.pallas_call(
        functools.partial(_post_kernel, n_row_tiles=n),
        grid=(n + filler_tiles,),
        in_specs=in_specs,
        out_specs=[pl.BlockSpec((tm * pitch, LANES), lambda i: (row0 // tm + i, 0)), rows(LANES)],
        out_shape=[x1_shape, jax.ShapeDtypeStruct((m, LANES), F32)],
        input_output_aliases=aliases,
        scratch_shapes=[pltpu.VMEM((d, LANES), BF16), pltpu.VMEM((d, LANES), BF16)],
        compiler_params=_cparams(("arbitrary",)),
        name=name,
    )(*operands)


def _routing_plan(pair_expert, n_tiles):
    p = pair_expert.shape[0]
    experts = jnp.arange(N_EXPERTS, dtype=jnp.int32)
    onehot = (pair_expert[:, None] == experts[None, :]).astype(jnp.int32)
    csum = jnp.cumsum(onehot, axis=0)
    rank = jnp.take_along_axis(csum, pair_expert[:, None], axis=1)[:, 0] - 1
    counts = csum[-1]
    tiles_per = (counts + MOE_TILE - 1) // MOE_TILE
    tile_end = jnp.cumsum(tiles_per)
    tile_start = tile_end - tiles_per
    dest = (tile_start[pair_expert] * MOE_TILE + rank).astype(jnp.int32)
    n_used = tile_end[-1]
    tile_ids = jnp.arange(n_tiles, dtype=jnp.int32)
    tile_expert = jnp.sum((tile_ids[:, None] >= tile_end[None, :]).astype(jnp.int32), axis=1)
    last_expert = jnp.sum((n_used - 1 >= tile_end).astype(jnp.int32))
    tile_expert = jnp.where(tile_ids < n_used, tile_expert, last_expert).astype(jnp.int32)
    slot_token = jnp.zeros((n_tiles * MOE_TILE,), jnp.int32).at[dest].set(
        jnp.arange(p, dtype=jnp.int32) // 2, unique_indices=True, mode="promise_in_bounds")
    return dest, slot_token, tile_expert, n_used.reshape(1).astype(jnp.int32)


def _moe_kernel(te_ref, nu_ref, st_ref, x_ref, wg_ref, wu_ref, wd_ref, ys_ref,
                buf_ref, wgb_ref, wub_ref, wdb_ref, sem_ref, *, nc, pitch):
    c = pl.program_id(0)
    n_tiles = pl.num_programs(0)
    n_used = nu_ref[0]

    def issue(tile, slot):
        def body(g, carry):
            for k in range(GATHER_UNROLL):
                i = g * GATHER_UNROLL + k
                src = st_ref[tile * MOE_TILE + i]
                pltpu.make_async_copy(x_ref.at[pl.ds(src, nc), :], buf_ref.at[slot, pl.ds(i * pitch, nc), :],
                                      sem_ref.at[slot]).start(priority=k % 2)
            return carry

        lax.fori_loop(0, MOE_TILE // GATHER_UNROLL, body, 0)

    def wait(slot):
        pltpu.make_async_copy(x_ref.at[pl.ds(0, MOE_TILE * nc), :], buf_ref.at[slot, pl.ds(0, MOE_TILE * nc), :],
                              sem_ref.at[slot]).wait()

    slot = c % 2

    @pl.when((c == 0) & (n_used > 0))
    def _():
        issue(0, 0)

    @pl.when((c + 1 < n_tiles) & (c + 1 < n_used))
    def _():
        issue(c + 1, 1 - slot)

    prev_expert = te_ref[jnp.maximum(c - 1, 0)]

    @pl.when((c == 0) | (te_ref[c] != prev_expert))
    def _():
        wgb_ref[...] = wg_ref[...].astype(BF16)
        wub_ref[...] = wu_ref[...].astype(BF16)
        wdb_ref[...] = wd_ref[...].astype(BF16)

    @pl.when(c < n_used)
    def _():
        wait(slot)
        x = _load_row_major(buf_ref.at[slot], MOE_TILE).astype(BF16)
        gate = jnp.dot(x, wgb_ref[...], preferred_element_type=F32)
        up = jnp.dot(x, wub_ref[...], preferred_element_type=F32)
        h = jax.nn.silu(gate) * up
        _store_row_major(ys_ref, jnp.dot(h.astype(BF16), wdb_ref[...], preferred_element_type=F32))

    @pl.when(c >= n_used)
    def _():
        ys_ref[...] = jnp.zeros(ys_ref.shape, ys_ref.dtype)


def _moe(x1, slot_row, tile_expert, n_used, w_gate, w_up, w_down, n_tiles):
    d = w_gate.shape[1]
    nc, pitch = d // LANES, _row_pitch(d)
    f = w_gate.shape[-1]
    grid_spec = pltpu.PrefetchScalarGridSpec(
        num_scalar_prefetch=3,
        grid=(n_tiles,),
        in_specs=[
            pl.BlockSpec(memory_space=pl.ANY),
            pl.BlockSpec((None, d, f), lambda c, te, nu, st: (te[c], 0, 0)),
            pl.BlockSpec((None, d, f), lambda c, te, nu, st: (te[c], 0, 0)),
            pl.BlockSpec((None, f, d), lambda c, te, nu, st: (te[c], 0, 0)),
        ],
        out_specs=pl.BlockSpec((MOE_TILE * pitch, LANES), lambda c, te, nu, st: (c, 0)),
        scratch_shapes=[
            pltpu.VMEM((2, MOE_TILE * pitch, LANES), F32),
            pltpu.VMEM((d, f), BF16),
            pltpu.VMEM((d, f), BF16),
            pltpu.VMEM((f, d), BF16),
            pltpu.SemaphoreType.DMA((2,)),
        ],
    )
    return pl.pallas_call(
        functools.partial(_moe_kernel, nc=nc, pitch=pitch),
        grid_spec=grid_spec,
        out_shape=jax.ShapeDtypeStruct((n_tiles * MOE_TILE * pitch, LANES), F32),
        compiler_params=_cparams(("arbitrary",)),
        name="moe",
    )(tile_expert, n_used, slot_row, x1, w_gate, w_up, w_down)


def _final_kernel(dest_ref, ys_ref, x1_ref, route_ref, p_ref, g2_ref, b2_ref, wpg_ref, wp_ref, o_ref,
                  buf_ref, sem_ref, *, tm, pair0, nc, pitch):
    i = pl.program_id(0)
    n_steps = pl.num_programs(0)

    def issue(step, slot):
        def body(r, carry):
            for k in range(2):
                src = dest_ref[pair0 + (step * tm + r) * 2 + k]
                pltpu.make_async_copy(ys_ref.at[pl.ds(src, nc), :], buf_ref.at[slot, k, pl.ds(r * pitch, nc), :],
                                      sem_ref.at[slot]).start()
            return carry

        lax.fori_loop(0, tm, body, 0, unroll=GATHER_UNROLL)

    def wait(slot):
        for k in range(2):
            pltpu.make_async_copy(ys_ref.at[pl.ds(0, tm * nc), :], buf_ref.at[slot, k, pl.ds(0, tm * nc), :],
                                  sem_ref.at[slot]).wait()

    slot = i % 2

    @pl.when(i == 0)
    def _():
        issue(0, 0)

    @pl.when(i + 1 < n_steps)
    def _():
        issue(i + 1, 1 - slot)

    wait(slot)
    route = route_ref[...]
    y = (route[:, 2:3] * _load_row_major(buf_ref.at[slot, 0], tm)
         + route[:, 3:4] * _load_row_major(buf_ref.at[slot, 1], tm))
    x2 = _layer_norm(ALPHA * _load_row_major(x1_ref, tm) + y, g2_ref[...], b2_ref[...])
    gate = jax.nn.sigmoid(jnp.dot(x2.astype(BF16), wpg_ref[...], preferred_element_type=F32))
    ple = jnp.dot(p_ref[...].astype(BF16), wp_ref[...], preferred_element_type=F32)
    o_ref[...] = x2 + gate * ple


def _final(dest, ys, x1, route, p, g2, b2, wpg, wp, *, tm, row0):
    m, pd = p.shape
    d = wpg.shape[0]
    nc, pitch = d // LANES, _row_pitch(d)
    assert m % tm == 0 and row0 % tm == 0
    pair0 = 2 * row0
    grid_spec = pltpu.PrefetchScalarGridSpec(
        num_scalar_prefetch=1,
        grid=(m // tm,),
        in_specs=[
            pl.BlockSpec(memory_space=pl.ANY),
            pl.BlockSpec((tm * pitch, LANES), lambda i, dst: (row0 // tm + i, 0)),
            pl.BlockSpec((tm, LANES), lambda i, dst: (i, 0)),
            pl.BlockSpec((tm, pd), lambda i, dst: (i, 0)),
            pl.BlockSpec((1, d), lambda i, dst: (0, 0)),
            pl.BlockSpec((1, d), lambda i, dst: (0, 0)),
            pl.BlockSpec((d, d), lambda i, dst: (0, 0)),
            pl.BlockSpec((pd, d), lambda i, dst: (0, 0)),
        ],
        out_specs=pl.BlockSpec((tm, d), lambda i, dst: (i, 0)),
        scratch_shapes=[pltpu.VMEM((2, 2, tm * pitch, LANES), F32), pltpu.SemaphoreType.DMA((2,))],
    )
    return pl.pallas_call(
        functools.partial(_final_kernel, tm=tm, pair0=pair0, nc=nc, pitch=pitch),
        grid_spec=grid_spec,
        out_shape=jax.ShapeDtypeStruct((m, d), F32),
        compiler_params=_cparams(("arbitrary",)),
        name="final",
    )(dest, ys, x1, route, p, g2, b2, wpg, wp)


def _pick_tile(m, pref):
    t = pref
    while m % t:
        t //= 2
    return t


def _layer(xp, xs, caches, state_pool, pp, ps, w):
    s, d = xp.shape
    db = xs.shape[0]

    cos_p, sin_p = _rope_tables(jnp.arange(s, dtype=jnp.int32))
    dils = tuple(dil for _, dil in DILATED_PATTERNS)
    tables = (jnp.stack([cos_p, jnp.ones_like(cos_p)]), jnp.stack([sin_p, jnp.zeros_like(sin_p)]))
    tm_in = _pick_tile(s, TM_POOL_PROJ)
    pool_p, xp_b, u_tail = _proj_u_pool(xp, w["w_in_f32"], w["w_pool"], w["pool_scale"], tm=tm_in)
    tm_pat = _pick_tile(s, TM_PATTERN_PROJ)
    qkv_p = [_proj_pattern(xp_b, w["w_in_f32"], tables, pi, dil, tm=tm_pat) for pi, dil in enumerate(dils)]
    os_, lses = [], []
    for pi, (window, dil) in enumerate(DILATED_PATTERNS):
        o, l = _attn_pattern(qkv_p[pi], pi, dil, window)
        os_.append(o)
        lses.append(l)

    cos_s, sin_s = _rope_tables(jnp.full((db,), PAST_LEN, jnp.int32))
    qkv_s, u_s = _in_proj_rows(xs, w["w_in_f32"], cos_s, sin_s)
    zt_s = qkv_s.T
    os_s, lses_s, kv_s = [], [], []
    for pi, ((window, dil), c) in enumerate(zip(DILATED_PATTERNS, caches)):
        cn, ot, lt = _sample_attn(zt_s, jnp.transpose(c, (0, 2, 3, 4, 1)), pi, dil)
        kv_s.append(jnp.transpose(cn, (0, 4, 1, 2, 3)))
        os_s.append(jnp.transpose(ot.T[:db].reshape(db, ATTN_WIDTH // LANES, LANES), (1, 0, 2)))
        lses_s.append(lt.T[:db])
    pool_s = _sample_pool(u_s, state_pool, w["w_pool_f32"], w["pool_scale"])

    post_w = (w["ln1_g"], w["ln1_b"], w["w_router"], w["b_router"])
    tm_p = _pick_tile(s, TM_ROWWISE)
    assert s % db == 0 and tm_p >= db
    x1, route_p = _post(os_, lses, pool_p, xp, w["w_out"], *post_w, tm=tm_p, name="post_prompt",
                        filler_tiles=1)
    x1, route_s = _post(os_s, lses_s, pool_s, xs, w["w_out_f32"], *post_w, tm=db, name="post_sample",
                        x1_into=x1, row0=s)

    pair_expert = jnp.concatenate([route_p[:, 0:2].reshape(-1), route_s[:, 0:2].reshape(-1)]).astype(jnp.int32)
    n_pairs = pair_expert.shape[0]
    n_tiles = -(-n_pairs // MOE_TILE) + N_EXPERTS
    dest, slot_token, tile_expert, n_used = _routing_plan(pair_expert, n_tiles)
    pitch = _row_pitch(d)
    ys = _moe(x1, slot_token * pitch, tile_expert, n_used, w["w_gate"], w["w_up"], w["w_down"], n_tiles)

    fin_w = (w["ln2_g"], w["ln2_b"], w["w_ple_gate"], w["w_ple"])
    y_p = _final(dest * pitch, ys, x1, route_p, pp, *fin_w, tm=tm_p, row0=0)
    y_s = _final(dest * pitch, ys, x1, route_s, ps, *fin_w, tm=db, row0=s)

    kv_p = []
    for pi, (window, dil) in enumerate(DILATED_PATTERNS):
        keep = min(window, s)
        kv = qkv_p[pi][:, (s - keep) // dil:, ATTN_WIDTH:3 * ATTN_WIDTH].astype(F32)
        kv = jnp.transpose(kv, (1, 0, 2))
        kv_p.append(kv.reshape(1, keep, 2, N_HEADS, HEAD_DIM))
    pool_state_p = u_tail[HALO - POOL_STATE:][None]
    pool_state_s = jnp.concatenate([state_pool[:, 1:], u_s[:, None, :]], axis=1)
    return y_p, y_s, kv_p, pool_state_p, kv_s, pool_state_s


def kernel(x_prompt, x_sample, cache_kv_w128_d1, cache_kv_w512_d4, cache_kv_w2048_d16, state_pool, p_prompt, p_sample, w_in, w_out, w_pool, pool_scale, ln1_g, ln1_b, w_group_router, b_group_router, w_expert_router, b_expert_router, w_gate, w_up, w_down, ln2_g, ln2_b, w_ple, w_ple_gate):
    assert w_in.shape[0] == DEPTH == 1 and x_prompt.shape[0] == 1 and x_sample.shape[1] == 1
    d = x_prompt.shape[-1]
    pad = LANES - N_GROUPS - N_EXPERTS
    w = {
        "w_in_f32": w_in[0],
        "w_out": w_out[0].astype(BF16), "w_out_f32": w_out[0],
        "w_pool": w_pool[0].astype(BF16), "w_pool_f32": w_pool[0],
        "pool_scale": pool_scale[0].reshape(1, POOL_WIDTH),
        "ln1_g": ln1_g[0].reshape(1, d), "ln1_b": ln1_b[0].reshape(1, d),
        "ln2_g": ln2_g[0].reshape(1, d), "ln2_b": ln2_b[0].reshape(1, d),
        "w_router": jnp.concatenate([w_group_router[0], w_expert_router[0], jnp.zeros((d, pad), F32)], axis=1),
        "b_router": jnp.concatenate([b_group_router[0], b_expert_router[0], jnp.zeros((pad,), F32)]).reshape(1, LANES),
        "w_gate": w_gate[0], "w_up": w_up[0], "w_down": w_down[0],
        "w_ple": w_ple[0].astype(BF16),
        "w_ple_gate": w_ple_gate[0].astype(BF16),
    }
    caches = [cache_kv_w128_d1[0], cache_kv_w512_d4[0], cache_kv_w2048_d16[0]]
    y_p, y_s, kv_p, pool_p, kv_s, pool_s = _layer(
        x_prompt[0], x_sample[:, 0], caches, state_pool[0], p_prompt[0, 0], p_sample[0, :, 0], w)
    return (y_p[None], y_s[:, None], kv_p[0][None], kv_p[1][None], kv_p[2][None], pool_p[None],
            kv_s[0][None], kv_s[1][None], kv_s[2][None], pool_s[None])
```

```python
import functools

import jax
import jax.numpy as jnp
from jax import lax
from jax.experimental import pallas as pl
from jax.experimental.pallas import tpu as pltpu

F32 = jnp.float32
BF16 = jnp.bfloat16

PAST_LEN = 8192
HEAD_DIM = 64
N_HEADS = 16
ATTN_WIDTH = N_HEADS * HEAD_DIM
DILATED_PATTERNS = ((128, 1), (512, 4), (2048, 16))
N_PATTERNS = len(DILATED_PATTERNS)
BAND = 128
POOL_WINDOWS = (2, 4, 8, 16)
POOL_GROUP = 256
POOL_WIDTH = POOL_GROUP * len(POOL_WINDOWS)
POOL_STATE = max(POOL_WINDOWS) - 1
ROT_DIM = HEAD_DIM // 4
ROPE_THETA = 500000.0
QKV_WIDTH = N_PATTERNS * 3 * ATTN_WIDTH
N_GROUPS = 4
EXPERTS_PER_GROUP = 8
N_EXPERTS = N_GROUPS * EXPERTS_PER_GROUP
DEPTH = 1
ALPHA = (2.0 * DEPTH) ** 0.25
LN_EPS = 1e-5
NEG_INF = -1e30

LANES = 128
SUBLANES = 8
VMEM_LIMIT = 48 * 1024 * 1024

TM_POOL_PROJ = 512
TM_PATTERN_PROJ = 512
TM_ROWWISE = 256
MOE_TILE = 128
HALO = 16
GATHER_UNROLL = 8


def _cparams(semantics):
    return pltpu.CompilerParams(dimension_semantics=semantics, vmem_limit_bytes=VMEM_LIMIT)


def _rope_tables(pos):
    inv_freq = ROPE_THETA ** (-jnp.arange(0, ROT_DIM, 2, dtype=F32) / ROT_DIM)
    ang = pos.astype(F32)[:, None] * inv_freq[None, :]
    cos, sin = jnp.cos(ang), jnp.sin(ang)
    t = pos.shape[0]
    rest = HEAD_DIM - ROT_DIM
    c64 = jnp.concatenate([cos, cos, jnp.ones((t, rest), F32)], -1)
    s64 = jnp.concatenate([-sin, sin, jnp.zeros((t, rest), F32)], -1)
    return jnp.tile(c64, (1, LANES // HEAD_DIM)), jnp.tile(s64, (1, LANES // HEAD_DIM))


def _rope_chunk(xc, cos, sin):
    half = ROT_DIM // 2
    lane = lax.broadcasted_iota(jnp.int32, xc.shape, 1) & (HEAD_DIM - 1)
    upper = pltpu.roll(xc, LANES - half, 1)
    lower = pltpu.roll(xc, half, 1)
    partner = jnp.where(lane < half, upper, lower)
    return xc * cos + partner * sin


def _split_bf16(a):
    hi = a.astype(BF16)
    return hi, (a - hi.astype(F32)).astype(BF16)


def _mm(a, w):
    if w.dtype == BF16:
        return jnp.dot(a.astype(BF16), w, preferred_element_type=F32)
    ah, al = _split_bf16(a.astype(F32))
    wh, wl = _split_bf16(w)
    return (jnp.dot(ah, wh, preferred_element_type=F32) + jnp.dot(al, wh, preferred_element_type=F32)
            + jnp.dot(ah, wl, preferred_element_type=F32))


ROW_PAD = 1


def _row_pitch(d):
    return d // LANES + ROW_PAD


def _store_row_major(ref, val):
    n, d = val.shape
    nc, pitch = d // LANES, _row_pitch(d)
    for c in range(nc):
        ref[pl.ds(c, n, stride=pitch), :] = val[:, c * LANES:(c + 1) * LANES]
    for c in range(nc, pitch):
        ref[pl.ds(c, n, stride=pitch), :] = jnp.zeros((n, LANES), F32)


def _load_row_major(ref, n):
    pitch = ref.shape[0] // n
    return jnp.concatenate([ref[pl.ds(c, n, stride=pitch), :] for c in range(pitch - ROW_PAD)], axis=1)


ROLE_TILES = 3


def _chunk(c):
    return slice(c * LANES, (c + 1) * LANES)


def _in_proj_rows_kernel(x_ref, w_ref, cos_ref, sin_ref, qkv_ref, u_ref):
    j = pl.program_id(0)
    n_qkv_tiles = N_PATTERNS * ROLE_TILES
    acc = _mm(x_ref[...], w_ref[...])
    role = j % ROLE_TILES
    is_qkv = j < n_qkv_tiles

    @pl.when(is_qkv & (role < 2))
    def _():
        cos = cos_ref[...]
        sin = sin_ref[...]
        for c in range(acc.shape[1] // LANES):
            qkv_ref[:, _chunk(c)] = _rope_chunk(acc[:, _chunk(c)], cos, sin)

    @pl.when(is_qkv & (role == 2))
    def _():
        qkv_ref[...] = acc

    @pl.when(j >= n_qkv_tiles)
    def _():
        u_ref[...] = acc


def _in_proj_rows(x, w, cos_t, sin_t):
    m, d = x.shape
    tn = ATTN_WIDTH
    assert w.shape[1] == QKV_WIDTH + POOL_WIDTH and POOL_WIDTH == tn
    n_qkv_tiles = N_PATTERNS * ROLE_TILES
    return pl.pallas_call(
        _in_proj_rows_kernel,
        grid=(n_qkv_tiles + 1,),
        in_specs=[
            pl.BlockSpec((m, d), lambda j: (0, 0)),
            pl.BlockSpec((d, tn), lambda j: (0, j)),
            pl.BlockSpec((m, LANES), lambda j: (0, 0)),
            pl.BlockSpec((m, LANES), lambda j: (0, 0)),
        ],
        out_specs=[pl.BlockSpec((m, tn), lambda j: (0, jnp.minimum(j, n_qkv_tiles - 1))),
                   pl.BlockSpec((m, tn), lambda j: (0, 0))],
        out_shape=[jax.ShapeDtypeStruct((m, QKV_WIDTH), F32), jax.ShapeDtypeStruct((m, POOL_WIDTH), F32)],
        compiler_params=_cparams(("arbitrary",)),
        name="in_proj_rows",
    )(x, w, cos_t, sin_t)


def _proj_pattern_kernel(x_ref, w_ref, cos_ref, sin_ref, out_ref, wb_ref, acc_ref, stage_ref, *, dil, n_rows):
    t = pl.program_id(0)
    n_chunks, tm, _ = stage_ref.shape

    @pl.when(t == 0)
    def _():
        acc_ref[...] = jnp.zeros(acc_ref.shape, F32)

    @pl.when(t % n_rows == 0)
    def _():
        wb_ref[...] = w_ref[...].astype(BF16)

    prev = acc_ref.at[(t + 1) % 2]
    cos = cos_ref[...]
    sin = sin_ref[...]
    for c in range(n_chunks):
        stage_ref[c] = _rope_chunk(prev[:, _chunk(c)], cos, sin)
    for r in range(dil):
        for c in range(n_chunks):
            rows = stage_ref[c, pl.ds(r, tm // dil, stride=dil), :] if dil > 1 else stage_ref[c]
            out_ref[r, :, _chunk(c)] = rows.astype(out_ref.dtype)
    acc_ref[t % 2] = jnp.dot(x_ref[...], wb_ref[...], preferred_element_type=F32)


def _proj_pattern(xb, w, tables, pi, dil, *, tm):
    m, d = xb.shape
    tn = ATTN_WIDTH
    assert m % tm == 0 and tm % (dil * 2 * SUBLANES) == 0
    n_rows = m // tm
    n = n_rows * ROLE_TILES

    def cur(t):
        return jnp.minimum(t, n - 1)

    def prv(t):
        return jnp.maximum(t - 1, 0)

    def table_map(t):
        identity = (prv(t) // n_rows == ROLE_TILES - 1).astype(jnp.int32)
        return (identity, prv(t) % n_rows, 0)

    return pl.pallas_call(
        functools.partial(_proj_pattern_kernel, dil=dil, n_rows=n_rows),
        grid=(n + 1,),
        in_specs=[
            pl.BlockSpec((tm, d), lambda t: (cur(t) % n_rows, 0)),
            pl.BlockSpec((d, tn), lambda t: (0, pi * ROLE_TILES + cur(t) // n_rows)),
            pl.BlockSpec((None, tm, LANES), table_map),
            pl.BlockSpec((None, tm, LANES), table_map),
        ],
        out_specs=pl.BlockSpec((dil, tm // dil, tn), lambda t: (0, prv(t) % n_rows, prv(t) // n_rows)),
        out_shape=jax.ShapeDtypeStruct((dil, m // dil, ROLE_TILES * tn), BF16),
        scratch_shapes=[pltpu.VMEM((d, tn), BF16), pltpu.VMEM((2, tm, tn), F32),
                        pltpu.VMEM((tn // LANES, tm, LANES), F32)],
        compiler_params=_cparams(("arbitrary",)),
        name=f"proj_p{pi}",
    )(xb, w, *tables)


def _proj_u_pool_kernel(x_ref, w_ref, w_pool_ref, scale_ref, pool_ref, xb_ref, utail_ref, wb_ref, buf_ref):
    t = pl.program_id(0)
    tm = x_ref.shape[0]

    @pl.when(t == 0)
    def _():
        wb_ref[...] = w_ref[...].astype(BF16)
        buf_ref[...] = jnp.zeros(buf_ref.shape, F32)

    prev = buf_ref.at[(t + 1) % 2]

    def store(g, val):
        pool_ref[:, g * POOL_GROUP:(g + 1) * POOL_GROUP] = val.astype(pool_ref.dtype)

    _pool_groups(prev, tm, jnp.maximum(t - 1, 0) * tm, w_pool_ref, scale_ref, store)

    xb = x_ref[...].astype(BF16)
    xb_ref[...] = xb
    u = jnp.dot(xb, wb_ref[...], preferred_element_type=F32)
    buf_ref[t % 2, HALO:HALO + tm, :] = u
    tail = u[tm - HALO:, :]
    utail_ref[...] = tail
    prev[0:HALO, :] = tail


def _proj_u_pool(x, w, w_pool_bf16, pool_scale, *, tm):
    m, d = x.shape
    assert m % tm == 0 and tm % HALO == 0
    n = m // tm
    n_qkv_tiles = N_PATTERNS * ROLE_TILES
    cur = lambda t: (jnp.minimum(t, n - 1), 0)
    prv = lambda t: (jnp.maximum(t - 1, 0), 0)
    return pl.pallas_call(
        _proj_u_pool_kernel,
        grid=(n + 1,),
        in_specs=[
            pl.BlockSpec((tm, d), cur),
            pl.BlockSpec((d, POOL_WIDTH), lambda t: (0, n_qkv_tiles), pipeline_mode=pl.Buffered(1)),
            pl.BlockSpec((len(POOL_WINDOWS), POOL_GROUP, POOL_GROUP), lambda t: (0, 0, 0)),
            pl.BlockSpec((1, POOL_WIDTH), lambda t: (0, 0)),
        ],
        out_specs=[
            pl.BlockSpec((tm, POOL_WIDTH), prv),
            pl.BlockSpec((tm, d), cur),
            pl.BlockSpec((HALO, POOL_WIDTH), lambda t: (0, 0)),
        ],
        out_shape=[
            jax.ShapeDtypeStruct((m, POOL_WIDTH), BF16),
            jax.ShapeDtypeStruct((m, d), BF16),
            jax.ShapeDtypeStruct((HALO, POOL_WIDTH), F32),
        ],
        scratch_shapes=[pltpu.VMEM((d, POOL_WIDTH), BF16), pltpu.VMEM((2, HALO + tm, POOL_WIDTH), F32)],
        compiler_params=_cparams(("arbitrary",)),
        name="proj_u_pool",
    )(x, w, w_pool_bf16, pool_scale)


def _attn_kernel(q_ref, kp_ref, kc_ref, vp_ref, vc_ref, o_ref, lse_ref,
                 lim_ref, s_ref, e_ref, m_ref, den_ref, *, r_max, dil):
    n = pl.program_id(0)
    r = pl.program_id(1)
    qi = lax.broadcasted_iota(jnp.int32, (BAND, 2 * BAND), 0)
    ki = lax.broadcasted_iota(jnp.int32, (BAND, 2 * BAND), 1)
    dist = BAND + qi - ki
    valid = (dist >= 0) & (dist <= r_max) & ((n > 0) | (ki >= BAND))
    lim_ref[...] = jnp.where(valid, jnp.float32(jnp.finfo(F32).max), jnp.float32(NEG_INF))

    lane = lax.broadcasted_iota(jnp.int32, (BAND, LANES), 1)
    heads_per_chunk = LANES // HEAD_DIM
    n_chunks = ATTN_WIDTH // LANES
    lane_head = lane >> (HEAD_DIM.bit_length() - 1)
    qscale = [jnp.where(lane_head == hh, HEAD_DIM ** -0.5, 0.0).astype(BF16) for hh in range(heads_per_chunk)]
    nt = (((1,), (1,)), ((), ()))

    for c in range(n_chunks):
        sl = slice(c * LANES, (c + 1) * LANES)
        q2 = q_ref[:, sl]
        kp, kc = kp_ref[:, sl], kc_ref[:, sl]
        for hh in range(heads_per_chunk):
            h = c * heads_per_chunk + hh
            qm = q2 * qscale[hh]
            s = jnp.concatenate(
                [lax.dot_general(qm, kp, nt, preferred_element_type=F32),
                 lax.dot_general(qm, kc, nt, preferred_element_type=F32)], axis=1)
            s = jnp.minimum(s, lim_ref[...])
            s_ref[h] = s
            m_ref[h] = jnp.broadcast_to(jnp.max(s, axis=-1, keepdims=True), (BAND, LANES))

    lse_acc = jnp.zeros((BAND, LANES), F32)
    for h in range(N_HEADS):
        m = m_ref[h]
        e = jnp.exp(s_ref[h] - jnp.concatenate([m, m], axis=1))
        e_ref[h] = e.astype(BF16)
        den = jnp.broadcast_to(jnp.sum(e, axis=-1, keepdims=True), (BAND, LANES))
        den_ref[h] = den
        lse_acc = jnp.where(lane == h, m + jnp.log(den), lse_acc)

    rows = slice(None) if dil == 1 else pl.ds(r, BAND, stride=dil)
    lse_ref[rows, :] = lse_acc
    for c in range(n_chunks):
        sl = slice(c * LANES, (c + 1) * LANES)
        vp, vc = vp_ref[:, sl], vc_ref[:, sl]
        o2 = jnp.zeros((BAND, LANES), F32)
        for hh in range(heads_per_chunk):
            h = c * heads_per_chunk + hh
            pv = (jnp.dot(e_ref[h, :, :BAND], vp, preferred_element_type=F32)
                  + jnp.dot(e_ref[h, :, BAND:], vc, preferred_element_type=F32))
            o2 = jnp.where(lane_head == hh, pv / den_ref[h], o2)
        o_ref[c, rows, :] = o2


def _attn_pattern(qkv, pi, dil, window):
    s = qkv.shape[0] * qkv.shape[1]
    assert qkv.shape[0] == dil and s % (dil * BAND) == 0
    nb = s // (dil * BAND)
    r_max = window // dil

    def cur(role):
        return lambda n, r: (r, n, role)

    def prev(role):
        return lambda n, r: (r, jnp.maximum(n - 1, 0), role)

    blk = (None, BAND, ATTN_WIDTH)
    span = BAND * dil
    return pl.pallas_call(
        functools.partial(_attn_kernel, r_max=r_max, dil=dil),
        grid=(nb, dil),
        in_specs=[
            pl.BlockSpec(blk, cur(0)),
            pl.BlockSpec(blk, prev(1)),
            pl.BlockSpec(blk, cur(1)),
            pl.BlockSpec(blk, prev(2)),
            pl.BlockSpec(blk, cur(2)),
        ],
        out_specs=[
            pl.BlockSpec((ATTN_WIDTH // LANES, span, LANES), lambda n, r: (0, n, 0)),
            pl.BlockSpec((span, LANES), lambda n, r: (n, 0)),
        ],
        out_shape=[
            jax.ShapeDtypeStruct((ATTN_WIDTH // LANES, s, LANES), F32),
            jax.ShapeDtypeStruct((s, LANES), F32),
        ],
        scratch_shapes=[
            pltpu.VMEM((BAND, 2 * BAND), F32),
            pltpu.VMEM((N_HEADS, BAND, 2 * BAND), F32),
            pltpu.VMEM((N_HEADS, BAND, 2 * BAND), BF16),
            pltpu.VMEM((N_HEADS, BAND, LANES), F32),
            pltpu.VMEM((N_HEADS, BAND, LANES), F32),
        ],
        compiler_params=_cparams(("arbitrary", "arbitrary")),
        name=f"attn_p{pi}",
    )(qkv, qkv, qkv, qkv, qkv)


def _pool_groups(buf_ref, tm, pos0, w_pool_ref, scale_ref, store):
    row = lax.broadcasted_iota(jnp.int32, (tm, 1), 0) + pos0
    for g, win in enumerate(POOL_WINDOWS):
        sl = slice(g * POOL_GROUP, (g + 1) * POOL_GROUP)
        cur = buf_ref[HALO:HALO + tm, sl]
        acc = cur
        for jj in range(1, win):
            acc = acc + buf_ref[HALO - jj:HALO - jj + tm, sl]
        cnt = jnp.minimum(row + 1, win).astype(F32)
        d = acc / cnt - cur
        y = _mm(d, w_pool_ref[g])
        store(g, y * scale_ref[:, sl])


def _sample_attn_kernel(zt_ref, c_ref, cn_ref, ot_ref, lt_ref, s_ref, *, dil, hps, length):
    b = pl.program_id(0)
    g = pl.program_id(1)

    @pl.when((b == 0) & (g == 0))
    def _():
        ot_ref[...] = jnp.zeros(ot_ref.shape, F32)
        lt_ref[...] = jnp.zeros(lt_ref.shape, F32)

    rows = hps * HEAD_DIM
    base = pl.multiple_of(g * rows, rows)
    lane_z = lax.broadcasted_iota(jnp.int32, (rows, zt_ref.shape[1]), 1)

    def column(role):
        blk = zt_ref[pl.ds(role * ATTN_WIDTH + base, rows), :]
        return jnp.sum(jnp.where(lane_z == b, blk, 0.0), axis=-1, keepdims=True)

    qc, knc, vnc = column(0), column(1), column(2)
    scale = HEAD_DIM ** -0.5
    lane = lax.broadcasted_iota(jnp.int32, (1, length), 1)
    key_ok = (lane & (dil - 1)) == 0
    last = lane == length - 1
    lane_o = lax.broadcasted_iota(jnp.int32, (HEAD_DIM, LANES), 1)
    lane_l = lax.broadcasted_iota(jnp.int32, (hps, LANES), 1)

    def head_rows(h):
        return slice(h * HEAD_DIM, (h + 1) * HEAD_DIM)

    for h in range(hps):
        k = c_ref[0, h]
        s_ref[h:h + 1, :] = jnp.sum(k * qc[head_rows(h)], axis=0, keepdims=True)
        cn_ref[0, h] = jnp.where(last, knc[head_rows(h)], pltpu.roll(k, length - 1, 1))

    s = jnp.where(key_ok, s_ref[...] * scale, NEG_INF)
    s_n = jnp.sum((knc * qc).reshape(hps, HEAD_DIM, 1), axis=1) * scale
    m = jnp.maximum(jnp.max(s, axis=-1, keepdims=True), s_n)
    e = jnp.exp(s - m)
    e_n = jnp.exp(s_n - m)
    den = jnp.sum(e, axis=-1, keepdims=True) + e_n
    s_ref[...] = e
    lse = m + jnp.log(den)
    head0 = pl.multiple_of(g * hps, hps)
    lt_ref[pl.ds(head0, hps), :] = jnp.where(lane_l == b, lse, lt_ref[pl.ds(head0, hps), :])

    for h in range(hps):
        v = c_ref[1, h]
        vn = vnc[head_rows(h)]
        o = (jnp.sum(v * s_ref[h:h + 1, :], axis=-1, keepdims=True) + vn * e_n[h:h + 1, :]) / den[h:h + 1, :]
        cn_ref[1, h] = jnp.where(last, vn, pltpu.roll(v, length - 1, 1))
        r0 = pl.multiple_of(base + h * HEAD_DIM, HEAD_DIM)
        ot_ref[pl.ds(r0, HEAD_DIM), :] = jnp.where(lane_o == b, o, ot_ref[pl.ds(r0, HEAD_DIM), :])


def _sample_attn(zt, c_t, pi, dil):
    db, _, _, _, length = c_t.shape
    assert db <= LANES and length == dil * BAND
    hps = min(N_HEADS, max(1, (4 * 2048) // length))
    blk = (None, 2, hps, HEAD_DIM, length)
    return pl.pallas_call(
        functools.partial(_sample_attn_kernel, dil=dil, hps=hps, length=length),
        grid=(db, N_HEADS // hps),
        in_specs=[
            pl.BlockSpec((3 * ATTN_WIDTH, db), lambda b, g: (pi, 0)),
            pl.BlockSpec(blk, lambda b, g: (b, 0, g, 0, 0)),
        ],
        out_specs=[
            pl.BlockSpec(blk, lambda b, g: (b, 0, g, 0, 0)),
            pl.BlockSpec((ATTN_WIDTH, LANES), lambda b, g: (0, 0)),
            pl.BlockSpec((LANES, LANES), lambda b, g: (0, 0)),
        ],
        out_shape=[
            jax.ShapeDtypeStruct(c_t.shape, c_t.dtype),
            jax.ShapeDtypeStruct((ATTN_WIDTH, LANES), F32),
            jax.ShapeDtypeStruct((LANES, LANES), F32),
        ],
        scratch_shapes=[pltpu.VMEM((hps, length), F32)],
        compiler_params=_cparams(("arbitrary", "arbitrary")),
        name=f"sample_attn_p{pi}",
    )(zt, c_t)


def _sample_pool_kernel(u_ref, sp_ref, w_pool_ref, scale_ref, pool_ref, buf_ref):
    buf_ref[...] = jnp.zeros(buf_ref.shape, F32)
    buf_ref[HALO - POOL_STATE:HALO, :] = sp_ref[...]
    buf_ref[HALO:HALO + 1, :] = u_ref[...]

    def store(g, val):
        pool_ref[:, g * POOL_GROUP:(g + 1) * POOL_GROUP] = val[0:1]

    _pool_groups(buf_ref, HALO, PAST_LEN, w_pool_ref, scale_ref, store)


def _head_expand():
    dim_head = jnp.arange(ATTN_WIDTH, dtype=jnp.int32) // HEAD_DIM
    return (jnp.arange(LANES, dtype=jnp.int32)[:, None] == dim_head[None, :]).astype(F32)


def _sample_pool(us, state_pool, w_pool_bf16, pool_scale):
    db = us.shape[0]
    pool = pl.pallas_call(
        _sample_pool_kernel,
        grid=(db,),
        in_specs=[
            pl.BlockSpec((None, 1, POOL_WIDTH), lambda b: (b, 0, 0)),
            pl.BlockSpec((None, POOL_STATE, POOL_WIDTH), lambda b: (b, 0, 0)),
            pl.BlockSpec((len(POOL_WINDOWS), POOL_GROUP, POOL_GROUP), lambda b: (0, 0, 0)),
            pl.BlockSpec((1, POOL_WIDTH), lambda b: (0, 0)),
        ],
        out_specs=pl.BlockSpec((None, 1, POOL_WIDTH), lambda b: (b, 0, 0)),
        out_shape=jax.ShapeDtypeStruct((db, 1, POOL_WIDTH), F32),
        scratch_shapes=[pltpu.VMEM((2 * HALO, POOL_WIDTH), F32)],
        compiler_params=_cparams(("arbitrary",)),
        name="sample_pool",
    )(us.reshape(db, 1, POOL_WIDTH), state_pool, w_pool_bf16, pool_scale)
    return pool.reshape(db, POOL_WIDTH)


def _layer_norm(xf, g, b):
    mu = jnp.mean(xf, axis=-1, keepdims=True)
    xc = xf - mu
    var = jnp.mean(xc * xc, axis=-1, keepdims=True)
    return xc * lax.rsqrt(var + LN_EPS) * g + b


def _route(logits):
    lane = lax.broadcasted_iota(jnp.int32, logits.shape, 1)
    lane_f = lane.astype(F32)
    neg = jnp.float32(-jnp.inf)

    def first_argmax(v, vmax):
        return jnp.min(jnp.where(v == vmax, lane_f, jnp.float32(LANES)), axis=-1, keepdims=True)

    gl = jnp.where(lane < N_GROUPS, logits, neg)
    gmax = jnp.max(gl, axis=-1, keepdims=True)
    g_sel = first_argmax(gl, gmax)
    g_gate = 1.0 / jnp.sum(jnp.exp(gl - gmax), axis=-1, keepdims=True)
    lane_group = (((lane + (EXPERTS_PER_GROUP - N_GROUPS)) >> 3) - 1).astype(F32)
    el = jnp.where(lane_group == g_sel, logits, neg)
    v1 = jnp.max(el, axis=-1, keepdims=True)
    i1 = first_argmax(el, v1)
    el2 = jnp.where(lane_f == i1, neg, el)
    v2 = jnp.max(el2, axis=-1, keepdims=True)
    i2 = first_argmax(el2, v2)
    t = jnp.exp(v2 - v1)
    w1 = (1.0 / (1.0 + t)) * g_gate
    w2 = (t / (1.0 + t)) * g_gate
    route = jnp.where(lane == 0, i1 - N_GROUPS, 0.0)
    route = jnp.where(lane == 1, i2 - N_GROUPS, route)
    route = jnp.where(lane == 2, w1, route)
    route = jnp.where(lane == 3, w2, route)
    return route


def _post_kernel(o0_ref, o1_ref, o2_ref, l0_ref, l1_ref, l2_ref, pool_ref, x_ref, hexp_ref,
                 wo_ref, g1_ref, b1_ref, wr_ref, br_ref, *rest, n_row_tiles):
    x1_ref, route_ref, wrh_ref, wrl_ref = rest[-4:]
    i = pl.program_id(0)

    @pl.when(i == 0)
    def _():
        wrh_ref[...], wrl_ref[...] = _split_bf16(wr_ref[...])

    @pl.when(i < n_row_tiles)
    def _():
        l0, l1, l2 = l0_ref[...], l1_ref[...], l2_ref[...]
        lmax = jnp.maximum(jnp.maximum(l0, l1), l2)
        e0, e1, e2 = jnp.exp(l0 - lmax), jnp.exp(l1 - lmax), jnp.exp(l2 - lmax)
        den = e0 + e1 + e2
        hexp = hexp_ref[...]
        attn = jnp.zeros((x_ref.shape[0], ATTN_WIDTH), F32)
        for o_ref, e in ((o0_ref, e0), (o1_ref, e1), (o2_ref, e2)):
            wh, wl = _split_bf16(e / den)
            w_x = (jnp.dot(wh, hexp, preferred_element_type=F32) + jnp.dot(wl, hexp, preferred_element_type=F32))
            o = jnp.concatenate([o_ref[c] for c in range(ATTN_WIDTH // LANES)], axis=1)
            attn = attn + o * w_x
        h = _mm(attn, wo_ref[0:ATTN_WIDTH, :]) + _mm(pool_ref[...], wo_ref[ATTN_WIDTH:, :])
        x1 = _layer_norm(ALPHA * x_ref[...] + h, g1_ref[...], b1_ref[...])
        _store_row_major(x1_ref, x1)
        xh, xl = _split_bf16(x1)
        wrh = wrh_ref[...]
        logits = (jnp.dot(xh, wrh, preferred_element_type=F32) + jnp.dot(xl, wrh, preferred_element_type=F32)
                  + jnp.dot(xh, wrl_ref[...], preferred_element_type=F32)) + br_ref[...]
        route_ref[...] = _route(logits)

    @pl.when(i >= n_row_tiles)
    def _():
        x1_ref[...] = jnp.zeros(x1_ref.shape, F32)


def _const_spec(shape):
    return pl.BlockSpec(shape, lambda i: (0,) * len(shape), pipeline_mode=pl.Buffered(1))


def _post(os_, lses, pool, x, wo, g1, b1, wr, br, *, tm, name, filler_tiles=0, x1_into=None, row0=0):
    m, d = x.shape
    n = m // tm
    pitch = _row_pitch(d)
    assert (tm * pitch) % SUBLANES == 0
    hexp = _head_expand().astype(BF16)
    last = n - 1

    def rows(width):
        return pl.BlockSpec((tm, width), lambda i: (jnp.minimum(i, last), 0))

    in_specs = ([pl.BlockSpec((ATTN_WIDTH // LANES, tm, LANES), lambda i: (0, jnp.minimum(i, last), 0))] * 3
                + [rows(LANES)] * 3
                + [rows(POOL_WIDTH), rows(d), _const_spec((LANES, ATTN_WIDTH)),
                   _const_spec((d, d)), _const_spec((1, d)), _const_spec((1, d)),
                   _const_spec((d, LANES)), _const_spec((1, LANES))])
    operands = [*os_, *lses, pool, x, hexp, wo, g1, b1, wr, br]
    aliases = {}
    if x1_into is None:
        x1_shape = jax.ShapeDtypeStruct(((m + filler_tiles * tm) * pitch, LANES), F32)
    else:
        assert filler_tiles == 0 and row0 % tm == 0
        in_specs.append(pl.BlockSpec(memory_space=pl.ANY))
        operands.append(x1_into)
        aliases = {len(operands) - 1: 0}
        x1_shape = jax.ShapeDtypeStruct(x1_into.shape, F32)
    return pl.pallas_call(
        functools.partial(_post_kernel, n_row_tiles=n),
        grid=(n + filler_tiles,),
        in_specs=in_specs,
        out_specs=[pl.BlockSpec((tm * pitch, LANES), lambda i: (row0 // tm + i, 0)), rows(LANES)],
        out_shape=[x1_shape, jax.ShapeDtypeStruct((m, LANES), F32)],
        input_output_aliases=aliases,
        scratch_shapes=[pltpu.VMEM((d, LANES), BF16), pltpu.VMEM((d, LANES), BF16)],
        compiler_params=_cparams(("arbitrary",)),
        name=name,
    )(*operands)


def _routing_plan(pair_expert, n_tiles):
    p = pair_expert.shape[0]
    experts = jnp.arange(N_EXPERTS, dtype=jnp.int32)
    onehot = (pair_expert[:, None] == experts[None, :]).astype(jnp.int32)
    csum = jnp.cumsum(onehot, axis=0)
    rank = jnp.take_along_axis(csum, pair_expert[:, None], axis=1)[:, 0] - 1
    counts = csum[-1]
    tiles_per = (counts + MOE_TILE - 1) // MOE_TILE
    tile_end = jnp.cumsum(tiles_per)
    tile_start = tile_end - tiles_per
    dest = (tile_start[pair_expert] * MOE_TILE + rank).astype(jnp.int32)
    n_used = tile_end[-1]
    tile_ids = jnp.arange(n_tiles, dtype=jnp.int32)
    tile_expert = jnp.sum((tile_ids[:, None] >= tile_end[None, :]).astype(jnp.int32), axis=1)
    last_expert = jnp.sum((n_used - 1 >= tile_end).astype(jnp.int32))
    tile_expert = jnp.where(tile_ids < n_used, tile_expert, last_expert).astype(jnp.int32)
    slot_token = jnp.zeros((n_tiles * MOE_TILE,), jnp.int32).at[dest].set(
        jnp.arange(p, dtype=jnp.int32) // 2, unique_indices=True, mode="promise_in_bounds")
    return dest, slot_token, tile_expert, n_used.reshape(1).astype(jnp.int32)


def _moe_kernel(te_ref, nu_ref, st_ref, x_ref, wg_ref, wu_ref, wd_ref, ys_ref,
                buf_ref, wgb_ref, wub_ref, wdb_ref, sem_ref, *, nc, pitch):
    c = pl.program_id(0)
    n_tiles = pl.num_programs(0)
    n_used = nu_ref[0]

    def issue(tile, slot):
        def body(g, carry):
            for k in range(GATHER_UNROLL):
                i = g * GATHER_UNROLL + k
                src = st_ref[tile * MOE_TILE + i]
                pltpu.make_async_copy(x_ref.at[pl.ds(src, nc), :], buf_ref.at[slot, pl.ds(i * pitch, nc), :],
                                      sem_ref.at[slot]).start(priority=k % 2)
            return carry

        lax.fori_loop(0, MOE_TILE // GATHER_UNROLL, body, 0)

    def wait(slot):
        pltpu.make_async_copy(x_ref.at[pl.ds(0, MOE_TILE * nc), :], buf_ref.at[slot, pl.ds(0, MOE_TILE * nc), :],
                              sem_ref.at[slot]).wait()

    slot = c % 2

    @pl.when((c == 0) & (n_used > 0))
    def _():
        issue(0, 0)

    @pl.when((c + 1 < n_tiles) & (c + 1 < n_used))
    def _():
        issue(c + 1, 1 - slot)

    prev_expert = te_ref[jnp.maximum(c - 1, 0)]

    @pl.when((c == 0) | (te_ref[c] != prev_expert))
    def _():
        wgb_ref[...] = wg_ref[...].astype(BF16)
        wub_ref[...] = wu_ref[...].astype(BF16)
        wdb_ref[...] = wd_ref[...].astype(BF16)

    @pl.when(c < n_used)
    def _():
        wait(slot)
        x = _load_row_major(buf_ref.at[slot], MOE_TILE).astype(BF16)
        gate = jnp.dot(x, wgb_ref[...], preferred_element_type=F32)
        up = jnp.dot(x, wub_ref[...], preferred_element_type=F32)
        h = jax.nn.silu(gate) * up
        _store_row_major(ys_ref, jnp.dot(h.astype(BF16), wdb_ref[...], preferred_element_type=F32))

    @pl.when(c >= n_used)
    def _():
        ys_ref[...] = jnp.zeros(ys_ref.shape, ys_ref.dtype)


def _moe(x1, slot_row, tile_expert, n_used, w_gate, w_up, w_down, n_tiles):
    d = w_gate.shape[1]
    nc, pitch = d // LANES, _row_pitch(d)
    f = w_gate.shape[-1]
    grid_spec = pltpu.PrefetchScalarGridSpec(
        num_scalar_prefetch=3,
        grid=(n_tiles,),
        in_specs=[
            pl.BlockSpec(memory_space=pl.ANY),
            pl.BlockSpec((None, d, f), lambda c, te, nu, st: (te[c], 0, 0)),
            pl.BlockSpec((None, d, f), lambda c, te, nu, st: (te[c], 0, 0)),
            pl.BlockSpec((None, f, d), lambda c, te, nu, st: (te[c], 0, 0)),
        ],
        out_specs=pl.BlockSpec((MOE_TILE * pitch, LANES), lambda c, te, nu, st: (c, 0)),
        scratch_shapes=[
            pltpu.VMEM((2, MOE_TILE * pitch, LANES), F32),
            pltpu.VMEM((d, f), BF16),
            pltpu.VMEM((d, f), BF16),
            pltpu.VMEM((f, d), BF16),
            pltpu.SemaphoreType.DMA((2,)),
        ],
    )
    return pl.pallas_call(
        functools.partial(_moe_kernel, nc=nc, pitch=pitch),
        grid_spec=grid_spec,
        out_shape=jax.ShapeDtypeStruct((n_tiles * MOE_TILE * pitch, LANES), F32),
        compiler_params=_cparams(("arbitrary",)),
        name="moe",
    )(tile_expert, n_used, slot_row, x1, w_gate, w_up, w_down)


def _final_kernel(dest_ref, ys_ref, x1_ref, route_ref, p_ref, g2_ref, b2_ref, wpg_ref, wp_ref, o_ref,
                  buf_ref, sem_ref, *, tm, pair0, nc, pitch):
    i = pl.program_id(0)
    n_steps = pl.num_programs(0)

    def issue(step, slot):
        def body(r, carry):
            for k in range(2):
                src = dest_ref[pair0 + (step * tm + r) * 2 + k]
                pltpu.make_async_copy(ys_ref.at[pl.ds(src, nc), :], buf_ref.at[slot, k, pl.ds(r * pitch, nc), :],
                                      sem_ref.at[slot]).start()
            return carry

        lax.fori_loop(0, tm, body, 0, unroll=GATHER_UNROLL)

    def wait(slot):
        for k in range(2):
            pltpu.make_async_copy(ys_ref.at[pl.ds(0, tm * nc), :], buf_ref.at[slot, k, pl.ds(0, tm * nc), :],
                                  sem_ref.at[slot]).wait()

    slot = i % 2

    @pl.when(i == 0)
    def _():
        issue(0, 0)

    @pl.when(i + 1 < n_steps)
    def _():
        issue(i + 1, 1 - slot)

    wait(slot)
    route = route_ref[...]
    y = (route[:, 2:3] * _load_row_major(buf_ref.at[slot, 0], tm)
         + route[:, 3:4] * _load_row_major(buf_ref.at[slot, 1], tm))
    x2 = _layer_norm(ALPHA * _load_row_major(x1_ref, tm) + y, g2_ref[...], b2_ref[...])
    gate = jax.nn.sigmoid(jnp.dot(x2.astype(BF16), wpg_ref[...], preferred_element_type=F32))
    ple = jnp.dot(p_ref[...].astype(BF16), wp_ref[...], preferred_element_type=F32)
    o_ref[...] = x2 + gate * ple


def _final(dest, ys, x1, route, p, g2, b2, wpg, wp, *, tm, row0):
    m, pd = p.shape
    d = wpg.shape[0]
    nc, pitch = d // LANES, _row_pitch(d)
    assert m % tm == 0 and row0 % tm == 0
    pair0 = 2 * row0
    grid_spec = pltpu.PrefetchScalarGridSpec(
        num_scalar_prefetch=1,
        grid=(m // tm,),
        in_specs=[
            pl.BlockSpec(memory_space=pl.ANY),
            pl.BlockSpec((tm * pitch, LANES), lambda i, dst: (row0 // tm + i, 0)),
            pl.BlockSpec((tm, LANES), lambda i, dst: (i, 0)),
            pl.BlockSpec((tm, pd), lambda i, dst: (i, 0)),
            pl.BlockSpec((1, d), lambda i, dst: (0, 0)),
            pl.BlockSpec((1, d), lambda i, dst: (0, 0)),
            pl.BlockSpec((d, d), lambda i, dst: (0, 0)),
            pl.BlockSpec((pd, d), lambda i, dst: (0, 0)),
        ],
        out_specs=pl.BlockSpec((tm, d), lambda i, dst: (i, 0)),
        scratch_shapes=[pltpu.VMEM((2, 2, tm * pitch, LANES), F32), pltpu.SemaphoreType.DMA((2,))],
    )
    return pl.pallas_call(
        functools.partial(_final_kernel, tm=tm, pair0=pair0, nc=nc, pitch=pitch),
        grid_spec=grid_spec,
        out_shape=jax.ShapeDtypeStruct((m, d), F32),
        compiler_params=_cparams(("arbitrary",)),
        name="final",
    )(dest, ys, x1, route, p, g2, b2, wpg, wp)


def _pick_tile(m, pref):
    t = pref
    while m % t:
        t //= 2
    return t


def _layer(xp, xs, caches, state_pool, pp, ps, w):
    s, d = xp.shape
    db = xs.shape[0]

    cos_p, sin_p = _rope_tables(jnp.arange(s, dtype=jnp.int32))
    dils = tuple(dil for _, dil in DILATED_PATTERNS)
    tables = (jnp.stack([cos_p, jnp.ones_like(cos_p)]), jnp.stack([sin_p, jnp.zeros_like(sin_p)]))
    tm_in = _pick_tile(s, TM_POOL_PROJ)
    pool_p, xp_b, u_tail = _proj_u_pool(xp, w["w_in_f32"], w["w_pool"], w["pool_scale"], tm=tm_in)
    tm_pat = _pick_tile(s, TM_PATTERN_PROJ)
    qkv_p = [_proj_pattern(xp_b, w["w_in_f32"], tables, pi, dil, tm=tm_pat) for pi, dil in enumerate(dils)]
    os_, lses = [], []
    for pi, (window, dil) in enumerate(DILATED_PATTERNS):
        o, l = _attn_pattern(qkv_p[pi], pi, dil, window)
        os_.append(o)
        lses.append(l)

    cos_s, sin_s = _rope_tables(jnp.full((db,), PAST_LEN, jnp.int32))
    qkv_s, u_s = _in_proj_rows(xs, w["w_in_f32"], cos_s, sin_s)
    zt_s = qkv_s.T
    os_s, lses_s, kv_s = [], [], []
    for pi, ((window, dil), c) in enumerate(zip(DILATED_PATTERNS, caches)):
        cn, ot, lt = _sample_attn(zt_s, jnp.transpose(c, (0, 2, 3, 4, 1)), pi, dil)
        kv_s.append(jnp.transpose(cn, (0, 4, 1, 2, 3)))
        os_s.append(jnp.transpose(ot.T[:db].reshape(db, ATTN_WIDTH // LANES, LANES), (1, 0, 2)))
        lses_s.append(lt.T[:db])
    pool_s = _sample_pool(u_s, state_pool, w["w_pool_f32"], w["pool_scale"])

    post_w = (w["ln1_g"], w["ln1_b"], w["w_router"], w["b_router"])
    tm_p = _pick_tile(s, TM_ROWWISE)
    assert s % db == 0 and tm_p >= db
    x1, route_p = _post(os_, lses, pool_p, xp, w["w_out"], *post_w, tm=tm_p, name="post_prompt",
                        filler_tiles=1)
    x1, route_s = _post(os_s, lses_s, pool_s, xs, w["w_out_f32"], *post_w, tm=db, name="post_sample",
                        x1_into=x1, row0=s)

    pair_expert = jnp.concatenate([route_p[:, 0:2].reshape(-1), route_s[:, 0:2].reshape(-1)]).astype(jnp.int32)
    n_pairs = pair_expert.shape[0]
    n_tiles = -(-n_pairs // MOE_TILE) + N_EXPERTS
    dest, slot_token, tile_expert, n_used = _routing_plan(pair_expert, n_tiles)
    pitch = _row_pitch(d)
    ys = _moe(x1, slot_token * pitch, tile_expert, n_used, w["w_gate"], w["w_up"], w["w_down"], n_tiles)

    fin_w = (w["ln2_g"], w["ln2_b"], w["w_ple_gate"], w["w_ple"])
    y_p = _final(dest * pitch, ys, x1, route_p, pp, *fin_w, tm=tm_p, row0=0)
    y_s = _final(dest * pitch, ys, x1, route_s, ps, *fin_w, tm=db, row0=s)

    kv_p = []
    for pi, (window, dil) in enumerate(DILATED_PATTERNS):
        keep = min(window, s)
        kv = qkv_p[pi][:, (s - keep) // dil:, ATTN_WIDTH:3 * ATTN_WIDTH].astype(F32)
        kv = jnp.transpose(kv, (1, 0, 2))
        kv_p.append(kv.reshape(1, keep, 2, N_HEADS, HEAD_DIM))
    pool_state_p = u_tail[HALO - POOL_STATE:][None]
    pool_state_s = jnp.concatenate([state_pool[:, 1:], u_s[:, None, :]], axis=1)
    return y_p, y_s, kv_p, pool_state_p, kv_s, pool_state_s


def kernel(x_prompt, x_sample, cache_kv_w128_d1, cache_kv_w512_d4, cache_kv_w2048_d16, state_pool, p_prompt, p_sample, w_in, w_out, w_pool, pool_scale, ln1_g, ln1_b, w_group_router, b_group_router, w_expert_router, b_expert_router, w_gate, w_up, w_down, ln2_g, ln2_b, w_ple, w_ple_gate):
    assert w_in.shape[0] == DEPTH == 1 and x_prompt.shape[0] == 1 and x_sample.shape[1] == 1
    d = x_prompt.shape[-1]
    pad = LANES - N_GROUPS - N_EXPERTS
    w = {
        "w_in_f32": w_in[0],
        "w_out": w_out[0].astype(BF16), "w_out_f32": w_out[0],
        "w_pool": w_pool[0].astype(BF16), "w_pool_f32": w_pool[0],
        "pool_scale": pool_scale[0].reshape(1, POOL_WIDTH),
        "ln1_g": ln1_g[0].reshape(1, d), "ln1_b": ln1_b[0].reshape(1, d),
        "ln2_g": ln2_g[0].reshape(1, d), "ln2_b": ln2_b[0].reshape(1, d),
        "w_router": jnp.concatenate([w_group_router[0], w_expert_router[0], jnp.zeros((d, pad), F32)], axis=1),
        "b_router": jnp.concatenate([b_group_router[0], b_expert_router[0], jnp.zeros((pad,), F32)]).reshape(1, LANES),
        "w_gate": w_gate[0], "w_up": w_up[0], "w_down": w_down[0],
        "w_ple": w_ple[0].astype(BF16),
        "w_ple_gate": w_ple_gate[0].astype(BF16),
    }
    caches = [cache_kv_w128_d1[0], cache_kv_w512_d4[0], cache_kv_w2048_d16[0]]
    y_p, y_s, kv_p, pool_p, kv_s, pool_s = _layer(
        x_prompt[0], x_sample[:, 0], caches, state_pool[0], p_prompt[0, 0], p_sample[0, :, 0], w)
    return (y_p[None], y_s[:, None], kv_p[0][None], kv_p[1][None], kv_p[2][None], pool_p[None],
            kv_s[0][None], kv_s[1][None], kv_s[2][None], pool_s[None])
```
